```python
import jax
import jax.numpy as jnp
from jax import lax
import numpy as np


D_MODEL = 1024
BATCH = 1
SEQ = 16384
DEPTH = 2

CHUNK = 64
D_MIX = D_MODEL
EPS = 1e-6
NEG_INF = -1e30

SGU_WIDTH = D_MIX // 4
SGU_HEADS = 4
SGU_HEAD_DIM = SGU_WIDTH // SGU_HEADS
SGU_BLOCK = 128

POOL_WIDTH = D_MIX // 4
POOL_WINDOWS = (2, 4, 8, 16)
POOL_GROUPS = len(POOL_WINDOWS)
POOL_GROUP_DIM = POOL_WIDTH // POOL_GROUPS

MLA_WIDTH = D_MIX // 2
MLA_HEADS = 4
V_HEAD_DIM = MLA_WIDTH // MLA_HEADS
QK_NOPE_DIM = 128
QK_ROPE_DIM = 64
QK_HEAD_DIM = QK_NOPE_DIM + QK_ROPE_DIM
Q_LORA_RANK = 384
KV_LORA_RANK = 256
ROPE_BASE = 10000.0
Q_BLOCK = 128

IN_SPLITS = (SGU_WIDTH, SGU_WIDTH, SGU_WIDTH, POOL_WIDTH, POOL_WIDTH, Q_LORA_RANK, KV_LORA_RANK, QK_ROPE_DIM, MLA_WIDTH)
D_IN = sum(IN_SPLITS)

kernel_name = "hybrid_sgu_pool_mla_block"


def rms_norm(x, g):
    xf = x.astype(jnp.float32)
    y = xf * lax.rsqrt(jnp.mean(xf * xf, axis=-1, keepdims=True) + EPS)
    return (y * g.astype(jnp.float32)).astype(x.dtype)


def layer_norm(x, g, b):
    xf = x.astype(jnp.float32)
    mu = jnp.mean(xf, axis=-1, keepdims=True)
    var = jnp.mean(jnp.square(xf - mu), axis=-1, keepdims=True)
    y = (xf - mu) * lax.rsqrt(var + EPS)
    return (y * g.astype(jnp.float32) + b.astype(jnp.float32)).astype(x.dtype)


def rope_tables(positions):
    inv_freq = ROPE_BASE ** (-jnp.arange(0, QK_ROPE_DIM, 2, dtype=jnp.float32) / QK_ROPE_DIM)
    ang = positions.astype(jnp.float32)[..., None] * inv_freq
    return jnp.cos(ang)[:, :, None, :], jnp.sin(ang)[:, :, None, :]


def apply_rope(x, cos, sin):
    xf = x.astype(jnp.float32)
    x1, x2 = jnp.split(xf, 2, axis=-1)
    return jnp.concatenate([x1 * cos - x2 * sin, x2 * cos + x1 * sin], axis=-1).astype(x.dtype)


def sgu_mixer(u, v, w_s, b_s, ln_g, ln_b):
    bsz, seq, _ = v.shape
    v = layer_norm(v, ln_g, ln_b)
    vb = v.reshape(bsz, seq // SGU_BLOCK, SGU_BLOCK, SGU_HEADS, SGU_HEAD_DIM)
    pos_chunk = jnp.arange(SGU_BLOCK) // CHUNK
    mask = (pos_chunk[None, :] <= pos_chunk[:, None]).astype(w_s.dtype)
    mixed = jnp.einsum("hij,bnjhd->bnihd", w_s * mask, vb) + b_s.T[None, None, :, :, None]
    return u * mixed.reshape(bsz, seq, SGU_WIDTH)


def pool_mixer(p, w_g, scale):
    bsz, seq, _ = p.shape
    pf = p.astype(jnp.float32)
    cs = jnp.concatenate([jnp.zeros((bsz, 1, POOL_WIDTH), jnp.float32), jnp.cumsum(pf, axis=1)], axis=1)
    t = jnp.arange(seq)
    outs = []
    for g, w in enumerate(POOL_WINDOWS):
        lo, hi = g * POOL_GROUP_DIM, (g + 1) * POOL_GROUP_DIM
        csg = cs[:, :, lo:hi]
        upper = csg[:, 1:]
        lower = jnp.concatenate([jnp.zeros((bsz, w - 1, POOL_GROUP_DIM), jnp.float32), csg[:, :seq + 1 - w]], axis=1)
        count = jnp.minimum(t + 1, w).astype(jnp.float32)[None, :, None]
        outs.append((upper - lower) / count - pf[:, :, lo:hi])
    pooled = jnp.stack(outs, axis=2).astype(p.dtype)
    mixed = jnp.einsum("bsgc,gcd->bsgd", pooled, w_g).reshape(bsz, seq, POOL_WIDTH)
    return mixed * scale


def mla_mixer(c_q, c_kv, k_rope, q_norm_g, w_uq, kv_norm_g, w_ukv, cos, sin):
    bsz, seq, _ = c_q.shape
    q = (rms_norm(c_q, q_norm_g) @ w_uq).reshape(bsz, seq, MLA_HEADS, QK_HEAD_DIM)
    q = jnp.concatenate([q[..., :QK_NOPE_DIM], apply_rope(q[..., QK_NOPE_DIM:], cos, sin)], axis=-1)
    kv = (rms_norm(c_kv, kv_norm_g) @ w_ukv).reshape(bsz, seq, MLA_HEADS, QK_NOPE_DIM + V_HEAD_DIM)
    k_nope, v = kv[..., :QK_NOPE_DIM], kv[..., QK_NOPE_DIM:]
    k_pe = apply_rope(k_rope[:, :, None, :], cos, sin)
    k = jnp.concatenate([k_nope, jnp.broadcast_to(k_pe, (bsz, seq, MLA_HEADS, QK_ROPE_DIM))], axis=-1)
    scale = QK_HEAD_DIM ** -0.5
    n_blk = seq // Q_BLOCK
    q_blocks = q.reshape(bsz, n_blk, Q_BLOCK, MLA_HEADS, QK_HEAD_DIM).transpose(1, 0, 2, 3, 4)
    key_chunk = jnp.arange(seq) // CHUNK

    def attend(args):
        q_blk, blk = args
        s = jnp.einsum("bqhd,bkhd->bhqk", q_blk, k).astype(jnp.float32) * scale
        q_chunk = (blk * Q_BLOCK + jnp.arange(Q_BLOCK)) // CHUNK
        allowed = key_chunk[None, :] <= q_chunk[:, None]
        s = jnp.where(allowed, s, NEG_INF)
        p = jax.nn.softmax(s, axis=-1).astype(v.dtype)
        return jnp.einsum("bhqk,bkhd->bqhd", p, v)

    o = lax.map(attend, (q_blocks, jnp.arange(n_blk)))
    return o.transpose(1, 0, 2, 3, 4).reshape(bsz, seq, MLA_WIDTH)


def hybrid_layer(x, pre_g, post_g, w_in, sgu_w, sgu_b, sgu_ln_g, sgu_ln_b, pool_w, pool_scale, q_norm_g, w_uq, kv_norm_g, w_ukv, w_out, cos, sin):
    h = rms_norm(x, pre_g)
    z = h @ w_in
    offs = [int(o) for o in np.cumsum(IN_SPLITS)[:-1]]
    sgu_u, sgu_v, sgu_gate, pool_in, pool_gate, c_q, c_kv, k_rope, mla_gate = jnp.split(z, offs, axis=-1)
    ya = sgu_mixer(sgu_u, sgu_v, sgu_w, sgu_b, sgu_ln_g, sgu_ln_b) * jax.nn.silu(sgu_gate)
    yb = pool_mixer(pool_in, pool_w, pool_scale) * jax.nn.silu(pool_gate)
    yc = mla_mixer(c_q, c_kv, k_rope, q_norm_g, w_uq, kv_norm_g, w_ukv, cos, sin) * jax.nn.silu(mla_gate)
    y = jnp.concatenate([ya, yb, yc], axis=-1) @ w_out
    return x + rms_norm(y, post_g)


def setup_inputs(seed: int = 0) -> dict:
    key = jax.random.key(seed)
    ks = jax.random.split(key, 16)

    def nrm(k, shape, s):
        return jax.random.normal(k, shape, jnp.float32) * s

    x = jax.random.normal(ks[0], (BATCH, SEQ, D_MODEL), jnp.float32)
    positions = jnp.broadcast_to(jnp.arange(SEQ, dtype=jnp.int32)[None, :], (BATCH, SEQ))
    pre_norm_g = 1.0 + nrm(ks[1], (DEPTH, D_MODEL), 0.02)
    post_norm_g = 1.0 + nrm(ks[2], (DEPTH, D_MODEL), 0.02)
    w_in = nrm(ks[3], (DEPTH, D_MODEL, D_IN), D_MODEL ** -0.5)
    sgu_w = nrm(ks[4], (DEPTH, SGU_HEADS, SGU_BLOCK, SGU_BLOCK), SGU_BLOCK ** -0.5)
    sgu_b = 1.0 + nrm(ks[5], (DEPTH, SGU_HEADS, SGU_BLOCK), 0.02)
    sgu_ln_g = 1.0 + nrm(ks[6], (DEPTH, SGU_WIDTH), 0.02)
    sgu_ln_b = nrm(ks[7], (DEPTH, SGU_WIDTH), 0.02)
    pool_w = nrm(ks[8], (DEPTH, POOL_GROUPS, POOL_GROUP_DIM, POOL_GROUP_DIM), POOL_GROUP_DIM ** -0.5)
    pool_scale = 1.0 + nrm(ks[9], (DEPTH, POOL_WIDTH), 0.02)
    q_norm_g = 1.0 + nrm(ks[10], (DEPTH, Q_LORA_RANK), 0.02)
    w_uq = nrm(ks[11], (DEPTH, Q_LORA_RANK, MLA_HEADS * QK_HEAD_DIM), Q_LORA_RANK ** -0.5)
    kv_norm_g = 1.0 + nrm(ks[12], (DEPTH, KV_LORA_RANK), 0.02)
    w_ukv = nrm(ks[13], (DEPTH, KV_LORA_RANK, MLA_HEADS * (QK_NOPE_DIM + V_HEAD_DIM)), KV_LORA_RANK ** -0.5)
    w_out = nrm(ks[14], (DEPTH, D_MIX, D_MODEL), D_MIX ** -0.5)
    return {"x": x, "positions": positions, "pre_norm_g": pre_norm_g, "post_norm_g": post_norm_g, "w_in": w_in, "sgu_w": sgu_w, "sgu_b": sgu_b, "sgu_ln_g": sgu_ln_g, "sgu_ln_b": sgu_ln_b, "pool_w": pool_w, "pool_scale": pool_scale, "q_norm_g": q_norm_g, "w_uq": w_uq, "kv_norm_g": kv_norm_g, "w_ukv": w_ukv, "w_out": w_out}


def reference(x, positions, pre_norm_g, post_norm_g, w_in, sgu_w, sgu_b, sgu_ln_g, sgu_ln_b, pool_w, pool_scale, q_norm_g, w_uq, kv_norm_g, w_ukv, w_out):
    cos, sin = rope_tables(positions)
    for l in range(DEPTH):
        x = hybrid_layer(x, pre_norm_g[l], post_norm_g[l], w_in[l], sgu_w[l], sgu_b[l], sgu_ln_g[l], sgu_ln_b[l], pool_w[l], pool_scale[l], q_norm_g[l], w_uq[l], kv_norm_g[l], w_ukv[l], w_out[l], cos, sin)
    return x
```

```python
import functools
import math

import jax
import jax.numpy as jnp
import numpy as np
from jax import lax
from jax.experimental import pallas as pl
from jax.experimental.pallas import tpu as pltpu

D_MODEL = 1024
SEQ = 16384
CHUNK = 64
EPS = 1e-6
NEG_INF = -1e30

SGU_WIDTH = 256
SGU_HEADS = 4
SGU_HEAD_DIM = SGU_WIDTH // SGU_HEADS
SGU_BLOCK = 128

POOL_WIDTH = 256
POOL_WINDOWS = (2, 4, 8, 16)
POOL_GROUP_DIM = POOL_WIDTH // len(POOL_WINDOWS)

MLA_WIDTH = 512
MLA_HEADS = 4
V_HEAD_DIM = MLA_WIDTH // MLA_HEADS
QK_NOPE_DIM = 128
QK_ROPE_DIM = 64
QK_HEAD_DIM = QK_NOPE_DIM + QK_ROPE_DIM
Q_LORA_RANK = 384
KV_LORA_RANK = 256
ROPE_BASE = 10000.0
ROPE_HALF = QK_ROPE_DIM // 2

LANES = 128
MXU_DIM = 256
VMEM_LIMIT_BYTES = 48 * 1024 * 1024

ROW_TILE = 512
ATT_TILE = 512
TABLE_TILE = 2048
QK_PAD_DIM = MXU_DIM
POOL_HALO = 32

OFF_U = 0
OFF_V = OFF_U + SGU_WIDTH
OFF_GA = OFF_V + SGU_WIDTH
OFF_PIN = OFF_GA + SGU_WIDTH
OFF_PG = OFF_PIN + POOL_WIDTH
OFF_CQ = OFF_PG + POOL_WIDTH
OFF_CKV = OFF_CQ + Q_LORA_RANK
OFF_MG = OFF_CKV + KV_LORA_RANK
OFF_KR = OFF_MG + MLA_WIDTH
D_IN_EXT = OFF_KR + 2 * QK_ROPE_DIM

Q_PRESCALE = (QK_HEAD_DIM ** -0.5) * math.log2(math.e)


def _silu(x):
    return x * (1.0 / (1.0 + jnp.exp(-x)))


def _rms(x, g):
    return x * lax.rsqrt(jnp.mean(x * x, axis=-1, keepdims=True) + EPS) * g


def _dot(a, b):
    return jnp.dot(a, b, preferred_element_type=jnp.float32)


def _dot_nt(a, b):
    return lax.dot_general(a, b, (((1,), (1,)), ((), ())), preferred_element_type=jnp.float32)


def _rope_tables_kernel(pos_row_ref, pos_col_ref, invf_col_ref, invf_row_ref,
                        cos_t_ref, sin_t_ref, cos_k_ref, sin_k_ref):
    ang_t = invf_col_ref[...] * pos_row_ref[...].astype(jnp.float32)
    cos_t_ref[...] = jnp.cos(ang_t)
    sin_t_ref[...] = jnp.sin(ang_t)
    ang = pos_col_ref[...].astype(jnp.float32) * invf_row_ref[...]
    lane = lax.broadcasted_iota(jnp.int32, ang.shape, 1)
    live = lane < QK_ROPE_DIM
    cos_k_ref[...] = jnp.where(live, jnp.cos(ang), 0.0)
    sin_k_ref[...] = jnp.where(live, jnp.sin(ang), 0.0)


def _rope_tables(positions):
    seq = positions.shape[-1]
    inv_freq = ROPE_BASE ** (-jnp.arange(0, QK_ROPE_DIM, 2, dtype=jnp.float32) / QK_ROPE_DIM)
    invf_col = inv_freq.reshape(ROPE_HALF, 1)
    invf_row = jnp.concatenate(
        [inv_freq, inv_freq, jnp.zeros((LANES - QK_ROPE_DIM,), jnp.float32)]).reshape(1, LANES)
    pos_row = positions.reshape(1, seq)
    pos_col = positions.reshape(seq, 1)
    n = seq // TABLE_TILE
    return pl.pallas_call(
        _rope_tables_kernel,
        grid=(n,),
        in_specs=[
            pl.BlockSpec((1, TABLE_TILE), lambda i: (0, i)),
            pl.BlockSpec((TABLE_TILE, 1), lambda i: (i, 0)),
            pl.BlockSpec((ROPE_HALF, 1), lambda i: (0, 0)),
            pl.BlockSpec((1, LANES), lambda i: (0, 0)),
        ],
        out_specs=[
            pl.BlockSpec((ROPE_HALF, TABLE_TILE), lambda i: (0, i)),
            pl.BlockSpec((ROPE_HALF, TABLE_TILE), lambda i: (0, i)),
            pl.BlockSpec((TABLE_TILE, LANES), lambda i: (i, 0)),
            pl.BlockSpec((TABLE_TILE, LANES), lambda i: (i, 0)),
        ],
        out_shape=[
            jax.ShapeDtypeStruct((ROPE_HALF, seq), jnp.float32),
            jax.ShapeDtypeStruct((ROPE_HALF, seq), jnp.float32),
            jax.ShapeDtypeStruct((seq, LANES), jnp.float32),
            jax.ShapeDtypeStruct((seq, LANES), jnp.float32),
        ],
        compiler_params=pltpu.CompilerParams(dimension_semantics=("arbitrary",)),
        name="rope_tables",
    )(pos_row, pos_col, invf_col, invf_row)


def _mixer_in_kernel(x_ref, pre_g_ref, w_in_ref, sgu_w_ref, sgu_bias_ref, ln_g_ref, ln_b_ref,
                     pool_w_ref, pool_scale_ref, qn_g_ref, w_uq_t_ref, kvn_g_ref, w_k_ref, w_v_t_ref,
                     cos_t_ref, sin_t_ref, cos_k_ref, sin_k_ref,
                     yab_ref, gate_ref, qt_ref, k_ref, vt_ref,
                     ext_ref, a2_ref, a4_ref, a8_ref):
    i = pl.program_id(0)
    tm = x_ref.shape[0]
    hb = _rms(x_ref[...], pre_g_ref[...]).astype(jnp.bfloat16)

    def proj(off, width):
        return _dot(hb, w_in_ref[:, off:off + width])

    v = proj(OFF_V, SGU_WIDTH)
    mu = jnp.mean(v, axis=-1, keepdims=True)
    vc = v - mu
    var = jnp.mean(vc * vc, axis=-1, keepdims=True)
    vn = vc * lax.rsqrt(var + EPS) * ln_g_ref[...] + ln_b_ref[...]
    w_rows = lax.broadcasted_iota(jnp.int32, (SGU_BLOCK, SGU_HEADS * SGU_BLOCK), 0)
    w_cols = lax.broadcasted_iota(jnp.int32, (SGU_BLOCK, SGU_HEADS * SGU_BLOCK), 1)
    w_keep = ((w_cols % SGU_BLOCK) // CHUNK) <= (w_rows // CHUNK)
    w_cat = jnp.where(w_keep, sgu_w_ref[...], 0.0).astype(jnp.bfloat16)
    head_of_col = lax.broadcasted_iota(jnp.int32, (SGU_BLOCK, SGU_WIDTH), 1) // SGU_HEAD_DIM
    mixed_blocks = []
    for r in range(tm // SGU_BLOCK):
        vb = vn[r * SGU_BLOCK:(r + 1) * SGU_BLOCK, :]
        v_stack = jnp.concatenate(
            [jnp.where(head_of_col == h, vb, 0.0) for h in range(SGU_HEADS)], axis=0)
        mixed_blocks.append(_dot(w_cat, v_stack.astype(jnp.bfloat16)) + sgu_bias_ref[...])
    mixed = jnp.concatenate(mixed_blocks, axis=0)
    ya = proj(OFF_U, SGU_WIDTH) * mixed * _silu(proj(OFF_GA, SGU_WIDTH))
    yab_ref[:, 0:SGU_WIDTH] = ya.astype(yab_ref.dtype)

    p = proj(OFF_PIN, POOL_WIDTH)

    @pl.when(i == 0)
    def _():
        ext_ref[0:POOL_HALO, :] = jnp.zeros((POOL_HALO, POOL_WIDTH), jnp.float32)

    ext_ref[POOL_HALO:POOL_HALO + tm, :] = p
    end = POOL_HALO + tm
    a2_ref[8:end, :] = ext_ref[8:end, :] + ext_ref[7:end - 1, :]
    a4_ref[16:end, :] = a2_ref[16:end, :] + a2_ref[14:end - 2, :]
    a8_ref[24:end, :] = a4_ref[24:end, :] + a4_ref[20:end - 4, :]
    a16 = a8_ref[POOL_HALO:end, :] + a8_ref[POOL_HALO - 8:end - 8, :]
    group = lax.broadcasted_iota(jnp.int32, (tm, POOL_WIDTH), 1) // POOL_GROUP_DIM
    sums = jnp.where(group == 0, a2_ref[POOL_HALO:end, :],
                     jnp.where(group == 1, a4_ref[POOL_HALO:end, :],
                               jnp.where(group == 2, a8_ref[POOL_HALO:end, :], a16)))
    window = jnp.where(group == 0, POOL_WINDOWS[0],
                       jnp.where(group == 1, POOL_WINDOWS[1],
                                 jnp.where(group == 2, POOL_WINDOWS[2], POOL_WINDOWS[3])))
    t_glob = i * tm + lax.broadcasted_iota(jnp.int32, (tm, POOL_WIDTH), 0)
    count = jnp.minimum(t_glob + 1, window).astype(jnp.float32)
    pooled = sums / count - p
    ext_ref[0:POOL_HALO, :] = ext_ref[tm:tm + POOL_HALO, :]
    yb = _dot(pooled.astype(jnp.bfloat16), pool_w_ref[...]) * pool_scale_ref[...] * _silu(proj(OFF_PG, POOL_WIDTH))
    yab_ref[:, SGU_WIDTH:SGU_WIDTH + POOL_WIDTH] = yb.astype(yab_ref.dtype)

    gate_ref[...] = _silu(proj(OFF_MG, MLA_WIDTH)).astype(gate_ref.dtype)

    cqn = _rms(proj(OFF_CQ, Q_LORA_RANK), qn_g_ref[...]).astype(jnp.bfloat16)
    q_t = _dot_nt(w_uq_t_ref[...], cqn) * Q_PRESCALE
    cos_t = cos_t_ref[...]
    sin_t = sin_t_ref[...]
    for h in range(MLA_HEADS):
        base = h * QK_HEAD_DIM
        x1 = q_t[base + QK_NOPE_DIM:base + QK_NOPE_DIM + ROPE_HALF, :]
        x2 = q_t[base + QK_NOPE_DIM + ROPE_HALF:base + QK_HEAD_DIM, :]
        qt_ref[h, 0:QK_NOPE_DIM, :] = q_t[base:base + QK_NOPE_DIM, :].astype(qt_ref.dtype)
        qt_ref[h, QK_NOPE_DIM:QK_NOPE_DIM + ROPE_HALF, :] = (x1 * cos_t - x2 * sin_t).astype(qt_ref.dtype)
        qt_ref[h, QK_NOPE_DIM + ROPE_HALF:QK_HEAD_DIM, :] = (x2 * cos_t + x1 * sin_t).astype(qt_ref.dtype)
        qt_ref[h, QK_HEAD_DIM:QK_PAD_DIM, :] = jnp.zeros((QK_PAD_DIM - QK_HEAD_DIM, tm), qt_ref.dtype)

    ckvn = _rms(proj(OFF_CKV, KV_LORA_RANK), kvn_g_ref[...]).astype(jnp.bfloat16)
    k_nope = _dot(ckvn, w_k_ref[...])
    v_t = _dot_nt(w_v_t_ref[...], ckvn)
    kr = proj(OFF_KR, 2 * QK_ROPE_DIM)
    kr_swapped = pltpu.roll(kr, QK_ROPE_DIM, axis=1)
    k_pe = (kr * cos_k_ref[...] + kr_swapped * sin_k_ref[...]).astype(k_ref.dtype)
    n_sub = tm // ATT_TILE
    for h in range(MLA_HEADS):
        for c in range(n_sub):
            rows = slice(c * ATT_TILE, (c + 1) * ATT_TILE)
            k_ref[h, c, :, 0:QK_NOPE_DIM] = k_nope[rows, h * QK_NOPE_DIM:(h + 1) * QK_NOPE_DIM].astype(k_ref.dtype)
            k_ref[h, c, :, QK_NOPE_DIM:QK_PAD_DIM] = k_pe[rows, :]
            vt_ref[h, c, :, :] = v_t[h * V_HEAD_DIM:(h + 1) * V_HEAD_DIM, rows].astype(vt_ref.dtype)


def _mixer_in(x, lw, tables):
    seq = x.shape[0]
    tm = ROW_TILE
    n = seq // tm
    n_sub = tm // ATT_TILE
    cos_t, sin_t, cos_k, sin_k = tables

    def const(shape):
        return pl.BlockSpec(shape, lambda i: (0,) * len(shape))

    return pl.pallas_call(
        _mixer_in_kernel,
        grid=(n,),
        in_specs=[
            pl.BlockSpec((tm, D_MODEL), lambda i: (i, 0)),
            const((1, D_MODEL)),
            const((D_MODEL, D_IN_EXT)),
            const((SGU_BLOCK, SGU_HEADS * SGU_BLOCK)),
            const((SGU_BLOCK, SGU_WIDTH)),
            const((1, SGU_WIDTH)),
            const((1, SGU_WIDTH)),
            const((POOL_WIDTH, POOL_WIDTH)),
            const((1, POOL_WIDTH)),
            const((1, Q_LORA_RANK)),
            const((MLA_HEADS * QK_HEAD_DIM, Q_LORA_RANK)),
            const((1, KV_LORA_RANK)),
            const((KV_LORA_RANK, MLA_HEADS * QK_NOPE_DIM)),
            const((MLA_HEADS * V_HEAD_DIM, KV_LORA_RANK)),
            pl.BlockSpec((ROPE_HALF, tm), lambda i: (0, i)),
            pl.BlockSpec((ROPE_HALF, tm), lambda i: (0, i)),
            pl.BlockSpec((tm, LANES), lambda i: (i, 0)),
            pl.BlockSpec((tm, LANES), lambda i: (i, 0)),
        ],
        out_specs=[
            pl.BlockSpec((tm, SGU_WIDTH + POOL_WIDTH), lambda i: (i, 0)),
            pl.BlockSpec((tm, MLA_WIDTH), lambda i: (i, 0)),
            pl.BlockSpec((MLA_HEADS, QK_PAD_DIM, tm), lambda i: (0, 0, i)),
            pl.BlockSpec((MLA_HEADS, n_sub, ATT_TILE, QK_PAD_DIM), lambda i: (0, i, 0, 0)),
            pl.BlockSpec((MLA_HEADS, n_sub, V_HEAD_DIM, ATT_TILE), lambda i: (0, i, 0, 0)),
        ],
        out_shape=[
            jax.ShapeDtypeStruct((seq, SGU_WIDTH + POOL_WIDTH), jnp.bfloat16),
            jax.ShapeDtypeStruct((seq, MLA_WIDTH), jnp.bfloat16),
            jax.ShapeDtypeStruct((MLA_HEADS, QK_PAD_DIM, seq), jnp.bfloat16),
            jax.ShapeDtypeStruct((MLA_HEADS, seq // ATT_TILE, ATT_TILE, QK_PAD_DIM), jnp.bfloat16),
            jax.ShapeDtypeStruct((MLA_HEADS, seq // ATT_TILE, V_HEAD_DIM, ATT_TILE), jnp.bfloat16),
        ],
        scratch_shapes=[
            pltpu.VMEM((POOL_HALO + tm, POOL_WIDTH), jnp.float32),
            pltpu.VMEM((POOL_HALO + tm, POOL_WIDTH), jnp.float32),
            pltpu.VMEM((POOL_HALO + tm, POOL_WIDTH), jnp.float32),
            pltpu.VMEM((POOL_HALO + tm, POOL_WIDTH), jnp.float32),
        ],
        compiler_params=pltpu.CompilerParams(
            dimension_semantics=("arbitrary",), vmem_limit_bytes=VMEM_LIMIT_BYTES),
        name="mixer_in",
    )(x, lw["pre_g"], lw["w_in"], lw["sgu_w"], lw["sgu_bias"], lw["ln_g"], lw["ln_b"],
      lw["pool_w"], lw["pool_scale"], lw["qn_g"], lw["w_uq_t"], lw["kvn_g"], lw["w_k"], lw["w_v_t"],
      cos_t, sin_t, cos_k, sin_k)


def _attention_kernel(qt_ref, k_ref, vt_ref, o_ref, m_ref, l_ref, acc_ref):
    i = pl.program_id(1)
    t = ATT_TILE
    qt = qt_ref[0]
    m_ref[...] = jnp.full(m_ref.shape, NEG_INF, jnp.float32)
    l_ref[...] = jnp.zeros(l_ref.shape, jnp.float32)
    acc_ref[...] = jnp.zeros(acc_ref.shape, jnp.float32)

    def update(j, s):
        m_old = m_ref[...]
        m_new = jnp.maximum(m_old, jnp.max(s, axis=0, keepdims=True))
        alpha = jnp.exp2(m_old - m_new)
        p = jnp.exp2(s - m_new)
        l_ref[...] = alpha * l_ref[...] + jnp.sum(p, axis=0, keepdims=True)
        acc_ref[...] = alpha * acc_ref[...] + _dot(vt_ref[0, j], p.astype(jnp.bfloat16))
        m_ref[...] = m_new

    def full_tile(j, carry):
        update(j, _dot(k_ref[0, j], qt))
        return carry

    lax.fori_loop(0, i, full_tile, 0)

    s = _dot(k_ref[0, i], qt)
    key_chunk = lax.broadcasted_iota(jnp.int32, (t, t), 0) // CHUNK
    qry_chunk = lax.broadcasted_iota(jnp.int32, (t, t), 1) // CHUNK
    update(i, jnp.where(key_chunk <= qry_chunk, s, NEG_INF))

    o_t = acc_ref[...] * (1.0 / l_ref[...])
    o_ref[...] = o_t.T.astype(o_ref.dtype)


def _attention(q_t, k, v_t):
    heads, n_kv, t, _ = k.shape
    seq = n_kv * t
    return pl.pallas_call(
        _attention_kernel,
        grid=(heads, seq // t),
        in_specs=[
            pl.BlockSpec((1, QK_PAD_DIM, t), lambda h, i: (h, 0, i)),
            pl.BlockSpec((1, n_kv, t, QK_PAD_DIM), lambda h, i: (h, 0, 0, 0)),
            pl.BlockSpec((1, n_kv, V_HEAD_DIM, t), lambda h, i: (h, 0, 0, 0)),
        ],
        out_specs=pl.BlockSpec((t, V_HEAD_DIM), lambda h, i: (i, h)),
        out_shape=jax.ShapeDtypeStruct((seq, heads * V_HEAD_DIM), jnp.bfloat16),
        scratch_shapes=[
            pltpu.VMEM((1, t), jnp.float32),
            pltpu.VMEM((1, t), jnp.float32),
            pltpu.VMEM((V_HEAD_DIM, t), jnp.float32),
        ],
        compiler_params=pltpu.CompilerParams(
            dimension_semantics=("arbitrary", "arbitrary"), vmem_limit_bytes=VMEM_LIMIT_BYTES),
        name="attention",
    )(q_t, k, v_t)


def _mixer_out_kernel(x_ref, yab_ref, o_ref, gate_ref, w_out_ref, post_g_ref, out_ref):
    yc = (o_ref[...].astype(jnp.float32) * gate_ref[...].astype(jnp.float32)).astype(jnp.bfloat16)
    half = SGU_WIDTH + POOL_WIDTH
    y = _dot(yab_ref[...], w_out_ref[0:half, :]) + _dot(yc, w_out_ref[half:, :])
    out_ref[...] = x_ref[...] + _rms(y, post_g_ref[...])


def _mixer_out(x, yab, o, gate, lw):
    seq = x.shape[0]
    tm = ROW_TILE
    row = lambda width: pl.BlockSpec((tm, width), lambda i: (i, 0))
    return pl.pallas_call(
        _mixer_out_kernel,
        grid=(seq // tm,),
        in_specs=[
            row(D_MODEL), row(SGU_WIDTH + POOL_WIDTH), row(MLA_WIDTH), row(MLA_WIDTH),
            pl.BlockSpec((D_MODEL, D_MODEL), lambda i: (0, 0)),
            pl.BlockSpec((1, D_MODEL), lambda i: (0, 0)),
        ],
        out_specs=row(D_MODEL),
        out_shape=jax.ShapeDtypeStruct((seq, D_MODEL), jnp.float32),
        compiler_params=pltpu.CompilerParams(
            dimension_semantics=("arbitrary",), vmem_limit_bytes=VMEM_LIMIT_BYTES),
        name="mixer_out",
    )(x, yab, o, gate, lw["w_out"], lw["post_g"])


def _prep_layer(l, pre_norm_g, post_norm_g, w_in, sgu_w, sgu_b, sgu_ln_g, sgu_ln_b, pool_w, pool_scale,
                q_norm_g, w_uq, kv_norm_g, w_ukv, w_out):
    bf = jnp.bfloat16
    splits = (SGU_WIDTH, SGU_WIDTH, SGU_WIDTH, POOL_WIDTH, POOL_WIDTH, Q_LORA_RANK, KV_LORA_RANK,
              QK_ROPE_DIM, MLA_WIDTH)
    offs = [int(o) for o in np.cumsum(splits)[:-1]]
    w_u, w_v, w_ga, w_pin, w_pg, w_cq, w_ckv, w_kr, w_mg = jnp.split(w_in[l], offs, axis=-1)
    w_kr_rot = jnp.concatenate([-w_kr[:, ROPE_HALF:], w_kr[:, :ROPE_HALF]], axis=-1)
    w_in_r = jnp.concatenate([w_u, w_v, w_ga, w_pin, w_pg, w_cq, w_ckv, w_mg, w_kr, w_kr_rot], axis=-1)
    w_ukv_r = w_ukv[l].reshape(KV_LORA_RANK, MLA_HEADS, QK_NOPE_DIM + V_HEAD_DIM)
    w_k = w_ukv_r[:, :, :QK_NOPE_DIM].reshape(KV_LORA_RANK, MLA_HEADS * QK_NOPE_DIM)
    w_v = w_ukv_r[:, :, QK_NOPE_DIM:].reshape(KV_LORA_RANK, MLA_HEADS * V_HEAD_DIM)
    pool_bd = jax.scipy.linalg.block_diag(*[pool_w[l, g] for g in range(len(POOL_WINDOWS))])
    return {
        "pre_g": pre_norm_g[l].reshape(1, D_MODEL),
        "post_g": post_norm_g[l].reshape(1, D_MODEL),
        "w_in": w_in_r.astype(bf),
        "sgu_w": sgu_w[l].transpose(1, 0, 2).reshape(SGU_BLOCK, SGU_HEADS * SGU_BLOCK),
        "sgu_bias": jnp.repeat(sgu_b[l].T, SGU_HEAD_DIM, axis=1),
        "ln_g": sgu_ln_g[l].reshape(1, SGU_WIDTH),
        "ln_b": sgu_ln_b[l].reshape(1, SGU_WIDTH),
        "pool_w": pool_bd.astype(bf),
        "pool_scale": pool_scale[l].reshape(1, POOL_WIDTH),
        "qn_g": q_norm_g[l].reshape(1, Q_LORA_RANK),
        "w_uq_t": w_uq[l].T.astype(bf),
        "kvn_g": kv_norm_g[l].reshape(1, KV_LORA_RANK),
        "w_k": w_k.astype(bf),
        "w_v_t": w_v.T.astype(bf),
        "w_out": w_out[l].astype(bf),
    }


def kernel(x, positions, pre_norm_g, post_norm_g, w_in, sgu_w, sgu_b, sgu_ln_g, sgu_ln_b, pool_w, pool_scale,
           q_norm_g, w_uq, kv_norm_g, w_ukv, w_out):
    bsz, seq, d_model = x.shape
    assert bsz == 1 and seq == SEQ and d_model == D_MODEL
    assert seq % ROW_TILE == 0 and ROW_TILE % ATT_TILE == 0 and seq % TABLE_TILE == 0
    tables = _rope_tables(positions)
    xs = x.reshape(seq, d_model)
    for l in range(pre_norm_g.shape[0]):
        lw = _prep_layer(l, pre_norm_g, post_norm_g, w_in, sgu_w, sgu_b, sgu_ln_g, sgu_ln_b, pool_w,
                         pool_scale, q_norm_g, w_uq, kv_norm_g, w_ukv, w_out)
        yab, gate, q_t, k, v_t = _mixer_in(xs, lw, tables)
        o = _attention(q_t, k, v_t)
        xs = _mixer_out(xs, yab, o, gate, lw)
    return xs.reshape(bsz, seq, d_model)
```

```python
import functools
import math

import jax
import jax.numpy as jnp
import numpy as np
from jax import lax
from jax.experimental import pallas as pl
from jax.experimental.pallas import tpu as pltpu

D_MODEL = 1024
SEQ = 16384
CHUNK = 64
EPS = 1e-6
NEG_INF = -1e30

SGU_WIDTH = 256
SGU_HEADS = 4
SGU_HEAD_DIM = SGU_WIDTH // SGU_HEADS
SGU_BLOCK = 128

POOL_WIDTH = 256
POOL_WINDOWS = (2, 4, 8, 16)
POOL_GROUP_DIM = POOL_WIDTH // len(POOL_WINDOWS)

MLA_WIDTH = 512
MLA_HEADS = 4
V_HEAD_DIM = MLA_WIDTH // MLA_HEADS
QK_NOPE_DIM = 128
QK_ROPE_DIM = 64
QK_HEAD_DIM = QK_NOPE_DIM + QK_ROPE_DIM
Q_LORA_RANK = 384
KV_LORA_RANK = 256
ROPE_BASE = 10000.0
ROPE_HALF = QK_ROPE_DIM // 2

LANES = 128
MXU_DIM = 256
VMEM_LIMIT_BYTES = 48 * 1024 * 1024

ROW_TILE = 512
ATT_TILE = 512
ATT_HEADS_PER_STEP = 2
TABLE_TILE = 2048
QK_PAD_DIM = MXU_DIM
POOL_HALO = 32

OFF_U = 0
OFF_V = OFF_U + SGU_WIDTH
OFF_GA = OFF_V + SGU_WIDTH
OFF_PIN = OFF_GA + SGU_WIDTH
OFF_PG = OFF_PIN + POOL_WIDTH
OFF_CQ = OFF_PG + POOL_WIDTH
OFF_CKV = OFF_CQ + Q_LORA_RANK
OFF_MG = OFF_CKV + KV_LORA_RANK
OFF_KR = OFF_MG + MLA_WIDTH
D_IN_EXT = OFF_KR + 2 * QK_ROPE_DIM

Q_PRESCALE = (QK_HEAD_DIM ** -0.5) * math.log2(math.e)


def _silu(x):
    return x * (1.0 / (1.0 + jnp.exp(-x)))


def _rms(x, g):
    return x * lax.rsqrt(jnp.mean(x * x, axis=-1, keepdims=True) + EPS) * g


def _dot(a, b):
    return jnp.dot(a, b, preferred_element_type=jnp.float32)


def _dot_nt(a, b):
    return lax.dot_general(a, b, (((1,), (1,)), ((), ())), preferred_element_type=jnp.float32)


def _rope_tables_kernel(pos_row_ref, pos_col_ref, invf_col_ref, invf_row_ref,
                        cos_t_ref, sin_t_ref, cos_k_ref, sin_k_ref):
    ang_t = invf_col_ref[...] * pos_row_ref[...].astype(jnp.float32)
    cos_t_ref[...] = jnp.cos(ang_t)
    sin_t_ref[...] = jnp.sin(ang_t)
    ang = pos_col_ref[...].astype(jnp.float32) * invf_row_ref[...]
    lane = lax.broadcasted_iota(jnp.int32, ang.shape, 1)
    live = lane < QK_ROPE_DIM
    cos_k_ref[...] = jnp.where(live, jnp.cos(ang), 0.0)
    sin_k_ref[...] = jnp.where(live, jnp.sin(ang), 0.0)


def _rope_tables(positions):
    seq = positions.shape[-1]
    inv_freq = ROPE_BASE ** (-jnp.arange(0, QK_ROPE_DIM, 2, dtype=jnp.float32) / QK_ROPE_DIM)
    invf_col = inv_freq.reshape(ROPE_HALF, 1)
    invf_row = jnp.concatenate(
        [inv_freq, inv_freq, jnp.zeros((LANES - QK_ROPE_DIM,), jnp.float32)]).reshape(1, LANES)
    pos_row = positions.reshape(1, seq)
    pos_col = positions.reshape(seq, 1)
    n = seq // TABLE_TILE
    return pl.pallas_call(
        _rope_tables_kernel,
        grid=(n,),
        in_specs=[
            pl.BlockSpec((1, TABLE_TILE), lambda i: (0, i)),
            pl.BlockSpec((TABLE_TILE, 1), lambda i: (i, 0)),
            pl.BlockSpec((ROPE_HALF, 1), lambda i: (0, 0)),
            pl.BlockSpec((1, LANES), lambda i: (0, 0)),
        ],
        out_specs=[
            pl.BlockSpec((ROPE_HALF, TABLE_TILE), lambda i: (0, i)),
            pl.BlockSpec((ROPE_HALF, TABLE_TILE), lambda i: (0, i)),
            pl.BlockSpec((TABLE_TILE, LANES), lambda i: (i, 0)),
            pl.BlockSpec((TABLE_TILE, LANES), lambda i: (i, 0)),
        ],
        out_shape=[
            jax.ShapeDtypeStruct((ROPE_HALF, seq), jnp.float32),
            jax.ShapeDtypeStruct((ROPE_HALF, seq), jnp.float32),
            jax.ShapeDtypeStruct((seq, LANES), jnp.float32),
            jax.ShapeDtypeStruct((seq, LANES), jnp.float32),
        ],
        compiler_params=pltpu.CompilerParams(dimension_semantics=("arbitrary",)),
        name="rope_tables",
    )(pos_row, pos_col, invf_col, invf_row)


def _mixer_in_kernel(x_ref, pre_g_ref, w_in_ref, sgu_w_ref, sgu_bias_ref, ln_g_ref, ln_b_ref,
                     pool_w_ref, pool_scale_ref, qn_g_ref, w_uq_t_ref, kvn_g_ref, w_k_ref, w_v_t_ref,
                     cos_t_ref, sin_t_ref, cos_k_ref, sin_k_ref,
                     yab_ref, gate_ref, qt_ref, k_ref, vt_ref,
                     ext_ref, a2_ref, a4_ref, a8_ref):
    i = pl.program_id(0)
    tm = x_ref.shape[0]
    hb = _rms(x_ref[...], pre_g_ref[...]).astype(jnp.bfloat16)

    def proj(off, width):
        return _dot(hb, w_in_ref[:, off:off + width])

    v = proj(OFF_V, SGU_WIDTH)
    mu = jnp.mean(v, axis=-1, keepdims=True)
    vc = v - mu
    var = jnp.mean(vc * vc, axis=-1, keepdims=True)
    vn = vc * lax.rsqrt(var + EPS) * ln_g_ref[...] + ln_b_ref[...]
    w_rows = lax.broadcasted_iota(jnp.int32, (SGU_BLOCK, SGU_HEADS * SGU_BLOCK), 0)
    w_cols = lax.broadcasted_iota(jnp.int32, (SGU_BLOCK, SGU_HEADS * SGU_BLOCK), 1)
    w_keep = ((w_cols % SGU_BLOCK) // CHUNK) <= (w_rows // CHUNK)
    w_cat = jnp.where(w_keep, sgu_w_ref[...], 0.0).astype(jnp.bfloat16)
    head_of_col = lax.broadcasted_iota(jnp.int32, (SGU_BLOCK, SGU_WIDTH), 1) // SGU_HEAD_DIM
    mixed_blocks = []
    for r in range(tm // SGU_BLOCK):
        vb = vn[r * SGU_BLOCK:(r + 1) * SGU_BLOCK, :]
        v_stack = jnp.concatenate(
            [jnp.where(head_of_col == h, vb, 0.0) for h in range(SGU_HEADS)], axis=0)
        mixed_blocks.append(_dot(w_cat, v_stack.astype(jnp.bfloat16)) + sgu_bias_ref[...])
    mixed = jnp.concatenate(mixed_blocks, axis=0)
    ya = proj(OFF_U, SGU_WIDTH) * mixed * _silu(proj(OFF_GA, SGU_WIDTH))
    yab_ref[:, 0:SGU_WIDTH] = ya.astype(yab_ref.dtype)

    p = proj(OFF_PIN, POOL_WIDTH)

    @pl.when(i == 0)
    def _():
        ext_ref[0:POOL_HALO, :] = jnp.zeros((POOL_HALO, POOL_WIDTH), jnp.float32)

    ext_ref[POOL_HALO:POOL_HALO + tm, :] = p
    end = POOL_HALO + tm
    a2_ref[8:end, :] = ext_ref[8:end, :] + ext_ref[7:end - 1, :]
    a4_ref[16:end, :] = a2_ref[16:end, :] + a2_ref[14:end - 2, :]
    a8_ref[24:end, :] = a4_ref[24:end, :] + a4_ref[20:end - 4, :]
    a16 = a8_ref[POOL_HALO:end, :] + a8_ref[POOL_HALO - 8:end - 8, :]
    group = lax.broadcasted_iota(jnp.int32, (tm, POOL_WIDTH), 1) // POOL_GROUP_DIM
    sums = jnp.where(group == 0, a2_ref[POOL_HALO:end, :],
                     jnp.where(group == 1, a4_ref[POOL_HALO:end, :],
                               jnp.where(group == 2, a8_ref[POOL_HALO:end, :], a16)))
    window = jnp.where(group == 0, POOL_WINDOWS[0],
                       jnp.where(group == 1, POOL_WINDOWS[1],
                                 jnp.where(group == 2, POOL_WINDOWS[2], POOL_WINDOWS[3])))
    t_glob = i * tm + lax.broadcasted_iota(jnp.int32, (tm, POOL_WIDTH), 0)
    count = jnp.minimum(t_glob + 1, window).astype(jnp.float32)
    pooled = sums / count - p
    ext_ref[0:POOL_HALO, :] = ext_ref[tm:tm + POOL_HALO, :]
    yb = _dot(pooled.astype(jnp.bfloat16), pool_w_ref[...]) * pool_scale_ref[...] * _silu(proj(OFF_PG, POOL_WIDTH))
    yab_ref[:, SGU_WIDTH:SGU_WIDTH + POOL_WIDTH] = yb.astype(yab_ref.dtype)

    gate_ref[...] = _silu(proj(OFF_MG, MLA_WIDTH)).astype(gate_ref.dtype)

    cqn = _rms(proj(OFF_CQ, Q_LORA_RANK), qn_g_ref[...]).astype(jnp.bfloat16)
    q_t = _dot_nt(w_uq_t_ref[...], cqn) * Q_PRESCALE
    cos_t = cos_t_ref[...]
    sin_t = sin_t_ref[...]
    for h in range(MLA_HEADS):
        base = h * QK_HEAD_DIM
        x1 = q_t[base + QK_NOPE_DIM:base + QK_NOPE_DIM + ROPE_HALF, :]
        x2 = q_t[base + QK_NOPE_DIM + ROPE_HALF:base + QK_HEAD_DIM, :]
        qt_ref[h, 0:QK_NOPE_DIM, :] = q_t[base:base + QK_NOPE_DIM, :].astype(qt_ref.dtype)
        qt_ref[h, QK_NOPE_DIM:QK_NOPE_DIM + ROPE_HALF, :] = (x1 * cos_t - x2 * sin_t).astype(qt_ref.dtype)
        qt_ref[h, QK_NOPE_DIM + ROPE_HALF:QK_HEAD_DIM, :] = (x2 * cos_t + x1 * sin_t).astype(qt_ref.dtype)
        qt_ref[h, QK_HEAD_DIM:QK_PAD_DIM, :] = jnp.zeros((QK_PAD_DIM - QK_HEAD_DIM, tm), qt_ref.dtype)

    ckvn = _rms(proj(OFF_CKV, KV_LORA_RANK), kvn_g_ref[...]).astype(jnp.bfloat16)
    k_nope = _dot(ckvn, w_k_ref[...])
    v_t = _dot_nt(w_v_t_ref[...], ckvn)
    kr = proj(OFF_KR, 2 * QK_ROPE_DIM)
    kr_swapped = pltpu.roll(kr, QK_ROPE_DIM, axis=1)
    k_pe = (kr * cos_k_ref[...] + kr_swapped * sin_k_ref[...]).astype(k_ref.dtype)
    n_sub = tm // ATT_TILE
    for h in range(MLA_HEADS):
        for c in range(n_sub):
            rows = slice(c * ATT_TILE, (c + 1) * ATT_TILE)
            k_ref[h, c, :, 0:QK_NOPE_DIM] = k_nope[rows, h * QK_NOPE_DIM:(h + 1) * QK_NOPE_DIM].astype(k_ref.dtype)
            k_ref[h, c, :, QK_NOPE_DIM:QK_PAD_DIM] = k_pe[rows, :]
            vt_ref[h, c, :, :] = v_t[h * V_HEAD_DIM:(h + 1) * V_HEAD_DIM, rows].astype(vt_ref.dtype)


def _mixer_in(x, lw, tables):
    seq = x.shape[0]
    tm = ROW_TILE
    n = seq // tm
    n_sub = tm // ATT_TILE
    cos_t, sin_t, cos_k, sin_k = tables

    def const(shape):
        return pl.BlockSpec(shape, lambda i: (0,) * len(shape))

    return pl.pallas_call(
        _mixer_in_kernel,
        grid=(n,),
        in_specs=[
            pl.BlockSpec((tm, D_MODEL), lambda i: (i, 0)),
            const((1, D_MODEL)),
            const((D_MODEL, D_IN_EXT)),
            const((SGU_BLOCK, SGU_HEADS * SGU_BLOCK)),
            const((SGU_BLOCK, SGU_WIDTH)),
            const((1, SGU_WIDTH)),
            const((1, SGU_WIDTH)),
            const((POOL_WIDTH, POOL_WIDTH)),
            const((1, POOL_WIDTH)),
            const((1, Q_LORA_RANK)),
            const((MLA_HEADS * QK_HEAD_DIM, Q_LORA_RANK)),
            const((1, KV_LORA_RANK)),
            const((KV_LORA_RANK, MLA_HEADS * QK_NOPE_DIM)),
            const((MLA_HEADS * V_HEAD_DIM, KV_LORA_RANK)),
            pl.BlockSpec((ROPE_HALF, tm), lambda i: (0, i)),
            pl.BlockSpec((ROPE_HALF, tm), lambda i: (0, i)),
            pl.BlockSpec((tm, LANES), lambda i: (i, 0)),
            pl.BlockSpec((tm, LANES), lambda i: (i, 0)),
        ],
        out_specs=[
            pl.BlockSpec((tm, SGU_WIDTH + POOL_WIDTH), lambda i: (i, 0)),
            pl.BlockSpec((tm, MLA_WIDTH), lambda i: (i, 0)),
            pl.BlockSpec((MLA_HEADS, QK_PAD_DIM, tm), lambda i: (0, 0, i)),
            pl.BlockSpec((MLA_HEADS, n_sub, ATT_TILE, QK_PAD_DIM), lambda i: (0, i, 0, 0)),
            pl.BlockSpec((MLA_HEADS, n_sub, V_HEAD_DIM, ATT_TILE), lambda i: (0, i, 0, 0)),
        ],
        out_shape=[
            jax.ShapeDtypeStruct((seq, SGU_WIDTH + POOL_WIDTH), jnp.bfloat16),
            jax.ShapeDtypeStruct((seq, MLA_WIDTH), jnp.bfloat16),
            jax.ShapeDtypeStruct((MLA_HEADS, QK_PAD_DIM, seq), jnp.bfloat16),
            jax.ShapeDtypeStruct((MLA_HEADS, seq // ATT_TILE, ATT_TILE, QK_PAD_DIM), jnp.bfloat16),
            jax.ShapeDtypeStruct((MLA_HEADS, seq // ATT_TILE, V_HEAD_DIM, ATT_TILE), jnp.bfloat16),
        ],
        scratch_shapes=[
            pltpu.VMEM((POOL_HALO + tm, POOL_WIDTH), jnp.float32),
            pltpu.VMEM((POOL_HALO + tm, POOL_WIDTH), jnp.float32),
            pltpu.VMEM((POOL_HALO + tm, POOL_WIDTH), jnp.float32),
            pltpu.VMEM((POOL_HALO + tm, POOL_WIDTH), jnp.float32),
        ],
        compiler_params=pltpu.CompilerParams(
            dimension_semantics=("arbitrary",), vmem_limit_bytes=VMEM_LIMIT_BYTES),
        name="mixer_in",
    )(x, lw["pre_g"], lw["w_in"], lw["sgu_w"], lw["sgu_bias"], lw["ln_g"], lw["ln_b"],
      lw["pool_w"], lw["pool_scale"], lw["qn_g"], lw["w_uq_t"], lw["kvn_g"], lw["w_k"], lw["w_v_t"],
      cos_t, sin_t, cos_k, sin_k)


def _attention_kernel(qt_ref, k_ref, vt_ref, o_ref, m_ref, l_ref, acc_ref,
                      s0_ref, s1_ref, p0_ref, p1_ref, a0_ref, a1_ref):
    i = pl.program_id(1)
    t = ATT_TILE
    n_heads = qt_ref.shape[0]
    s_refs = (s0_ref, s1_ref)
    p_refs = (p0_ref, p1_ref)
    a_refs = (a0_ref, a1_ref)
    m_ref[...] = jnp.full(m_ref.shape, NEG_INF, jnp.float32)
    l_ref[...] = jnp.zeros(l_ref.shape, jnp.float32)
    acc_ref[...] = jnp.zeros(acc_ref.shape, jnp.float32)

    def score(slot, j, masked=False):
        for h in range(n_heads):
            s = _dot(k_ref[h, j], qt_ref[h])
            if masked:
                key_chunk = lax.broadcasted_iota(jnp.int32, (t, t), 0) // CHUNK
                qry_chunk = lax.broadcasted_iota(jnp.int32, (t, t), 1) // CHUNK
                s = jnp.where(key_chunk <= qry_chunk, s, NEG_INF)
            s_refs[slot][h] = s

    def softmax(slot):
        for h in range(n_heads):
            m_old = m_ref[h]
            m_new = jnp.maximum(m_old, jnp.max(s_refs[slot][h], axis=0, keepdims=True))
            alpha = jnp.exp2(m_old - m_new)
            p = jnp.exp2(s_refs[slot][h] - m_new)
            l_ref[h] = alpha * l_ref[h] + jnp.sum(p, axis=0, keepdims=True)
            m_ref[h] = m_new
            a_refs[slot][h] = alpha
            p_refs[slot][h] = p.astype(jnp.bfloat16)

    def value(slot, j):
        for h in range(n_heads):
            acc_ref[h] = a_refs[slot][h] * acc_ref[h] + _dot(vt_ref[h, j], p_refs[slot][h])

    @pl.when(i == 0)
    def _():
        score(0, 0, masked=True)
        softmax(0)
        value(0, 0)

    @pl.when(i == 1)
    def _():
        score(0, 0)
        score(1, 1, masked=True)
        softmax(0)
        softmax(1)
        value(0, 0)
        value(1, 1)

    @pl.when(i >= 2)
    def _():
        score(0, 0)
        score(1, 1)
        softmax(0)
        n_pairs = (i - 2) // 2

        def pair(jj, carry):
            t0 = 2 + 2 * jj
            score(0, t0)
            softmax(1)
            value(0, t0 - 2)
            score(1, t0 + 1)
            softmax(0)
            value(1, t0 - 1)
            return carry

        lax.fori_loop(0, n_pairs, pair, 0)

        @pl.when(i % 2 == 0)
        def _():
            score(0, i, masked=True)
            softmax(1)
            value(0, i - 2)
            softmax(0)
            value(1, i - 1)
            value(0, i)

        @pl.when(i % 2 == 1)
        def _():
            score(0, i - 1)
            softmax(1)
            value(0, i - 3)
            score(1, i, masked=True)
            softmax(0)
            value(1, i - 2)
            softmax(1)
            value(0, i - 1)
            value(1, i)

    for h in range(n_heads):
        o_t = acc_ref[h] * (1.0 / l_ref[h])
        o_ref[:, h * V_HEAD_DIM:(h + 1) * V_HEAD_DIM] = o_t.T.astype(o_ref.dtype)


def _attention(q_t, k, v_t):
    heads, n_kv, t, _ = k.shape
    seq = n_kv * t
    hp = ATT_HEADS_PER_STEP
    resident = dict(pipeline_mode=pl.Buffered(1))
    return pl.pallas_call(
        _attention_kernel,
        grid=(heads // hp, seq // t),
        in_specs=[
            pl.BlockSpec((hp, QK_PAD_DIM, t), lambda g, i: (g, 0, i)),
            pl.BlockSpec((hp, n_kv, t, QK_PAD_DIM), lambda g, i: (g, 0, 0, 0), **resident),
            pl.BlockSpec((hp, n_kv, V_HEAD_DIM, t), lambda g, i: (g, 0, 0, 0), **resident),
        ],
        out_specs=pl.BlockSpec((t, hp * V_HEAD_DIM), lambda g, i: (i, g)),
        out_shape=jax.ShapeDtypeStruct((seq, heads * V_HEAD_DIM), jnp.bfloat16),
        scratch_shapes=[
            pltpu.VMEM((hp, 1, t), jnp.float32),
            pltpu.VMEM((hp, 1, t), jnp.float32),
            pltpu.VMEM((hp, V_HEAD_DIM, t), jnp.float32),
            pltpu.VMEM((hp, t, t), jnp.float32),
            pltpu.VMEM((hp, t, t), jnp.float32),
            pltpu.VMEM((hp, t, t), jnp.bfloat16),
            pltpu.VMEM((hp, t, t), jnp.bfloat16),
            pltpu.VMEM((hp, 1, t), jnp.float32),
            pltpu.VMEM((hp, 1, t), jnp.float32),
        ],
        compiler_params=pltpu.CompilerParams(
            dimension_semantics=("arbitrary", "arbitrary"), vmem_limit_bytes=VMEM_LIMIT_BYTES),
        name="attention",
    )(q_t, k, v_t)


def _mixer_out_kernel(x_ref, yab_ref, o_ref, gate_ref, w_out_ref, post_g_ref, out_ref):
    yc = (o_ref[...].astype(jnp.float32) * gate_ref[...].astype(jnp.float32)).astype(jnp.bfloat16)
    half = SGU_WIDTH + POOL_WIDTH
    y = _dot(yab_ref[...], w_out_ref[0:half, :]) + _dot(yc, w_out_ref[half:, :])
    out_ref[...] = x_ref[...] + _rms(y, post_g_ref[...])


def _mixer_out(x, yab, o, gate, lw):
    seq = x.shape[0]
    tm = ROW_TILE
    row = lambda width: pl.BlockSpec((tm, width), lambda i: (i, 0))
    return pl.pallas_call(
        _mixer_out_kernel,
        grid=(seq // tm,),
        in_specs=[
            row(D_MODEL), row(SGU_WIDTH + POOL_WIDTH), row(MLA_WIDTH), row(MLA_WIDTH),
            pl.BlockSpec((D_MODEL, D_MODEL), lambda i: (0, 0)),
            pl.BlockSpec((1, D_MODEL), lambda i: (0, 0)),
        ],
        out_specs=row(D_MODEL),
        out_shape=jax.ShapeDtypeStruct((seq, D_MODEL), jnp.float32),
        compiler_params=pltpu.CompilerParams(
            dimension_semantics=("arbitrary",), vmem_limit_bytes=VMEM_LIMIT_BYTES),
        name="mixer_out",
    )(x, yab, o, gate, lw["w_out"], lw["post_g"])


def _prep_layer(l, pre_norm_g, post_norm_g, w_in, sgu_w, sgu_b, sgu_ln_g, sgu_ln_b, pool_w, pool_scale,
                q_norm_g, w_uq, kv_norm_g, w_ukv, w_out):
    bf = jnp.bfloat16
    splits = (SGU_WIDTH, SGU_WIDTH, SGU_WIDTH, POOL_WIDTH, POOL_WIDTH, Q_LORA_RANK, KV_LORA_RANK,
              QK_ROPE_DIM, MLA_WIDTH)
    offs = [int(o) for o in np.cumsum(splits)[:-1]]
    w_u, w_v, w_ga, w_pin, w_pg, w_cq, w_ckv, w_kr, w_mg = jnp.split(w_in[l], offs, axis=-1)
    w_kr_rot = jnp.concatenate([-w_kr[:, ROPE_HALF:], w_kr[:, :ROPE_HALF]], axis=-1)
    w_in_r = jnp.concatenate([w_u, w_v, w_ga, w_pin, w_pg, w_cq, w_ckv, w_mg, w_kr, w_kr_rot], axis=-1)
    w_ukv_r = w_ukv[l].reshape(KV_LORA_RANK, MLA_HEADS, QK_NOPE_DIM + V_HEAD_DIM)
    w_k = w_ukv_r[:, :, :QK_NOPE_DIM].reshape(KV_LORA_RANK, MLA_HEADS * QK_NOPE_DIM)
    w_v = w_ukv_r[:, :, QK_NOPE_DIM:].reshape(KV_LORA_RANK, MLA_HEADS * V_HEAD_DIM)
    pool_bd = jax.scipy.linalg.block_diag(*[pool_w[l, g] for g in range(len(POOL_WINDOWS))])
    return {
        "pre_g": pre_norm_g[l].reshape(1, D_MODEL),
        "post_g": post_norm_g[l].reshape(1, D_MODEL),
        "w_in": w_in_r.astype(bf),
        "sgu_w": sgu_w[l].transpose(1, 0, 2).reshape(SGU_BLOCK, SGU_HEADS * SGU_BLOCK),
        "sgu_bias": jnp.repeat(sgu_b[l].T, SGU_HEAD_DIM, axis=1),
        "ln_g": sgu_ln_g[l].reshape(1, SGU_WIDTH),
        "ln_b": sgu_ln_b[l].reshape(1, SGU_WIDTH),
        "pool_w": pool_bd.astype(bf),
        "pool_scale": pool_scale[l].reshape(1, POOL_WIDTH),
        "qn_g": q_norm_g[l].reshape(1, Q_LORA_RANK),
        "w_uq_t": w_uq[l].T.astype(bf),
        "kvn_g": kv_norm_g[l].reshape(1, KV_LORA_RANK),
        "w_k": w_k.astype(bf),
        "w_v_t": w_v.T.astype(bf),
        "w_out": w_out[l].astype(bf),
    }


def kernel(x, positions, pre_norm_g, post_norm_g, w_in, sgu_w, sgu_b, sgu_ln_g, sgu_ln_b, pool_w, pool_scale,
           q_norm_g, w_uq, kv_norm_g, w_ukv, w_out):
    bsz, seq, d_model = x.shape
    assert bsz == 1 and seq == SEQ and d_model == D_MODEL
    assert seq % ROW_TILE == 0 and ROW_TILE % ATT_TILE == 0 and seq % TABLE_TILE == 0
    tables = _rope_tables(positions)
    xs = x.reshape(seq, d_model)
    for l in range(pre_norm_g.shape[0]):
        lw = _prep_layer(l, pre_norm_g, post_norm_g, w_in, sgu_w, sgu_b, sgu_ln_g, sgu_ln_b, pool_w,
                         pool_scale, q_norm_g, w_uq, kv_norm_g, w_ukv, w_out)
        yab, gate, q_t, k, v_t = _mixer_in(xs, lw, tables)
        o = _attention(q_t, k, v_t)
        xs = _mixer_out(xs, yab, o, gate, lw)
    return xs.reshape(bsz, seq, d_model)
```

```python
import functools
import math

import jax
import jax.numpy as jnp
import numpy as np
from jax import lax
from jax.experimental import pallas as pl
from jax.experimental.pallas import tpu as pltpu

D_MODEL = 1024
SEQ = 16384
CHUNK = 64
EPS = 1e-6
NEG_INF = -1e30

SGU_WIDTH = 256
SGU_HEADS = 4
SGU_HEAD_DIM = SGU_WIDTH // SGU_HEADS
SGU_BLOCK = 128

POOL_WIDTH = 256
POOL_WINDOWS = (2, 4, 8, 16)
POOL_GROUP_DIM = POOL_WIDTH // len(POOL_WINDOWS)

MLA_WIDTH = 512
MLA_HEADS = 4
V_HEAD_DIM = MLA_WIDTH // MLA_HEADS
QK_NOPE_DIM = 128
QK_ROPE_DIM = 64
QK_HEAD_DIM = QK_NOPE_DIM + QK_ROPE_DIM
Q_LORA_RANK = 384
KV_LORA_RANK = 256
ROPE_BASE = 10000.0
ROPE_HALF = QK_ROPE_DIM // 2

LANES = 128
SUBLANES = 8
MXU_DIM = 256
VMEM_LIMIT_BYTES = 48 * 1024 * 1024

ROW_TILE = 512
ATT_TILE = 512
ATT_HEADS_PER_STEP = 2
SOFTMAX_ROWS = 64
TABLE_TILE = 2048
QK_PAD_DIM = MXU_DIM
VT_ROWS = V_HEAD_DIM + 16
POOL_HALO = 32

OFF_U = 0
OFF_V = OFF_U + SGU_WIDTH
OFF_GA = OFF_V + SGU_WIDTH
OFF_PIN = OFF_GA + SGU_WIDTH
OFF_PG = OFF_PIN + POOL_WIDTH
OFF_CQ = OFF_PG + POOL_WIDTH
OFF_CKV = OFF_CQ + Q_LORA_RANK
OFF_MG = OFF_CKV + KV_LORA_RANK
OFF_KR = OFF_MG + MLA_WIDTH
D_IN_EXT = OFF_KR + 2 * QK_ROPE_DIM

Q_PRESCALE = (QK_HEAD_DIM ** -0.5) * math.log2(math.e)


def _silu(x):
    return x * (1.0 / (1.0 + jnp.exp(-x)))


def _rms(x, g):
    return x * lax.rsqrt(jnp.mean(x * x, axis=-1, keepdims=True) + EPS) * g


def _dot(a, b):
    return jnp.dot(a, b, preferred_element_type=jnp.float32)


def _dot_nt(a, b):
    return lax.dot_general(a, b, (((1,), (1,)), ((), ())), preferred_element_type=jnp.float32)


def _rope_tables_kernel(pos_row_ref, pos_col_ref, invf_col_ref, invf_row_ref,
                        cos_t_ref, sin_t_ref, cos_k_ref, sin_k_ref):
    ang_t = invf_col_ref[...] * pos_row_ref[...].astype(jnp.float32)
    cos_t_ref[...] = jnp.cos(ang_t)
    sin_t_ref[...] = jnp.sin(ang_t)
    ang = pos_col_ref[...].astype(jnp.float32) * invf_row_ref[...]
    lane = lax.broadcasted_iota(jnp.int32, ang.shape, 1)
    live = lane < QK_ROPE_DIM
    cos_k_ref[...] = jnp.where(live, jnp.cos(ang), 0.0)
    sin_k_ref[...] = jnp.where(live, jnp.sin(ang), 0.0)


def _rope_tables(positions):
    seq = positions.shape[-1]
    inv_freq = ROPE_BASE ** (-jnp.arange(0, QK_ROPE_DIM, 2, dtype=jnp.float32) / QK_ROPE_DIM)
    invf_col = inv_freq.reshape(ROPE_HALF, 1)
    invf_row = jnp.concatenate(
        [inv_freq, inv_freq, jnp.zeros((LANES - QK_ROPE_DIM,), jnp.float32)]).reshape(1, LANES)
    pos_row = positions.reshape(1, seq)
    pos_col = positions.reshape(seq, 1)
    n = seq // TABLE_TILE
    return pl.pallas_call(
        _rope_tables_kernel,
        grid=(n,),
        in_specs=[
            pl.BlockSpec((1, TABLE_TILE), lambda i: (0, i)),
            pl.BlockSpec((TABLE_TILE, 1), lambda i: (i, 0)),
            pl.BlockSpec((ROPE_HALF, 1), lambda i: (0, 0)),
            pl.BlockSpec((1, LANES), lambda i: (0, 0)),
        ],
        out_specs=[
            pl.BlockSpec((ROPE_HALF, TABLE_TILE), lambda i: (0, i)),
            pl.BlockSpec((ROPE_HALF, TABLE_TILE), lambda i: (0, i)),
            pl.BlockSpec((TABLE_TILE, LANES), lambda i: (i, 0)),
            pl.BlockSpec((TABLE_TILE, LANES), lambda i: (i, 0)),
        ],
        out_shape=[
            jax.ShapeDtypeStruct((ROPE_HALF, seq), jnp.float32),
            jax.ShapeDtypeStruct((ROPE_HALF, seq), jnp.float32),
            jax.ShapeDtypeStruct((seq, LANES), jnp.float32),
            jax.ShapeDtypeStruct((seq, LANES), jnp.float32),
        ],
        compiler_params=pltpu.CompilerParams(dimension_semantics=("arbitrary",)),
        name="rope_tables",
    )(pos_row, pos_col, invf_col, invf_row)


def _mixer_in_kernel(x_ref, pre_g_ref, w_in_ref, sgu_w_ref, sgu_bias_ref, ln_g_ref, ln_b_ref,
                     pool_w_ref, pool_scale_ref, qn_g_ref, w_uq_t_ref, kvn_g_ref, w_k_ref, w_v_t_ref,
                     cos_t_ref, sin_t_ref, cos_k_ref, sin_k_ref,
                     yab_ref, gate_ref, qt_ref, k_ref, vt_ref,
                     ext_ref, a2_ref, a4_ref, a8_ref):
    i = pl.program_id(0)
    tm = x_ref.shape[0]
    hb = _rms(x_ref[...], pre_g_ref[...]).astype(jnp.bfloat16)

    def proj(off, width):
        return _dot(hb, w_in_ref[:, off:off + width])

    v = proj(OFF_V, SGU_WIDTH)
    mu = jnp.mean(v, axis=-1, keepdims=True)
    vc = v - mu
    var = jnp.mean(vc * vc, axis=-1, keepdims=True)
    vn = vc * lax.rsqrt(var + EPS) * ln_g_ref[...] + ln_b_ref[...]
    w_rows = lax.broadcasted_iota(jnp.int32, (SGU_BLOCK, SGU_HEADS * SGU_BLOCK), 0)
    w_cols = lax.broadcasted_iota(jnp.int32, (SGU_BLOCK, SGU_HEADS * SGU_BLOCK), 1)
    w_keep = ((w_cols % SGU_BLOCK) // CHUNK) <= (w_rows // CHUNK)
    w_cat = jnp.where(w_keep, sgu_w_ref[...], 0.0).astype(jnp.bfloat16)
    head_of_col = lax.broadcasted_iota(jnp.int32, (SGU_BLOCK, SGU_WIDTH), 1) // SGU_HEAD_DIM
    mixed_blocks = []
    for r in range(tm // SGU_BLOCK):
        vb = vn[r * SGU_BLOCK:(r + 1) * SGU_BLOCK, :]
        v_stack = jnp.concatenate(
            [jnp.where(head_of_col == h, vb, 0.0) for h in range(SGU_HEADS)], axis=0)
        mixed_blocks.append(_dot(w_cat, v_stack.astype(jnp.bfloat16)) + sgu_bias_ref[...])
    mixed = jnp.concatenate(mixed_blocks, axis=0)
    ya = proj(OFF_U, SGU_WIDTH) * mixed * _silu(proj(OFF_GA, SGU_WIDTH))
    yab_ref[:, 0:SGU_WIDTH] = ya.astype(yab_ref.dtype)

    p = proj(OFF_PIN, POOL_WIDTH)

    @pl.when(i == 0)
    def _():
        ext_ref[0:POOL_HALO, :] = jnp.zeros((POOL_HALO, POOL_WIDTH), jnp.float32)

    ext_ref[POOL_HALO:POOL_HALO + tm, :] = p
    end = POOL_HALO + tm
    a2_ref[8:end, :] = ext_ref[8:end, :] + ext_ref[7:end - 1, :]
    a4_ref[16:end, :] = a2_ref[16:end, :] + a2_ref[14:end - 2, :]
    a8_ref[24:end, :] = a4_ref[24:end, :] + a4_ref[20:end - 4, :]
    a16 = a8_ref[POOL_HALO:end, :] + a8_ref[POOL_HALO - 8:end - 8, :]
    group = lax.broadcasted_iota(jnp.int32, (tm, POOL_WIDTH), 1) // POOL_GROUP_DIM
    sums = jnp.where(group == 0, a2_ref[POOL_HALO:end, :],
                     jnp.where(group == 1, a4_ref[POOL_HALO:end, :],
                               jnp.where(group == 2, a8_ref[POOL_HALO:end, :], a16)))
    window = jnp.where(group == 0, POOL_WINDOWS[0],
                       jnp.where(group == 1, POOL_WINDOWS[1],
                                 jnp.where(group == 2, POOL_WINDOWS[2], POOL_WINDOWS[3])))
    t_glob = i * tm + lax.broadcasted_iota(jnp.int32, (tm, POOL_WIDTH), 0)
    count = jnp.minimum(t_glob + 1, window).astype(jnp.float32)
    pooled = sums / count - p
    ext_ref[0:POOL_HALO, :] = ext_ref[tm:tm + POOL_HALO, :]
    yb = _dot(pooled.astype(jnp.bfloat16), pool_w_ref[...]) * pool_scale_ref[...] * _silu(proj(OFF_PG, POOL_WIDTH))
    yab_ref[:, SGU_WIDTH:SGU_WIDTH + POOL_WIDTH] = yb.astype(yab_ref.dtype)

    gate_ref[...] = _silu(proj(OFF_MG, MLA_WIDTH)).astype(gate_ref.dtype)

    cqn = _rms(proj(OFF_CQ, Q_LORA_RANK), qn_g_ref[...]).astype(jnp.bfloat16)
    q_t = _dot_nt(w_uq_t_ref[...], cqn) * Q_PRESCALE
    cos_t = cos_t_ref[...]
    sin_t = sin_t_ref[...]
    for h in range(MLA_HEADS):
        base = h * QK_HEAD_DIM
        x1 = q_t[base + QK_NOPE_DIM:base + QK_NOPE_DIM + ROPE_HALF, :]
        x2 = q_t[base + QK_NOPE_DIM + ROPE_HALF:base + QK_HEAD_DIM, :]
        qt_ref[h, 0:QK_NOPE_DIM, :] = q_t[base:base + QK_NOPE_DIM, :].astype(qt_ref.dtype)
        qt_ref[h, QK_NOPE_DIM:QK_NOPE_DIM + ROPE_HALF, :] = (x1 * cos_t - x2 * sin_t).astype(qt_ref.dtype)
        qt_ref[h, QK_NOPE_DIM + ROPE_HALF:QK_HEAD_DIM, :] = (x2 * cos_t + x1 * sin_t).astype(qt_ref.dtype)
        qt_ref[h, QK_HEAD_DIM:QK_PAD_DIM, :] = jnp.zeros((QK_PAD_DIM - QK_HEAD_DIM, tm), qt_ref.dtype)

    ckvn = _rms(proj(OFF_CKV, KV_LORA_RANK), kvn_g_ref[...]).astype(jnp.bfloat16)
    k_nope = _dot(ckvn, w_k_ref[...])
    v_t = _dot_nt(w_v_t_ref[...], ckvn)
    kr = proj(OFF_KR, 2 * QK_ROPE_DIM)
    kr_swapped = pltpu.roll(kr, QK_ROPE_DIM, axis=1)
    k_pe = (kr * cos_k_ref[...] + kr_swapped * sin_k_ref[...]).astype(k_ref.dtype)
    n_sub = tm // ATT_TILE
    for h in range(MLA_HEADS):
        for c in range(n_sub):
            rows = slice(c * ATT_TILE, (c + 1) * ATT_TILE)
            k_ref[h, c, :, 0:QK_NOPE_DIM] = k_nope[rows, h * QK_NOPE_DIM:(h + 1) * QK_NOPE_DIM].astype(k_ref.dtype)
            k_ref[h, c, :, QK_NOPE_DIM:QK_PAD_DIM] = k_pe[rows, :]
            vt_ref[h, c, 0:V_HEAD_DIM, :] = v_t[h * V_HEAD_DIM:(h + 1) * V_HEAD_DIM, rows].astype(vt_ref.dtype)
            ones_row = lax.broadcasted_iota(jnp.int32, (VT_ROWS - V_HEAD_DIM, ATT_TILE), 0) == 0
            vt_ref[h, c, V_HEAD_DIM:VT_ROWS, :] = jnp.where(ones_row, 1.0, 0.0).astype(vt_ref.dtype)


def _mixer_in(x, lw, tables):
    seq = x.shape[0]
    tm = ROW_TILE
    n = seq // tm
    n_sub = tm // ATT_TILE
    cos_t, sin_t, cos_k, sin_k = tables

    def const(shape):
        return pl.BlockSpec(shape, lambda i: (0,) * len(shape))

    return pl.pallas_call(
        _mixer_in_kernel,
        grid=(n,),
        in_specs=[
            pl.BlockSpec((tm, D_MODEL), lambda i: (i, 0)),
            const((1, D_MODEL)),
            const((D_MODEL, D_IN_EXT)),
            const((SGU_BLOCK, SGU_HEADS * SGU_BLOCK)),
            const((SGU_BLOCK, SGU_WIDTH)),
            const((1, SGU_WIDTH)),
            const((1, SGU_WIDTH)),
            const((POOL_WIDTH, POOL_WIDTH)),
            const((1, POOL_WIDTH)),
            const((1, Q_LORA_RANK)),
            const((MLA_HEADS * QK_HEAD_DIM, Q_LORA_RANK)),
            const((1, KV_LORA_RANK)),
            const((KV_LORA_RANK, MLA_HEADS * QK_NOPE_DIM)),
            const((MLA_HEADS * V_HEAD_DIM, KV_LORA_RANK)),
            pl.BlockSpec((ROPE_HALF, tm), lambda i: (0, i)),
            pl.BlockSpec((ROPE_HALF, tm), lambda i: (0, i)),
            pl.BlockSpec((tm, LANES), lambda i: (i, 0)),
            pl.BlockSpec((tm, LANES), lambda i: (i, 0)),
        ],
        out_specs=[
            pl.BlockSpec((tm, SGU_WIDTH + POOL_WIDTH), lambda i: (i, 0)),
            pl.BlockSpec((tm, MLA_WIDTH), lambda i: (i, 0)),
            pl.BlockSpec((MLA_HEADS, QK_PAD_DIM, tm), lambda i: (0, 0, i)),
            pl.BlockSpec((MLA_HEADS, n_sub, ATT_TILE, QK_PAD_DIM), lambda i: (0, i, 0, 0)),
            pl.BlockSpec((MLA_HEADS, n_sub, VT_ROWS, ATT_TILE), lambda i: (0, i, 0, 0)),
        ],
        out_shape=[
            jax.ShapeDtypeStruct((seq, SGU_WIDTH + POOL_WIDTH), jnp.bfloat16),
            jax.ShapeDtypeStruct((seq, MLA_WIDTH), jnp.bfloat16),
            jax.ShapeDtypeStruct((MLA_HEADS, QK_PAD_DIM, seq), jnp.bfloat16),
            jax.ShapeDtypeStruct((MLA_HEADS, seq // ATT_TILE, ATT_TILE, QK_PAD_DIM), jnp.bfloat16),
            jax.ShapeDtypeStruct((MLA_HEADS, seq // ATT_TILE, VT_ROWS, ATT_TILE), jnp.bfloat16),
        ],
        scratch_shapes=[
            pltpu.VMEM((POOL_HALO + tm, POOL_WIDTH), jnp.float32),
            pltpu.VMEM((POOL_HALO + tm, POOL_WIDTH), jnp.float32),
            pltpu.VMEM((POOL_HALO + tm, POOL_WIDTH), jnp.float32),
            pltpu.VMEM((POOL_HALO + tm, POOL_WIDTH), jnp.float32),
        ],
        compiler_params=pltpu.CompilerParams(
            dimension_semantics=("arbitrary",), vmem_limit_bytes=VMEM_LIMIT_BYTES),
        name="mixer_in",
    )(x, lw["pre_g"], lw["w_in"], lw["sgu_w"], lw["sgu_bias"], lw["ln_g"], lw["ln_b"],
      lw["pool_w"], lw["pool_scale"], lw["qn_g"], lw["w_uq_t"], lw["kvn_g"], lw["w_k"], lw["w_v_t"],
      cos_t, sin_t, cos_k, sin_k)


def _attention_kernel(qt_ref, k_ref, vt_ref, o_ref, m_ref, acc_ref,
                      s0_ref, s1_ref, p0_ref, p1_ref, a0_ref, a1_ref, x0_ref, x1_ref):
    i = pl.program_id(1)
    t = ATT_TILE
    n_heads = qt_ref.shape[0]
    s_refs = (s0_ref, s1_ref)
    p_refs = (p0_ref, p1_ref)
    a_refs = (a0_ref, a1_ref)
    x_refs = (x0_ref, x1_ref)
    m_ref[...] = jnp.full(m_ref.shape, NEG_INF, jnp.float32)
    acc_ref[...] = jnp.zeros(acc_ref.shape, jnp.float32)

    def score(slot, j, masked=False):
        for h in range(n_heads):
            s = _dot(k_ref[h, j], qt_ref[h])
            if masked:
                key_chunk = lax.broadcasted_iota(jnp.int32, (t, t), 0) // CHUNK
                qry_chunk = lax.broadcasted_iota(jnp.int32, (t, t), 1) // CHUNK
                s = jnp.where(key_chunk <= qry_chunk, s, NEG_INF)
            s_refs[slot][h] = s
            x_refs[slot][h] = jnp.max(s, axis=0, keepdims=True)

    def softmax(slot):
        rc = SOFTMAX_ROWS
        for h in range(n_heads):
            m_old = m_ref[h]
            m_new = jnp.maximum(m_old, x_refs[slot][h])
            m_ref[h] = m_new
            a_refs[slot][h] = jnp.exp2(m_old - m_new)
            for r in range(0, t, rc):
                d = s_refs[slot][h, r:r + rc, :] - m_new
                p_refs[slot][h, r:r + rc, :] = jnp.exp2(d.astype(jnp.bfloat16))

    def value(slot, j):
        for h in range(n_heads):
            acc_ref[h] = a_refs[slot][h] * acc_ref[h] + _dot(vt_ref[h, j], p_refs[slot][h])

    def steady(t0, count):
        for u in range(count):
            score(u % 2, t0 + u)
            softmax((u + 1) % 2)
            value(u % 2, t0 + u - 2)

    @pl.when(i == 0)
    def _():
        score(0, 0, masked=True)
        softmax(0)
        value(0, 0)

    @pl.when(i == 1)
    def _():
        score(0, 0)
        score(1, 1, masked=True)
        softmax(0)
        softmax(1)
        value(0, 0)
        value(1, 1)

    @pl.when(i >= 2)
    def _():
        score(0, 0)
        score(1, 1)
        softmax(0)
        def pair_block(jj, carry):
            steady(2 + 2 * jj, 2)
            return carry

        lax.fori_loop(0, (i - 2) // 2, pair_block, 0)

        @pl.when(i % 2 == 0)
        def _():
            score(0, i, masked=True)
            softmax(1)
            value(0, i - 2)
            softmax(0)
            value(1, i - 1)
            value(0, i)

        @pl.when(i % 2 == 1)
        def _():
            steady(i - 1, 1)
            score(1, i, masked=True)
            softmax(0)
            value(1, i - 2)
            softmax(1)
            value(0, i - 1)
            value(1, i)

    for h in range(n_heads):
        o_t = acc_ref[h, 0:V_HEAD_DIM, :] * (1.0 / acc_ref[h, V_HEAD_DIM:V_HEAD_DIM + 1, :])
        o_ref[:, h * V_HEAD_DIM:(h + 1) * V_HEAD_DIM] = o_t.T.astype(o_ref.dtype)


def _attention(q_t, k, v_t):
    heads, n_kv, t, _ = k.shape
    seq = n_kv * t
    hp = ATT_HEADS_PER_STEP
    resident = dict(pipeline_mode=pl.Buffered(1))
    return pl.pallas_call(
        _attention_kernel,
        grid=(heads // hp, seq // t),
        in_specs=[
            pl.BlockSpec((hp, QK_PAD_DIM, t), lambda g, i: (g, 0, i)),
            pl.BlockSpec((hp, n_kv, t, QK_PAD_DIM), lambda g, i: (g, 0, 0, 0), **resident),
            pl.BlockSpec((hp, n_kv, VT_ROWS, t), lambda g, i: (g, 0, 0, 0), **resident),
        ],
        out_specs=pl.BlockSpec((t, hp * V_HEAD_DIM), lambda g, i: (i, g)),
        out_shape=jax.ShapeDtypeStruct((seq, heads * V_HEAD_DIM), jnp.bfloat16),
        scratch_shapes=[
            pltpu.VMEM((hp, 1, t), jnp.float32),
            pltpu.VMEM((hp, VT_ROWS, t), jnp.float32),
            pltpu.VMEM((hp, t, t), jnp.float32),
            pltpu.VMEM((hp, t, t), jnp.float32),
            pltpu.VMEM((hp, t, t), jnp.bfloat16),
            pltpu.VMEM((hp, t, t), jnp.bfloat16),
            pltpu.VMEM((hp, 1, t), jnp.float32),
            pltpu.VMEM((hp, 1, t), jnp.float32),
            pltpu.VMEM((hp, 1, t), jnp.float32),
            pltpu.VMEM((hp, 1, t), jnp.float32),
        ],
        compiler_params=pltpu.CompilerParams(
            dimension_semantics=("arbitrary", "arbitrary"), vmem_limit_bytes=VMEM_LIMIT_BYTES),
        name="attention",
    )(q_t, k, v_t)


def _mixer_out_kernel(x_ref, yab_ref, o_ref, gate_ref, w_out_ref, post_g_ref, out_ref):
    yc = (o_ref[...].astype(jnp.float32) * gate_ref[...].astype(jnp.float32)).astype(jnp.bfloat16)
    half = SGU_WIDTH + POOL_WIDTH
    y = _dot(yab_ref[...], w_out_ref[0:half, :]) + _dot(yc, w_out_ref[half:, :])
    out_ref[...] = x_ref[...] + _rms(y, post_g_ref[...])


def _mixer_out(x, yab, o, gate, lw):
    seq = x.shape[0]
    tm = ROW_TILE
    row = lambda width: pl.BlockSpec((tm, width), lambda i: (i, 0))
    return pl.pallas_call(
        _mixer_out_kernel,
        grid=(seq // tm,),
        in_specs=[
            row(D_MODEL), row(SGU_WIDTH + POOL_WIDTH), row(MLA_WIDTH), row(MLA_WIDTH),
            pl.BlockSpec((D_MODEL, D_MODEL), lambda i: (0, 0)),
            pl.BlockSpec((1, D_MODEL), lambda i: (0, 0)),
        ],
        out_specs=row(D_MODEL),
        out_shape=jax.ShapeDtypeStruct((seq, D_MODEL), jnp.float32),
        compiler_params=pltpu.CompilerParams(
            dimension_semantics=("arbitrary",), vmem_limit_bytes=VMEM_LIMIT_BYTES),
        name="mixer_out",
    )(x, yab, o, gate, lw["w_out"], lw["post_g"])


def _prep_layer(l, pre_norm_g, post_norm_g, w_in, sgu_w, sgu_b, sgu_ln_g, sgu_ln_b, pool_w, pool_scale,
                q_norm_g, w_uq, kv_norm_g, w_ukv, w_out):
    bf = jnp.bfloat16
    splits = (SGU_WIDTH, SGU_WIDTH, SGU_WIDTH, POOL_WIDTH, POOL_WIDTH, Q_LORA_RANK, KV_LORA_RANK,
              QK_ROPE_DIM, MLA_WIDTH)
    offs = [int(o) for o in np.cumsum(splits)[:-1]]
    w_u, w_v, w_ga, w_pin, w_pg, w_cq, w_ckv, w_kr, w_mg = jnp.split(w_in[l], offs, axis=-1)
    w_kr_rot = jnp.concatenate([-w_kr[:, ROPE_HALF:], w_kr[:, :ROPE_HALF]], axis=-1)
    w_in_r = jnp.concatenate([w_u, w_v, w_ga, w_pin, w_pg, w_cq, w_ckv, w_mg, w_kr, w_kr_rot], axis=-1)
    w_ukv_r = w_ukv[l].reshape(KV_LORA_RANK, MLA_HEADS, QK_NOPE_DIM + V_HEAD_DIM)
    w_k = w_ukv_r[:, :, :QK_NOPE_DIM].reshape(KV_LORA_RANK, MLA_HEADS * QK_NOPE_DIM)
    w_v = w_ukv_r[:, :, QK_NOPE_DIM:].reshape(KV_LORA_RANK, MLA_HEADS * V_HEAD_DIM)
    pool_bd = jax.scipy.linalg.block_diag(*[pool_w[l, g] for g in range(len(POOL_WINDOWS))])
    return {
        "pre_g": pre_norm_g[l].reshape(1, D_MODEL),
        "post_g": post_norm_g[l].reshape(1, D_MODEL),
        "w_in": w_in_r.astype(bf),
        "sgu_w": sgu_w[l].transpose(1, 0, 2).reshape(SGU_BLOCK, SGU_HEADS * SGU_BLOCK),
        "sgu_bias": jnp.repeat(sgu_b[l].T, SGU_HEAD_DIM, axis=1),
        "ln_g": sgu_ln_g[l].reshape(1, SGU_WIDTH),
        "ln_b": sgu_ln_b[l].reshape(1, SGU_WIDTH),
        "pool_w": pool_bd.astype(bf),
        "pool_scale": pool_scale[l].reshape(1, POOL_WIDTH),
        "qn_g": q_norm_g[l].reshape(1, Q_LORA_RANK),
        "w_uq_t": w_uq[l].T.astype(bf),
        "kvn_g": kv_norm_g[l].reshape(1, KV_LORA_RANK),
        "w_k": w_k.astype(bf),
        "w_v_t": w_v.T.astype(bf),
        "w_out": w_out[l].astype(bf),
    }


def kernel(x, positions, pre_norm_g, post_norm_g, w_in, sgu_w, sgu_b, sgu_ln_g, sgu_ln_b, pool_w, pool_scale,
           q_norm_g, w_uq, kv_norm_g, w_ukv, w_out):
    bsz, seq, d_model = x.shape
    assert bsz == 1 and seq == SEQ and d_model == D_MODEL
    assert seq % ROW_TILE == 0 and ROW_TILE % ATT_TILE == 0 and seq % TABLE_TILE == 0
    tables = _rope_tables(positions)
    xs = x.reshape(seq, d_model)
    for l in range(pre_norm_g.shape[0]):
        lw = _prep_layer(l, pre_norm_g, post_norm_g, w_in, sgu_w, sgu_b, sgu_ln_g, sgu_ln_b, pool_w,
                         pool_scale, q_norm_g, w_uq, kv_norm_g, w_ukv, w_out)
        yab, gate, q_t, k, v_t = _mixer_in(xs, lw, tables)
        o = _attention(q_t, k, v_t)
        xs = _mixer_out(xs, yab, o, gate, lw)
    return xs.reshape(bsz, seq, d_model)
```

```python
import functools
import math

import jax
import jax.numpy as jnp
import numpy as np
from jax import lax
from jax.experimental import pallas as pl
from jax.experimental.pallas import tpu as pltpu

D_MODEL = 1024
SEQ = 16384
CHUNK = 64
EPS = 1e-6
NEG_INF = -1e30

SGU_WIDTH = 256
SGU_HEADS = 4
SGU_HEAD_DIM = SGU_WIDTH // SGU_HEADS
SGU_BLOCK = 128

POOL_WIDTH = 256
POOL_WINDOWS = (2, 4, 8, 16)
POOL_GROUP_DIM = POOL_WIDTH // len(POOL_WINDOWS)

MLA_WIDTH = 512
MLA_HEADS = 4
V_HEAD_DIM = MLA_WIDTH // MLA_HEADS
QK_NOPE_DIM = 128
QK_ROPE_DIM = 64
QK_HEAD_DIM = QK_NOPE_DIM + QK_ROPE_DIM
Q_LORA_RANK = 384
KV_LORA_RANK = 256
ROPE_BASE = 10000.0
ROPE_HALF = QK_ROPE_DIM // 2

LANES = 128
SUBLANES = 8
MXU_DIM = 256
VMEM_LIMIT_BYTES = 48 * 1024 * 1024

ROW_TILE = 512
ATT_TILE = 512
ATT_HEADS_PER_STEP = 2
SOFTMAX_ROWS = 64
STAGE_ORDER = ("score", "value", "softmax")
TABLE_TILE = 2048
QK_PAD_DIM = MXU_DIM
VT_ROWS = V_HEAD_DIM + 16
POOL_HALO = 32

OFF_U = 0
OFF_V = OFF_U + SGU_WIDTH
OFF_GA = OFF_V + SGU_WIDTH
OFF_PIN = OFF_GA + SGU_WIDTH
OFF_PG = OFF_PIN + POOL_WIDTH
OFF_CQ = OFF_PG + POOL_WIDTH
OFF_CKV = OFF_CQ + Q_LORA_RANK
OFF_MG = OFF_CKV + KV_LORA_RANK
OFF_KR = OFF_MG + MLA_WIDTH
D_IN_EXT = OFF_KR + 2 * QK_ROPE_DIM

Q_PRESCALE = (QK_HEAD_DIM ** -0.5) * math.log2(math.e)


def _silu(x):
    return x * (1.0 / (1.0 + jnp.exp(-x)))


def _rms(x, g):
    return x * lax.rsqrt(jnp.mean(x * x, axis=-1, keepdims=True) + EPS) * g


def _dot(a, b):
    return jnp.dot(a, b, preferred_element_type=jnp.float32)


def _dot_nt(a, b):
    return lax.dot_general(a, b, (((1,), (1,)), ((), ())), preferred_element_type=jnp.float32)


def _rope_tables_kernel(pos_row_ref, pos_col_ref, invf_col_ref, invf_row_ref,
                        cos_t_ref, sin_t_ref, cos_k_ref, sin_k_ref):
    ang_t = invf_col_ref[...] * pos_row_ref[...].astype(jnp.float32)
    cos_t_ref[...] = jnp.cos(ang_t)
    sin_t_ref[...] = jnp.sin(ang_t)
    ang = pos_col_ref[...].astype(jnp.float32) * invf_row_ref[...]
    lane = lax.broadcasted_iota(jnp.int32, ang.shape, 1)
    live = lane < QK_ROPE_DIM
    cos_k_ref[...] = jnp.where(live, jnp.cos(ang), 0.0)
    sin_k_ref[...] = jnp.where(live, jnp.sin(ang), 0.0)


def _rope_tables(positions):
    seq = positions.shape[-1]
    inv_freq = ROPE_BASE ** (-jnp.arange(0, QK_ROPE_DIM, 2, dtype=jnp.float32) / QK_ROPE_DIM)
    invf_col = inv_freq.reshape(ROPE_HALF, 1)
    invf_row = jnp.concatenate(
        [inv_freq, inv_freq, jnp.zeros((LANES - QK_ROPE_DIM,), jnp.float32)]).reshape(1, LANES)
    pos_row = positions.reshape(1, seq)
    pos_col = positions.reshape(seq, 1)
    n = seq // TABLE_TILE
    return pl.pallas_call(
        _rope_tables_kernel,
        grid=(n,),
        in_specs=[
            pl.BlockSpec((1, TABLE_TILE), lambda i: (0, i)),
            pl.BlockSpec((TABLE_TILE, 1), lambda i: (i, 0)),
            pl.BlockSpec((ROPE_HALF, 1), lambda i: (0, 0)),
            pl.BlockSpec((1, LANES), lambda i: (0, 0)),
        ],
        out_specs=[
            pl.BlockSpec((ROPE_HALF, TABLE_TILE), lambda i: (0, i)),
            pl.BlockSpec((ROPE_HALF, TABLE_TILE), lambda i: (0, i)),
            pl.BlockSpec((TABLE_TILE, LANES), lambda i: (i, 0)),
            pl.BlockSpec((TABLE_TILE, LANES), lambda i: (i, 0)),
        ],
        out_shape=[
            jax.ShapeDtypeStruct((ROPE_HALF, seq), jnp.float32),
            jax.ShapeDtypeStruct((ROPE_HALF, seq), jnp.float32),
            jax.ShapeDtypeStruct((seq, LANES), jnp.float32),
            jax.ShapeDtypeStruct((seq, LANES), jnp.float32),
        ],
        compiler_params=pltpu.CompilerParams(dimension_semantics=("arbitrary",)),
        name="rope_tables",
    )(pos_row, pos_col, invf_col, invf_row)


def _mixer_in_kernel(x_ref, pre_g_ref, w_in_ref, sgu_w_ref, sgu_bias_ref, ln_g_ref, ln_b_ref,
                     pool_w_ref, pool_scale_ref, qn_g_ref, w_uq_t_ref, kvn_g_ref, w_k_ref, w_v_t_ref,
                     cos_t_ref, sin_t_ref, cos_k_ref, sin_k_ref,
                     yab_ref, gate_ref, qt_ref, k_ref, vt_ref,
                     z_ref, ext_ref, a2_ref, a4_ref, a8_ref):
    i = pl.program_id(0)
    tm = x_ref.shape[0]
    hb = _rms(x_ref[...], pre_g_ref[...]).astype(jnp.bfloat16)
    z_ref[...] = _dot(hb, w_in_ref[...])

    def proj(off, width):
        return z_ref[:, off:off + width]

    v = proj(OFF_V, SGU_WIDTH)
    mu = jnp.mean(v, axis=-1, keepdims=True)
    vc = v - mu
    var = jnp.mean(vc * vc, axis=-1, keepdims=True)
    vn = vc * lax.rsqrt(var + EPS) * ln_g_ref[...] + ln_b_ref[...]
    w_rows = lax.broadcasted_iota(jnp.int32, (SGU_BLOCK, SGU_HEADS * SGU_BLOCK), 0)
    w_cols = lax.broadcasted_iota(jnp.int32, (SGU_BLOCK, SGU_HEADS * SGU_BLOCK), 1)
    w_keep = ((w_cols % SGU_BLOCK) // CHUNK) <= (w_rows // CHUNK)
    w_cat = jnp.where(w_keep, sgu_w_ref[...], 0.0).astype(jnp.bfloat16)
    head_of_col = lax.broadcasted_iota(jnp.int32, (SGU_BLOCK, SGU_WIDTH), 1) // SGU_HEAD_DIM
    n_blk = tm // SGU_BLOCK
    v_stacks = []
    for r in range(n_blk):
        vb = vn[r * SGU_BLOCK:(r + 1) * SGU_BLOCK, :]
        v_stacks.append(jnp.concatenate(
            [jnp.where(head_of_col == h, vb, 0.0) for h in range(SGU_HEADS)], axis=0).astype(jnp.bfloat16))
    mixed_wide = _dot(w_cat, jnp.concatenate(v_stacks, axis=1))
    mixed = jnp.concatenate(
        [mixed_wide[:, r * SGU_WIDTH:(r + 1) * SGU_WIDTH] + sgu_bias_ref[...] for r in range(n_blk)], axis=0)
    ya = proj(OFF_U, SGU_WIDTH) * mixed * _silu(proj(OFF_GA, SGU_WIDTH))
    yab_ref[:, 0:SGU_WIDTH] = ya.astype(yab_ref.dtype)

    p = proj(OFF_PIN, POOL_WIDTH)

    @pl.when(i == 0)
    def _():
        ext_ref[0:POOL_HALO, :] = jnp.zeros((POOL_HALO, POOL_WIDTH), jnp.float32)

    ext_ref[POOL_HALO:POOL_HALO + tm, :] = p
    end = POOL_HALO + tm
    a2_ref[8:end, :] = ext_ref[8:end, :] + ext_ref[7:end - 1, :]
    a4_ref[16:end, :] = a2_ref[16:end, :] + a2_ref[14:end - 2, :]
    a8_ref[24:end, :] = a4_ref[24:end, :] + a4_ref[20:end - 4, :]
    a16 = a8_ref[POOL_HALO:end, :] + a8_ref[POOL_HALO - 8:end - 8, :]
    group = lax.broadcasted_iota(jnp.int32, (tm, POOL_WIDTH), 1) // POOL_GROUP_DIM
    sums = jnp.where(group == 0, a2_ref[POOL_HALO:end, :],
                     jnp.where(group == 1, a4_ref[POOL_HALO:end, :],
                               jnp.where(group == 2, a8_ref[POOL_HALO:end, :], a16)))
    window = jnp.where(group == 0, POOL_WINDOWS[0],
                       jnp.where(group == 1, POOL_WINDOWS[1],
                                 jnp.where(group == 2, POOL_WINDOWS[2], POOL_WINDOWS[3])))
    t_glob = i * tm + lax.broadcasted_iota(jnp.int32, (tm, POOL_WIDTH), 0)
    count = jnp.minimum(t_glob + 1, window).astype(jnp.float32)
    pooled = sums / count - p
    ext_ref[0:POOL_HALO, :] = ext_ref[tm:tm + POOL_HALO, :]
    yb = _dot(pooled.astype(jnp.bfloat16), pool_w_ref[...]) * pool_scale_ref[...] * _silu(proj(OFF_PG, POOL_WIDTH))
    yab_ref[:, SGU_WIDTH:SGU_WIDTH + POOL_WIDTH] = yb.astype(yab_ref.dtype)

    gate_ref[...] = _silu(proj(OFF_MG, MLA_WIDTH)).astype(gate_ref.dtype)

    cqn = _rms(proj(OFF_CQ, Q_LORA_RANK), qn_g_ref[...]).astype(jnp.bfloat16)
    q_t = _dot_nt(w_uq_t_ref[...], cqn) * Q_PRESCALE
    cos_t = cos_t_ref[...]
    sin_t = sin_t_ref[...]
    for h in range(MLA_HEADS):
        base = h * QK_HEAD_DIM
        x1 = q_t[base + QK_NOPE_DIM:base + QK_NOPE_DIM + ROPE_HALF, :]
        x2 = q_t[base + QK_NOPE_DIM + ROPE_HALF:base + QK_HEAD_DIM, :]
        qt_ref[h, 0:QK_NOPE_DIM, :] = q_t[base:base + QK_NOPE_DIM, :].astype(qt_ref.dtype)
        qt_ref[h, QK_NOPE_DIM:QK_NOPE_DIM + ROPE_HALF, :] = (x1 * cos_t - x2 * sin_t).astype(qt_ref.dtype)
        qt_ref[h, QK_NOPE_DIM + ROPE_HALF:QK_HEAD_DIM, :] = (x2 * cos_t + x1 * sin_t).astype(qt_ref.dtype)
        qt_ref[h, QK_HEAD_DIM:QK_PAD_DIM, :] = jnp.zeros((QK_PAD_DIM - QK_HEAD_DIM, tm), qt_ref.dtype)

    ckvn = _rms(proj(OFF_CKV, KV_LORA_RANK), kvn_g_ref[...]).astype(jnp.bfloat16)
    k_nope = _dot(ckvn, w_k_ref[...])
    v_t = _dot_nt(w_v_t_ref[...], ckvn)
    kr = proj(OFF_KR, 2 * QK_ROPE_DIM)
    kr_swapped = pltpu.roll(kr, QK_ROPE_DIM, axis=1)
    k_pe = (kr * cos_k_ref[...] + kr_swapped * sin_k_ref[...]).astype(k_ref.dtype)
    n_sub = tm // ATT_TILE
    for h in range(MLA_HEADS):
        for c in range(n_sub):
            rows = slice(c * ATT_TILE, (c + 1) * ATT_TILE)
            k_ref[h, c, :, 0:QK_NOPE_DIM] = k_nope[rows, h * QK_NOPE_DIM:(h + 1) * QK_NOPE_DIM].astype(k_ref.dtype)
            k_ref[h, c, :, QK_NOPE_DIM:QK_PAD_DIM] = k_pe[rows, :]
            vt_ref[h, c, 0:V_HEAD_DIM, :] = v_t[h * V_HEAD_DIM:(h + 1) * V_HEAD_DIM, rows].astype(vt_ref.dtype)
            ones_row = lax.broadcasted_iota(jnp.int32, (VT_ROWS - V_HEAD_DIM, ATT_TILE), 0) == 0
            vt_ref[h, c, V_HEAD_DIM:VT_ROWS, :] = jnp.where(ones_row, 1.0, 0.0).astype(vt_ref.dtype)


def _mixer_in(x, lw, tables):
    seq = x.shape[0]
    tm = ROW_TILE
    n = seq // tm
    n_sub = tm // ATT_TILE
    cos_t, sin_t, cos_k, sin_k = tables

    def const(shape):
        return pl.BlockSpec(shape, lambda i: (0,) * len(shape))

    return pl.pallas_call(
        _mixer_in_kernel,
        grid=(n,),
        in_specs=[
            pl.BlockSpec((tm, D_MODEL), lambda i: (i, 0)),
            const((1, D_MODEL)),
            const((D_MODEL, D_IN_EXT)),
            const((SGU_BLOCK, SGU_HEADS * SGU_BLOCK)),
            const((SGU_BLOCK, SGU_WIDTH)),
            const((1, SGU_WIDTH)),
            const((1, SGU_WIDTH)),
            const((POOL_WIDTH, POOL_WIDTH)),
            const((1, POOL_WIDTH)),
            const((1, Q_LORA_RANK)),
            const((MLA_HEADS * QK_HEAD_DIM, Q_LORA_RANK)),
            const((1, KV_LORA_RANK)),
            const((KV_LORA_RANK, MLA_HEADS * QK_NOPE_DIM)),
            const((MLA_HEADS * V_HEAD_DIM, KV_LORA_RANK)),
            pl.BlockSpec((ROPE_HALF, tm), lambda i: (0, i)),
            pl.BlockSpec((ROPE_HALF, tm), lambda i: (0, i)),
            pl.BlockSpec((tm, LANES), lambda i: (i, 0)),
            pl.BlockSpec((tm, LANES), lambda i: (i, 0)),
        ],
        out_specs=[
            pl.BlockSpec((tm, SGU_WIDTH + POOL_WIDTH), lambda i: (i, 0)),
            pl.BlockSpec((tm, MLA_WIDTH), lambda i: (i, 0)),
            pl.BlockSpec((MLA_HEADS, QK_PAD_DIM, tm), lambda i: (0, 0, i)),
            pl.BlockSpec((MLA_HEADS, n_sub, ATT_TILE, QK_PAD_DIM), lambda i: (0, i, 0, 0)),
            pl.BlockSpec((MLA_HEADS, n_sub, VT_ROWS, ATT_TILE), lambda i: (0, i, 0, 0)),
        ],
        out_shape=[
            jax.ShapeDtypeStruct((seq, SGU_WIDTH + POOL_WIDTH), jnp.bfloat16),
            jax.ShapeDtypeStruct((seq, MLA_WIDTH), jnp.bfloat16),
            jax.ShapeDtypeStruct((MLA_HEADS, QK_PAD_DIM, seq), jnp.bfloat16),
            jax.ShapeDtypeStruct((MLA_HEADS, seq // ATT_TILE, ATT_TILE, QK_PAD_DIM), jnp.bfloat16),
            jax.ShapeDtypeStruct((MLA_HEADS, seq // ATT_TILE, VT_ROWS, ATT_TILE), jnp.bfloat16),
        ],
        scratch_shapes=[
            pltpu.VMEM((tm, D_IN_EXT), jnp.float32),
            pltpu.VMEM((POOL_HALO + tm, POOL_WIDTH), jnp.float32),
            pltpu.VMEM((POOL_HALO + tm, POOL_WIDTH), jnp.float32),
            pltpu.VMEM((POOL_HALO + tm, POOL_WIDTH), jnp.float32),
            pltpu.VMEM((POOL_HALO + tm, POOL_WIDTH), jnp.float32),
        ],
        compiler_params=pltpu.CompilerParams(
            dimension_semantics=("arbitrary",), vmem_limit_bytes=VMEM_LIMIT_BYTES),
        name="mixer_in",
    )(x, lw["pre_g"], lw["w_in"], lw["sgu_w"], lw["sgu_bias"], lw["ln_g"], lw["ln_b"],
      lw["pool_w"], lw["pool_scale"], lw["qn_g"], lw["w_uq_t"], lw["kvn_g"], lw["w_k"], lw["w_v_t"],
      cos_t, sin_t, cos_k, sin_k)


def _attention_kernel(qt_ref, k_ref, vt_ref, o_ref, m_ref, acc_ref,
                      s0_ref, s1_ref, p0_ref, p1_ref, a0_ref, a1_ref, x0_ref, x1_ref):
    i = pl.program_id(1)
    t = ATT_TILE
    n_heads = qt_ref.shape[0]
    s_refs = (s0_ref, s1_ref)
    p_refs = (p0_ref, p1_ref)
    a_refs = (a0_ref, a1_ref)
    x_refs = (x0_ref, x1_ref)
    m_ref[...] = jnp.full(m_ref.shape, NEG_INF, jnp.float32)
    acc_ref[...] = jnp.zeros(acc_ref.shape, jnp.float32)

    def score(slot, j, masked=False):
        for h in range(n_heads):
            s = jnp.dot(k_ref[h, j], qt_ref[h], preferred_element_type=jnp.float32)
            if masked:
                key_chunk = lax.broadcasted_iota(jnp.int32, (t, t), 0) // CHUNK
                qry_chunk = lax.broadcasted_iota(jnp.int32, (t, t), 1) // CHUNK
                s = jnp.where(key_chunk <= qry_chunk, s, NEG_INF)
            s_refs[slot][h] = s
            x_refs[slot][h] = jnp.max(s, axis=0, keepdims=True)

    def softmax(slot):
        rc = SOFTMAX_ROWS
        for h in range(n_heads):
            m_old = m_ref[h]
            m_new = jnp.maximum(m_old, x_refs[slot][h])
            m_ref[h] = m_new
            a_refs[slot][h] = jnp.exp2(m_old - m_new)
            for r in range(0, t, rc):
                d = s_refs[slot][h, r:r + rc, :] - m_new
                p_refs[slot][h, r:r + rc, :] = jnp.exp2(d.astype(jnp.bfloat16))

    def value(slot, j):
        for h in range(n_heads):
            pv = jnp.dot(vt_ref[h, j], p_refs[slot][h], preferred_element_type=jnp.float32)
            acc_ref[h] = a_refs[slot][h] * acc_ref[h] + pv

    def steady(t0, count):
        for u in range(count):
            for stage in STAGE_ORDER:
                if stage == "score":
                    score(u % 2, t0 + u)
                elif stage == "softmax":
                    softmax((u + 1) % 2)
                else:
                    value(u % 2, t0 + u - 2)

    @pl.when(i == 0)
    def _():
        score(0, 0, masked=True)
        softmax(0)
        value(0, 0)

    @pl.when(i == 1)
    def _():
        score(0, 0)
        score(1, 1, masked=True)
        softmax(0)
        softmax(1)
        value(0, 0)
        value(1, 1)

    @pl.when(i >= 2)
    def _():
        score(0, 0)
        score(1, 1)
        softmax(0)
        def pair_block(jj, carry):
            steady(2 + 2 * jj, 2)
            return carry

        lax.fori_loop(0, (i - 2) // 2, pair_block, 0)

        @pl.when(i % 2 == 0)
        def _():
            score(0, i, masked=True)
            softmax(1)
            value(0, i - 2)
            softmax(0)
            value(1, i - 1)
            value(0, i)

        @pl.when(i % 2 == 1)
        def _():
            steady(i - 1, 1)
            score(1, i, masked=True)
            softmax(0)
            value(1, i - 2)
            softmax(1)
            value(0, i - 1)
            value(1, i)

    for h in range(n_heads):
        o_t = acc_ref[h, 0:V_HEAD_DIM, :] * (1.0 / acc_ref[h, V_HEAD_DIM:V_HEAD_DIM + 1, :])
        o_ref[:, h * V_HEAD_DIM:(h + 1) * V_HEAD_DIM] = o_t.T.astype(o_ref.dtype)


def _attention(q_t, k, v_t):
    heads, n_kv, t, _ = k.shape
    seq = n_kv * t
    hp = ATT_HEADS_PER_STEP
    resident = dict(pipeline_mode=pl.Buffered(1))
    return pl.pallas_call(
        _attention_kernel,
        grid=(heads // hp, seq // t),
        in_specs=[
            pl.BlockSpec((hp, QK_PAD_DIM, t), lambda g, i: (g, 0, i)),
            pl.BlockSpec((hp, n_kv, t, QK_PAD_DIM), lambda g, i: (g, 0, 0, 0), **resident),
            pl.BlockSpec((hp, n_kv, VT_ROWS, t), lambda g, i: (g, 0, 0, 0), **resident),
        ],
        out_specs=pl.BlockSpec((t, hp * V_HEAD_DIM), lambda g, i: (i, g)),
        out_shape=jax.ShapeDtypeStruct((seq, heads * V_HEAD_DIM), jnp.bfloat16),
        scratch_shapes=[
            pltpu.VMEM((hp, 1, t), jnp.float32),
            pltpu.VMEM((hp, VT_ROWS, t), jnp.float32),
            pltpu.VMEM((hp, t, t), jnp.float32),
            pltpu.VMEM((hp, t, t), jnp.float32),
            pltpu.VMEM((hp, t, t), jnp.bfloat16),
            pltpu.VMEM((hp, t, t), jnp.bfloat16),
            pltpu.VMEM((hp, 1, t), jnp.float32),
            pltpu.VMEM((hp, 1, t), jnp.float32),
            pltpu.VMEM((hp, 1, t), jnp.float32),
            pltpu.VMEM((hp, 1, t), jnp.float32),
        ],
        compiler_params=pltpu.CompilerParams(
            dimension_semantics=("arbitrary", "arbitrary"), vmem_limit_bytes=VMEM_LIMIT_BYTES),
        name="attention",
    )(q_t, k, v_t)


def _mixer_out_kernel(x_ref, yab_ref, o_ref, gate_ref, w_out_ref, post_g_ref, out_ref):
    yc = (o_ref[...].astype(jnp.float32) * gate_ref[...].astype(jnp.float32)).astype(jnp.bfloat16)
    half = SGU_WIDTH + POOL_WIDTH
    y = _dot(yab_ref[...], w_out_ref[0:half, :]) + _dot(yc, w_out_ref[half:, :])
    out_ref[...] = x_ref[...] + _rms(y, post_g_ref[...])


def _mixer_out(x, yab, o, gate, lw):
    seq = x.shape[0]
    tm = ROW_TILE
    row = lambda width: pl.BlockSpec((tm, width), lambda i: (i, 0))
    return pl.pallas_call(
        _mixer_out_kernel,
        grid=(seq // tm,),
        in_specs=[
            row(D_MODEL), row(SGU_WIDTH + POOL_WIDTH), row(MLA_WIDTH), row(MLA_WIDTH),
            pl.BlockSpec((D_MODEL, D_MODEL), lambda i: (0, 0)),
            pl.BlockSpec((1, D_MODEL), lambda i: (0, 0)),
        ],
        out_specs=row(D_MODEL),
        out_shape=jax.ShapeDtypeStruct((seq, D_MODEL), jnp.float32),
        compiler_params=pltpu.CompilerParams(
            dimension_semantics=("arbitrary",), vmem_limit_bytes=VMEM_LIMIT_BYTES),
        name="mixer_out",
    )(x, yab, o, gate, lw["w_out"], lw["post_g"])


def _prep_layer(l, pre_norm_g, post_norm_g, w_in, sgu_w, sgu_b, sgu_ln_g, sgu_ln_b, pool_w, pool_scale,
                q_norm_g, w_uq, kv_norm_g, w_ukv, w_out):
    bf = jnp.bfloat16
    splits = (SGU_WIDTH, SGU_WIDTH, SGU_WIDTH, POOL_WIDTH, POOL_WIDTH, Q_LORA_RANK, KV_LORA_RANK,
              QK_ROPE_DIM, MLA_WIDTH)
    offs = [int(o) for o in np.cumsum(splits)[:-1]]
    w_u, w_v, w_ga, w_pin, w_pg, w_cq, w_ckv, w_kr, w_mg = jnp.split(w_in[l], offs, axis=-1)
    w_kr_rot = jnp.concatenate([-w_kr[:, ROPE_HALF:], w_kr[:, :ROPE_HALF]], axis=-1)
    w_in_r = jnp.concatenate([w_u, w_v, w_ga, w_pin, w_pg, w_cq, w_ckv, w_mg, w_kr, w_kr_rot], axis=-1)
    w_ukv_r = w_ukv[l].reshape(KV_LORA_RANK, MLA_HEADS, QK_NOPE_DIM + V_HEAD_DIM)
    w_k = w_ukv_r[:, :, :QK_NOPE_DIM].reshape(KV_LORA_RANK, MLA_HEADS * QK_NOPE_DIM)
    w_v = w_ukv_r[:, :, QK_NOPE_DIM:].reshape(KV_LORA_RANK, MLA_HEADS * V_HEAD_DIM)
    pool_bd = jax.scipy.linalg.block_diag(*[pool_w[l, g] for g in range(len(POOL_WINDOWS))])
    return {
        "pre_g": pre_norm_g[l].reshape(1, D_MODEL),
        "post_g": post_norm_g[l].reshape(1, D_MODEL),
        "w_in": w_in_r.astype(bf),
        "sgu_w": sgu_w[l].transpose(1, 0, 2).reshape(SGU_BLOCK, SGU_HEADS * SGU_BLOCK),
        "sgu_bias": jnp.repeat(sgu_b[l].T, SGU_HEAD_DIM, axis=1),
        "ln_g": sgu_ln_g[l].reshape(1, SGU_WIDTH),
        "ln_b": sgu_ln_b[l].reshape(1, SGU_WIDTH),
        "pool_w": pool_bd.astype(bf),
        "pool_scale": pool_scale[l].reshape(1, POOL_WIDTH),
        "qn_g": q_norm_g[l].reshape(1, Q_LORA_RANK),
        "w_uq_t": w_uq[l].T.astype(bf),
        "kvn_g": kv_norm_g[l].reshape(1, KV_LORA_RANK),
        "w_k": w_k.astype(bf),
        "w_v_t": w_v.T.astype(bf),
        "w_out": w_out[l].astype(bf),
    }


def kernel(x, positions, pre_norm_g, post_norm_g, w_in, sgu_w, sgu_b, sgu_ln_g, sgu_ln_b, pool_w, pool_scale,
           q_norm_g, w_uq, kv_norm_g, w_ukv, w_out):
    bsz, seq, d_model = x.shape
    assert bsz == 1 and seq == SEQ and d_model == D_MODEL
    assert seq % ROW_TILE == 0 and ROW_TILE % ATT_TILE == 0 and seq % TABLE_TILE == 0
    tables = _rope_tables(positions)
    xs = x.reshape(seq, d_model)
    for l in range(pre_norm_g.shape[0]):
        lw = _prep_layer(l, pre_norm_g, post_norm_g, w_in, sgu_w, sgu_b, sgu_ln_g, sgu_ln_b, pool_w,
                         pool_scale, q_norm_g, w_uq, kv_norm_g, w_ukv, w_out)
        yab, gate, q_t, k, v_t = _mixer_in(xs, lw, tables)
        o = _attention(q_t, k, v_t)
        xs = _mixer_out(xs, yab, o, gate, lw)
    return xs.reshape(bsz, seq, d_model)
```

```python
import functools
import math

import jax
import jax.numpy as jnp
import numpy as np
from jax import lax
from jax.experimental import pallas as pl
from jax.experimental.pallas import tpu as pltpu

D_MODEL = 1024
SEQ = 16384
CHUNK = 64
EPS = 1e-6
NEG_INF = -1e30

SGU_WIDTH = 256
SGU_HEADS = 4
SGU_HEAD_DIM = SGU_WIDTH // SGU_HEADS
SGU_BLOCK = 128

POOL_WIDTH = 256
POOL_WINDOWS = (2, 4, 8, 16)
POOL_GROUP_DIM = POOL_WIDTH // len(POOL_WINDOWS)

MLA_WIDTH = 512
MLA_HEADS = 4
V_HEAD_DIM = MLA_WIDTH // MLA_HEADS
QK_NOPE_DIM = 128
QK_ROPE_DIM = 64
QK_HEAD_DIM = QK_NOPE_DIM + QK_ROPE_DIM
Q_LORA_RANK = 384
KV_LORA_RANK = 256
ROPE_BASE = 10000.0
ROPE_HALF = QK_ROPE_DIM // 2

LANES = 128
SUBLANES = 8
MXU_DIM = 256
VMEM_LIMIT_BYTES = 48 * 1024 * 1024
ATT_VMEM_LIMIT_BYTES = 54 * 1024 * 1024

ROW_TILE = 512
ATT_TILE = 512
ATT_Q_TILE = 2 * ATT_TILE
ATT_HEADS_PER_STEP = 2
SOFTMAX_ROWS = 64
TABLE_TILE = 2048
QK_PAD_DIM = MXU_DIM
VT_ROWS = V_HEAD_DIM + 16
POOL_HALO = 32

OFF_U = 0
OFF_V = OFF_U + SGU_WIDTH
OFF_GA = OFF_V + SGU_WIDTH
OFF_PIN = OFF_GA + SGU_WIDTH
OFF_PG = OFF_PIN + POOL_WIDTH
OFF_CQ = OFF_PG + POOL_WIDTH
OFF_CKV = OFF_CQ + Q_LORA_RANK
OFF_MG = OFF_CKV + KV_LORA_RANK
OFF_KR = OFF_MG + MLA_WIDTH
D_IN_EXT = OFF_KR + 2 * QK_ROPE_DIM

Q_PRESCALE = (QK_HEAD_DIM ** -0.5) * math.log2(math.e)


def _silu(x):
    return x * (1.0 / (1.0 + jnp.exp(-x)))


def _rms(x, g):
    return x * lax.rsqrt(jnp.mean(x * x, axis=-1, keepdims=True) + EPS) * g


def _dot(a, b):
    return jnp.dot(a, b, preferred_element_type=jnp.float32)


def _dot_nt(a, b):
    return lax.dot_general(a, b, (((1,), (1,)), ((), ())), preferred_element_type=jnp.float32)


def _rope_tables_kernel(pos_row_ref, pos_col_ref, invf_col_ref, invf_row_ref,
                        cos_t_ref, sin_t_ref, cos_k_ref, sin_k_ref):
    ang_t = invf_col_ref[...] * pos_row_ref[...].astype(jnp.float32)
    cos_t_ref[...] = jnp.cos(ang_t)
    sin_t_ref[...] = jnp.sin(ang_t)
    ang = pos_col_ref[...].astype(jnp.float32) * invf_row_ref[...]
    lane = lax.broadcasted_iota(jnp.int32, ang.shape, 1)
    live = lane < QK_ROPE_DIM
    cos_k_ref[...] = jnp.where(live, jnp.cos(ang), 0.0)
    sin_k_ref[...] = jnp.where(live, jnp.sin(ang), 0.0)


def _rope_tables(positions):
    seq = positions.shape[-1]
    inv_freq = ROPE_BASE ** (-jnp.arange(0, QK_ROPE_DIM, 2, dtype=jnp.float32) / QK_ROPE_DIM)
    invf_col = inv_freq.reshape(ROPE_HALF, 1)
    invf_row = jnp.concatenate(
        [inv_freq, inv_freq, jnp.zeros((LANES - QK_ROPE_DIM,), jnp.float32)]).reshape(1, LANES)
    pos_row = positions.reshape(1, seq)
    pos_col = positions.reshape(seq, 1)
    n = seq // TABLE_TILE
    return pl.pallas_call(
        _rope_tables_kernel,
        grid=(n,),
        in_specs=[
            pl.BlockSpec((1, TABLE_TILE), lambda i: (0, i)),
            pl.BlockSpec((TABLE_TILE, 1), lambda i: (i, 0)),
            pl.BlockSpec((ROPE_HALF, 1), lambda i: (0, 0)),
            pl.BlockSpec((1, LANES), lambda i: (0, 0)),
        ],
        out_specs=[
            pl.BlockSpec((ROPE_HALF, TABLE_TILE), lambda i: (0, i)),
            pl.BlockSpec((ROPE_HALF, TABLE_TILE), lambda i: (0, i)),
            pl.BlockSpec((TABLE_TILE, LANES), lambda i: (i, 0)),
            pl.BlockSpec((TABLE_TILE, LANES), lambda i: (i, 0)),
        ],
        out_shape=[
            jax.ShapeDtypeStruct((ROPE_HALF, seq), jnp.float32),
            jax.ShapeDtypeStruct((ROPE_HALF, seq), jnp.float32),
            jax.ShapeDtypeStruct((seq, LANES), jnp.float32),
            jax.ShapeDtypeStruct((seq, LANES), jnp.float32),
        ],
        compiler_params=pltpu.CompilerParams(dimension_semantics=("arbitrary",)),
        name="rope_tables",
    )(pos_row, pos_col, invf_col, invf_row)


def _mixer_in_kernel(x_ref, pre_g_ref, w_in_ref, sgu_w_ref, sgu_bias_ref, ln_g_ref, ln_b_ref,
                     pool_w_ref, pool_scale_ref, qn_g_ref, w_uq_t_ref, kvn_g_ref, w_k_ref, w_v_t_ref,
                     cos_t_ref, sin_t_ref, cos_k_ref, sin_k_ref,
                     yab_ref, gate_ref, qt_ref, k_ref, vt_ref,
                     z_ref, ext_ref, a2_ref, a4_ref, a8_ref):
    i = pl.program_id(0)
    tm = x_ref.shape[0]
    hb = _rms(x_ref[...], pre_g_ref[...]).astype(jnp.bfloat16)
    z_ref[...] = _dot(hb, w_in_ref[...])

    def proj(off, width):
        return z_ref[:, off:off + width]

    v = proj(OFF_V, SGU_WIDTH)
    mu = jnp.mean(v, axis=-1, keepdims=True)
    vc = v - mu
    var = jnp.mean(vc * vc, axis=-1, keepdims=True)
    vn = vc * lax.rsqrt(var + EPS) * ln_g_ref[...] + ln_b_ref[...]
    w_rows = lax.broadcasted_iota(jnp.int32, (SGU_BLOCK, SGU_HEADS * SGU_BLOCK), 0)
    w_cols = lax.broadcasted_iota(jnp.int32, (SGU_BLOCK, SGU_HEADS * SGU_BLOCK), 1)
    w_keep = ((w_cols % SGU_BLOCK) // CHUNK) <= (w_rows // CHUNK)
    w_cat = jnp.where(w_keep, sgu_w_ref[...], 0.0).astype(jnp.bfloat16)
    head_of_col = lax.broadcasted_iota(jnp.int32, (SGU_BLOCK, SGU_WIDTH), 1) // SGU_HEAD_DIM
    n_blk = tm // SGU_BLOCK
    v_stacks = []
    for r in range(n_blk):
        vb = vn[r * SGU_BLOCK:(r + 1) * SGU_BLOCK, :]
        v_stacks.append(jnp.concatenate(
            [jnp.where(head_of_col == h, vb, 0.0) for h in range(SGU_HEADS)], axis=0).astype(jnp.bfloat16))
    mixed_wide = _dot(w_cat, jnp.concatenate(v_stacks, axis=1))
    mixed = jnp.concatenate(
        [mixed_wide[:, r * SGU_WIDTH:(r + 1) * SGU_WIDTH] + sgu_bias_ref[...] for r in range(n_blk)], axis=0)
    ya = proj(OFF_U, SGU_WIDTH) * mixed * _silu(proj(OFF_GA, SGU_WIDTH))
    yab_ref[:, 0:SGU_WIDTH] = ya.astype(yab_ref.dtype)

    p = proj(OFF_PIN, POOL_WIDTH)

    @pl.when(i == 0)
    def _():
        ext_ref[0:POOL_HALO, :] = jnp.zeros((POOL_HALO, POOL_WIDTH), jnp.float32)

    ext_ref[POOL_HALO:POOL_HALO + tm, :] = p
    end = POOL_HALO + tm
    a2_ref[8:end, :] = ext_ref[8:end, :] + ext_ref[7:end - 1, :]
    a4_ref[16:end, :] = a2_ref[16:end, :] + a2_ref[14:end - 2, :]
    a8_ref[24:end, :] = a4_ref[24:end, :] + a4_ref[20:end - 4, :]
    a16 = a8_ref[POOL_HALO:end, :] + a8_ref[POOL_HALO - 8:end - 8, :]
    group = lax.broadcasted_iota(jnp.int32, (tm, POOL_WIDTH), 1) // POOL_GROUP_DIM
    sums = jnp.where(group == 0, a2_ref[POOL_HALO:end, :],
                     jnp.where(group == 1, a4_ref[POOL_HALO:end, :],
                               jnp.where(group == 2, a8_ref[POOL_HALO:end, :], a16)))
    window = jnp.where(group == 0, POOL_WINDOWS[0],
                       jnp.where(group == 1, POOL_WINDOWS[1],
                                 jnp.where(group == 2, POOL_WINDOWS[2], POOL_WINDOWS[3])))
    t_glob = i * tm + lax.broadcasted_iota(jnp.int32, (tm, POOL_WIDTH), 0)
    count = jnp.minimum(t_glob + 1, window).astype(jnp.float32)
    pooled = sums / count - p
    ext_ref[0:POOL_HALO, :] = ext_ref[tm:tm + POOL_HALO, :]
    yb = _dot(pooled.astype(jnp.bfloat16), pool_w_ref[...]) * pool_scale_ref[...] * _silu(proj(OFF_PG, POOL_WIDTH))
    yab_ref[:, SGU_WIDTH:SGU_WIDTH + POOL_WIDTH] = yb.astype(yab_ref.dtype)

    gate_ref[...] = _silu(proj(OFF_MG, MLA_WIDTH)).astype(gate_ref.dtype)

    cqn = _rms(proj(OFF_CQ, Q_LORA_RANK), qn_g_ref[...]).astype(jnp.bfloat16)
    q_t = _dot_nt(w_uq_t_ref[...], cqn) * Q_PRESCALE
    cos_t = cos_t_ref[...]
    sin_t = sin_t_ref[...]
    for h in range(MLA_HEADS):
        base = h * QK_HEAD_DIM
        x1 = q_t[base + QK_NOPE_DIM:base + QK_NOPE_DIM + ROPE_HALF, :]
        x2 = q_t[base + QK_NOPE_DIM + ROPE_HALF:base + QK_HEAD_DIM, :]
        qt_ref[h, 0:QK_NOPE_DIM, :] = q_t[base:base + QK_NOPE_DIM, :].astype(qt_ref.dtype)
        qt_ref[h, QK_NOPE_DIM:QK_NOPE_DIM + ROPE_HALF, :] = (x1 * cos_t - x2 * sin_t).astype(qt_ref.dtype)
        qt_ref[h, QK_NOPE_DIM + ROPE_HALF:QK_HEAD_DIM, :] = (x2 * cos_t + x1 * sin_t).astype(qt_ref.dtype)
        qt_ref[h, QK_HEAD_DIM:QK_PAD_DIM, :] = jnp.zeros((QK_PAD_DIM - QK_HEAD_DIM, tm), qt_ref.dtype)

    ckvn = _rms(proj(OFF_CKV, KV_LORA_RANK), kvn_g_ref[...]).astype(jnp.bfloat16)
    k_nope = _dot(ckvn, w_k_ref[...])
    v_t = _dot_nt(w_v_t_ref[...], ckvn)
    kr = proj(OFF_KR, 2 * QK_ROPE_DIM)
    kr_swapped = pltpu.roll(kr, QK_ROPE_DIM, axis=1)
    k_pe = (kr * cos_k_ref[...] + kr_swapped * sin_k_ref[...]).astype(k_ref.dtype)
    n_sub = tm // ATT_TILE
    for h in range(MLA_HEADS):
        for c in range(n_sub):
            rows = slice(c * ATT_TILE, (c + 1) * ATT_TILE)
            k_ref[h, c, :, 0:QK_NOPE_DIM] = k_nope[rows, h * QK_NOPE_DIM:(h + 1) * QK_NOPE_DIM].astype(k_ref.dtype)
            k_ref[h, c, :, QK_NOPE_DIM:QK_PAD_DIM] = k_pe[rows, :]
            vt_ref[h, c, 0:V_HEAD_DIM, :] = v_t[h * V_HEAD_DIM:(h + 1) * V_HEAD_DIM, rows].astype(vt_ref.dtype)
            ones_row = lax.broadcasted_iota(jnp.int32, (VT_ROWS - V_HEAD_DIM, ATT_TILE), 0) == 0
            vt_ref[h, c, V_HEAD_DIM:VT_ROWS, :] = jnp.where(ones_row, 1.0, 0.0).astype(vt_ref.dtype)


def _mixer_in(x, lw, tables):
    seq = x.shape[0]
    tm = ROW_TILE
    n = seq // tm
    n_sub = tm // ATT_TILE
    cos_t, sin_t, cos_k, sin_k = tables

    def const(shape):
        return pl.BlockSpec(shape, lambda i: (0,) * len(shape))

    return pl.pallas_call(
        _mixer_in_kernel,
        grid=(n,),
        in_specs=[
            pl.BlockSpec((tm, D_MODEL), lambda i: (i, 0)),
            const((1, D_MODEL)),
            const((D_MODEL, D_IN_EXT)),
            const((SGU_BLOCK, SGU_HEADS * SGU_BLOCK)),
            const((SGU_BLOCK, SGU_WIDTH)),
            const((1, SGU_WIDTH)),
            const((1, SGU_WIDTH)),
            const((POOL_WIDTH, POOL_WIDTH)),
            const((1, POOL_WIDTH)),
            const((1, Q_LORA_RANK)),
            const((MLA_HEADS * QK_HEAD_DIM, Q_LORA_RANK)),
            const((1, KV_LORA_RANK)),
            const((KV_LORA_RANK, MLA_HEADS * QK_NOPE_DIM)),
            const((MLA_HEADS * V_HEAD_DIM, KV_LORA_RANK)),
            pl.BlockSpec((ROPE_HALF, tm), lambda i: (0, i)),
            pl.BlockSpec((ROPE_HALF, tm), lambda i: (0, i)),
            pl.BlockSpec((tm, LANES), lambda i: (i, 0)),
            pl.BlockSpec((tm, LANES), lambda i: (i, 0)),
        ],
        out_specs=[
            pl.BlockSpec((tm, SGU_WIDTH + POOL_WIDTH), lambda i: (i, 0)),
            pl.BlockSpec((tm, MLA_WIDTH), lambda i: (i, 0)),
            pl.BlockSpec((MLA_HEADS, QK_PAD_DIM, tm), lambda i: (0, 0, i)),
            pl.BlockSpec((MLA_HEADS, n_sub, ATT_TILE, QK_PAD_DIM), lambda i: (0, i, 0, 0)),
            pl.BlockSpec((MLA_HEADS, n_sub, VT_ROWS, ATT_TILE), lambda i: (0, i, 0, 0)),
        ],
        out_shape=[
            jax.ShapeDtypeStruct((seq, SGU_WIDTH + POOL_WIDTH), jnp.bfloat16),
            jax.ShapeDtypeStruct((seq, MLA_WIDTH), jnp.bfloat16),
            jax.ShapeDtypeStruct((MLA_HEADS, QK_PAD_DIM, seq), jnp.bfloat16),
            jax.ShapeDtypeStruct((MLA_HEADS, seq // ATT_TILE, ATT_TILE, QK_PAD_DIM), jnp.bfloat16),
            jax.ShapeDtypeStruct((MLA_HEADS, seq // ATT_TILE, VT_ROWS, ATT_TILE), jnp.bfloat16),
        ],
        scratch_shapes=[
            pltpu.VMEM((tm, D_IN_EXT), jnp.float32),
            pltpu.VMEM((POOL_HALO + tm, POOL_WIDTH), jnp.float32),
            pltpu.VMEM((POOL_HALO + tm, POOL_WIDTH), jnp.float32),
            pltpu.VMEM((POOL_HALO + tm, POOL_WIDTH), jnp.float32),
            pltpu.VMEM((POOL_HALO + tm, POOL_WIDTH), jnp.float32),
        ],
        compiler_params=pltpu.CompilerParams(
            dimension_semantics=("arbitrary",), vmem_limit_bytes=VMEM_LIMIT_BYTES),
        name="mixer_in",
    )(x, lw["pre_g"], lw["w_in"], lw["sgu_w"], lw["sgu_bias"], lw["ln_g"], lw["ln_b"],
      lw["pool_w"], lw["pool_scale"], lw["qn_g"], lw["w_uq_t"], lw["kvn_g"], lw["w_k"], lw["w_v_t"],
      cos_t, sin_t, cos_k, sin_k)


def _attention_kernel(qt_ref, k_ref, vt_ref, o_ref, m_ref, acc_ref,
                      s0_ref, s1_ref, p0_ref, p1_ref, a0_ref, a1_ref, x0_ref, x1_ref):
    i = pl.program_id(1)
    tk = ATT_TILE
    tq = ATT_Q_TILE
    n_heads = qt_ref.shape[0]
    n_tiles = (tq // tk) * (i + 1)
    s_refs = (s0_ref, s1_ref)
    p_refs = (p0_ref, p1_ref)
    a_refs = (a0_ref, a1_ref)
    x_refs = (x0_ref, x1_ref)
    m_ref[...] = jnp.full(m_ref.shape, NEG_INF, jnp.float32)
    acc_ref[...] = jnp.zeros(acc_ref.shape, jnp.float32)

    def score(slot, j, diag=None):
        for h in range(n_heads):
            s = jnp.dot(k_ref[h, j], qt_ref[h], preferred_element_type=jnp.float32)
            if diag is not None:
                key_chunk = (diag * tk + lax.broadcasted_iota(jnp.int32, (tk, tq), 0)) // CHUNK
                qry_chunk = lax.broadcasted_iota(jnp.int32, (tk, tq), 1) // CHUNK
                s = jnp.where(key_chunk <= qry_chunk, s, NEG_INF)
            s_refs[slot][h] = s
            x_refs[slot][h] = jnp.max(s, axis=0, keepdims=True)

    def softmax(slot):
        rc = SOFTMAX_ROWS
        for h in range(n_heads):
            m_old = m_ref[h]
            m_new = jnp.maximum(m_old, x_refs[slot][h])
            m_ref[h] = m_new
            a_refs[slot][h] = jnp.exp2(m_old - m_new)
            for r in range(0, tk, rc):
                d = s_refs[slot][h, r:r + rc, :] - m_new
                p_refs[slot][h, r:r + rc, :] = jnp.exp2(d.astype(jnp.bfloat16))

    def value(slot, j):
        for h in range(n_heads):
            pv = jnp.dot(vt_ref[h, j], p_refs[slot][h], preferred_element_type=jnp.float32)
            acc_ref[h] = a_refs[slot][h] * acc_ref[h] + pv

    def pair(t0, masked):
        for u in range(2):
            score(u, t0 + u, diag=u if masked else None)
            softmax(1 - u)
            value(u, t0 + u - 2)

    @pl.when(i == 0)
    def _():
        score(0, 0, diag=0)
        score(1, 1, diag=1)
        softmax(0)

    @pl.when(i > 0)
    def _():
        score(0, 0)
        score(1, 1)
        softmax(0)

        def pair_block(jj, carry):
            pair(2 + 2 * jj, masked=False)
            return carry

        lax.fori_loop(0, i - 1, pair_block, 0)
        pair(n_tiles - 2, masked=True)

    softmax(1)
    value(0, n_tiles - 2)
    value(1, n_tiles - 1)

    for h in range(n_heads):
        o_t = acc_ref[h, 0:V_HEAD_DIM, :] * (1.0 / acc_ref[h, V_HEAD_DIM:V_HEAD_DIM + 1, :])
        o_ref[:, h * V_HEAD_DIM:(h + 1) * V_HEAD_DIM] = o_t.T.astype(o_ref.dtype)


def _attention(q_t, k, v_t):
    heads, n_kv, tk, _ = k.shape
    seq = n_kv * tk
    tq = ATT_Q_TILE
    hp = ATT_HEADS_PER_STEP
    resident = dict(pipeline_mode=pl.Buffered(1))
    return pl.pallas_call(
        _attention_kernel,
        grid=(heads // hp, seq // tq),
        in_specs=[
            pl.BlockSpec((hp, QK_PAD_DIM, tq), lambda g, i: (g, 0, i)),
            pl.BlockSpec((hp, n_kv, tk, QK_PAD_DIM), lambda g, i: (g, 0, 0, 0), **resident),
            pl.BlockSpec((hp, n_kv, VT_ROWS, tk), lambda g, i: (g, 0, 0, 0), **resident),
        ],
        out_specs=pl.BlockSpec((tq, hp * V_HEAD_DIM), lambda g, i: (i, g)),
        out_shape=jax.ShapeDtypeStruct((seq, heads * V_HEAD_DIM), jnp.bfloat16),
        scratch_shapes=[
            pltpu.VMEM((hp, 1, tq), jnp.float32),
            pltpu.VMEM((hp, VT_ROWS, tq), jnp.float32),
            pltpu.VMEM((hp, tk, tq), jnp.float32),
            pltpu.VMEM((hp, tk, tq), jnp.float32),
            pltpu.VMEM((hp, tk, tq), jnp.bfloat16),
            pltpu.VMEM((hp, tk, tq), jnp.bfloat16),
            pltpu.VMEM((hp, 1, tq), jnp.float32),
            pltpu.VMEM((hp, 1, tq), jnp.float32),
            pltpu.VMEM((hp, 1, tq), jnp.float32),
            pltpu.VMEM((hp, 1, tq), jnp.float32),
        ],
        compiler_params=pltpu.CompilerParams(
            dimension_semantics=("arbitrary", "arbitrary"), vmem_limit_bytes=ATT_VMEM_LIMIT_BYTES),
        name="attention",
    )(q_t, k, v_t)


def _mixer_out_kernel(x_ref, yab_ref, o_ref, gate_ref, w_out_ref, post_g_ref, out_ref):
    yc = (o_ref[...].astype(jnp.float32) * gate_ref[...].astype(jnp.float32)).astype(jnp.bfloat16)
    half = SGU_WIDTH + POOL_WIDTH
    y = _dot(yab_ref[...], w_out_ref[0:half, :]) + _dot(yc, w_out_ref[half:, :])
    out_ref[...] = x_ref[...] + _rms(y, post_g_ref[...])


def _mixer_out(x, yab, o, gate, lw):
    seq = x.shape[0]
    tm = ROW_TILE
    row = lambda width: pl.BlockSpec((tm, width), lambda i: (i, 0))
    return pl.pallas_call(
        _mixer_out_kernel,
        grid=(seq // tm,),
        in_specs=[
            row(D_MODEL), row(SGU_WIDTH + POOL_WIDTH), row(MLA_WIDTH), row(MLA_WIDTH),
            pl.BlockSpec((D_MODEL, D_MODEL), lambda i: (0, 0)),
            pl.BlockSpec((1, D_MODEL), lambda i: (0, 0)),
        ],
        out_specs=row(D_MODEL),
        out_shape=jax.ShapeDtypeStruct((seq, D_MODEL), jnp.float32),
        compiler_params=pltpu.CompilerParams(
            dimension_semantics=("arbitrary",), vmem_limit_bytes=VMEM_LIMIT_BYTES),
        name="mixer_out",
    )(x, yab, o, gate, lw["w_out"], lw["post_g"])


def _prep_layer(l, pre_norm_g, post_norm_g, w_in, sgu_w, sgu_b, sgu_ln_g, sgu_ln_b, pool_w, pool_scale,
                q_norm_g, w_uq, kv_norm_g, w_ukv, w_out):
    bf = jnp.bfloat16
    splits = (SGU_WIDTH, SGU_WIDTH, SGU_WIDTH, POOL_WIDTH, POOL_WIDTH, Q_LORA_RANK, KV_LORA_RANK,
              QK_ROPE_DIM, MLA_WIDTH)
    offs = [int(o) for o in np.cumsum(splits)[:-1]]
    w_u, w_v, w_ga, w_pin, w_pg, w_cq, w_ckv, w_kr, w_mg = jnp.split(w_in[l], offs, axis=-1)
    w_kr_rot = jnp.concatenate([-w_kr[:, ROPE_HALF:], w_kr[:, :ROPE_HALF]], axis=-1)
    w_in_r = jnp.concatenate([w_u, w_v, w_ga, w_pin, w_pg, w_cq, w_ckv, w_mg, w_kr, w_kr_rot], axis=-1)
    w_ukv_r = w_ukv[l].reshape(KV_LORA_RANK, MLA_HEADS, QK_NOPE_DIM + V_HEAD_DIM)
    w_k = w_ukv_r[:, :, :QK_NOPE_DIM].reshape(KV_LORA_RANK, MLA_HEADS * QK_NOPE_DIM)
    w_v = w_ukv_r[:, :, QK_NOPE_DIM:].reshape(KV_LORA_RANK, MLA_HEADS * V_HEAD_DIM)
    pool_bd = jax.scipy.linalg.block_diag(*[pool_w[l, g] for g in range(len(POOL_WINDOWS))])
    return {
        "pre_g": pre_norm_g[l].reshape(1, D_MODEL),
        "post_g": post_norm_g[l].reshape(1, D_MODEL),
        "w_in": w_in_r.astype(bf),
        "sgu_w": sgu_w[l].transpose(1, 0, 2).reshape(SGU_BLOCK, SGU_HEADS * SGU_BLOCK),
        "sgu_bias": jnp.repeat(sgu_b[l].T, SGU_HEAD_DIM, axis=1),
        "ln_g": sgu_ln_g[l].reshape(1, SGU_WIDTH),
        "ln_b": sgu_ln_b[l].reshape(1, SGU_WIDTH),
        "pool_w": pool_bd.astype(bf),
        "pool_scale": pool_scale[l].reshape(1, POOL_WIDTH),
        "qn_g": q_norm_g[l].reshape(1, Q_LORA_RANK),
        "w_uq_t": w_uq[l].T.astype(bf),
        "kvn_g": kv_norm_g[l].reshape(1, KV_LORA_RANK),
        "w_k": w_k.astype(bf),
        "w_v_t": w_v.T.astype(bf),
        "w_out": w_out[l].astype(bf),
    }


def kernel(x, positions, pre_norm_g, post_norm_g, w_in, sgu_w, sgu_b, sgu_ln_g, sgu_ln_b, pool_w, pool_scale,
           q_norm_g, w_uq, kv_norm_g, w_ukv, w_out):
    bsz, seq, d_model = x.shape
    assert bsz == 1 and seq == SEQ and d_model == D_MODEL
    assert seq % ROW_TILE == 0 and ROW_TILE % ATT_TILE == 0 and seq % TABLE_TILE == 0
    tables = _rope_tables(positions)
    xs = x.reshape(seq, d_model)
    for l in range(pre_norm_g.shape[0]):
        lw = _prep_layer(l, pre_norm_g, post_norm_g, w_in, sgu_w, sgu_b, sgu_ln_g, sgu_ln_b, pool_w,
                         pool_scale, q_norm_g, w_uq, kv_norm_g, w_ukv, w_out)
        yab, gate, q_t, k, v_t = _mixer_in(xs, lw, tables)
        o = _attention(q_t, k, v_t)
        xs = _mixer_out(xs, yab, o, gate, lw)
    return xs.reshape(bsz, seq, d_model)
```

```python
import functools
import math

import jax
import jax.numpy as jnp
import numpy as np
from jax import lax
from jax.experimental import pallas as pl
from jax.experimental.pallas import tpu as pltpu

D_MODEL = 1024
SEQ = 16384
CHUNK = 64
EPS = 1e-6
NEG_INF = -1e30

SGU_WIDTH = 256
SGU_HEADS = 4
SGU_HEAD_DIM = SGU_WIDTH // SGU_HEADS
SGU_BLOCK = 128

POOL_WIDTH = 256
POOL_WINDOWS = (2, 4, 8, 16)
POOL_GROUP_DIM = POOL_WIDTH // len(POOL_WINDOWS)

MLA_WIDTH = 512
MLA_HEADS = 4
V_HEAD_DIM = MLA_WIDTH // MLA_HEADS
QK_NOPE_DIM = 128
QK_ROPE_DIM = 64
QK_HEAD_DIM = QK_NOPE_DIM + QK_ROPE_DIM
Q_LORA_RANK = 384
KV_LORA_RANK = 256
ROPE_BASE = 10000.0
ROPE_HALF = QK_ROPE_DIM // 2

LANES = 128
SUBLANES = 8
MXU_DIM = 256
VMEM_LIMIT_BYTES = 48 * 1024 * 1024
ATT_VMEM_LIMIT_BYTES = 54 * 1024 * 1024

ROW_TILE = 512
ATT_TILE = 512
ATT_Q_TILE = 2 * ATT_TILE
ATT_HEADS_PER_STEP = 2
SOFTMAX_ROWS = 64
TABLE_TILE = 2048
QK_PAD_DIM = MXU_DIM
VT_ROWS = V_HEAD_DIM + 16
POOL_HALO = 32

OFF_U = 0
OFF_V = OFF_U + SGU_WIDTH
OFF_GA = OFF_V + SGU_WIDTH
OFF_PIN = OFF_GA + SGU_WIDTH
OFF_PG = OFF_PIN + POOL_WIDTH
OFF_CQ = OFF_PG + POOL_WIDTH
OFF_CKV = OFF_CQ + Q_LORA_RANK
OFF_MG = OFF_CKV + KV_LORA_RANK
OFF_KR = OFF_MG + MLA_WIDTH
D_IN_EXT = OFF_KR + 2 * QK_ROPE_DIM

Q_PRESCALE = (QK_HEAD_DIM ** -0.5) * math.log2(math.e)


def _silu(x):
    return x * (1.0 / (1.0 + jnp.exp(-x)))


def _rms(x, g):
    return x * lax.rsqrt(jnp.mean(x * x, axis=-1, keepdims=True) + EPS) * g


def _dot(a, b):
    return jnp.dot(a, b, preferred_element_type=jnp.float32)


def _dot_nt(a, b):
    return lax.dot_general(a, b, (((1,), (1,)), ((), ())), preferred_element_type=jnp.float32)


def _rope_tables_kernel(pos_row_ref, pos_col_ref, invf_col_ref, invf_row_ref,
                        cos_t_ref, sin_t_ref, cos_k_ref, sin_k_ref):
    ang_t = invf_col_ref[...] * pos_row_ref[...].astype(jnp.float32)
    cos_t_ref[...] = jnp.cos(ang_t)
    sin_t_ref[...] = jnp.sin(ang_t)
    ang = pos_col_ref[...].astype(jnp.float32) * invf_row_ref[...]
    lane = lax.broadcasted_iota(jnp.int32, ang.shape, 1)
    live = lane < QK_ROPE_DIM
    cos_k_ref[...] = jnp.where(live, jnp.cos(ang), 0.0)
    sin_k_ref[...] = jnp.where(live, jnp.sin(ang), 0.0)


def _rope_tables(positions):
    seq = positions.shape[-1]
    inv_freq = ROPE_BASE ** (-jnp.arange(0, QK_ROPE_DIM, 2, dtype=jnp.float32) / QK_ROPE_DIM)
    invf_col = inv_freq.reshape(ROPE_HALF, 1)
    invf_row = jnp.concatenate(
        [inv_freq, inv_freq, jnp.zeros((LANES - QK_ROPE_DIM,), jnp.float32)]).reshape(1, LANES)
    pos_row = positions.reshape(1, seq)
    pos_col = positions.reshape(seq, 1)
    n = seq // TABLE_TILE
    return pl.pallas_call(
        _rope_tables_kernel,
        grid=(n,),
        in_specs=[
            pl.BlockSpec((1, TABLE_TILE), lambda i: (0, i)),
            pl.BlockSpec((TABLE_TILE, 1), lambda i: (i, 0)),
            pl.BlockSpec((ROPE_HALF, 1), lambda i: (0, 0)),
            pl.BlockSpec((1, LANES), lambda i: (0, 0)),
        ],
        out_specs=[
            pl.BlockSpec((ROPE_HALF, TABLE_TILE), lambda i: (0, i)),
            pl.BlockSpec((ROPE_HALF, TABLE_TILE), lambda i: (0, i)),
            pl.BlockSpec((TABLE_TILE, LANES), lambda i: (i, 0)),
            pl.BlockSpec((TABLE_TILE, LANES), lambda i: (i, 0)),
        ],
        out_shape=[
            jax.ShapeDtypeStruct((ROPE_HALF, seq), jnp.float32),
            jax.ShapeDtypeStruct((ROPE_HALF, seq), jnp.float32),
            jax.ShapeDtypeStruct((seq, LANES), jnp.float32),
            jax.ShapeDtypeStruct((seq, LANES), jnp.float32),
        ],
        compiler_params=pltpu.CompilerParams(dimension_semantics=("arbitrary",)),
        name="rope_tables",
    )(pos_row, pos_col, invf_col, invf_row)


def _mixer_in_kernel(x_ref, pre_g_ref, w_in_ref, sgu_w_ref, sgu_bias_ref, ln_g_ref, ln_b_ref,
                     pool_w_ref, pool_scale_ref, qn_g_ref, w_uq_t_ref, kvn_g_ref, w_k_ref, w_v_t_ref,
                     cos_t_ref, sin_t_ref, cos_k_ref, sin_k_ref,
                     yab_ref, gate_ref, qt_ref, k_ref, vt_ref,
                     z_ref, ext_ref, a2_ref, a4_ref, a8_ref):
    i = pl.program_id(0)
    tm = x_ref.shape[0]
    hb = _rms(x_ref[...], pre_g_ref[...]).astype(jnp.bfloat16)
    z_ref[...] = _dot(hb, w_in_ref[...])

    def proj(off, width):
        return z_ref[:, off:off + width]

    v = proj(OFF_V, SGU_WIDTH)
    mu = jnp.mean(v, axis=-1, keepdims=True)
    vc = v - mu
    var = jnp.mean(vc * vc, axis=-1, keepdims=True)
    vn = vc * lax.rsqrt(var + EPS) * ln_g_ref[...] + ln_b_ref[...]
    w_rows = lax.broadcasted_iota(jnp.int32, (SGU_BLOCK, SGU_HEADS * SGU_BLOCK), 0)
    w_cols = lax.broadcasted_iota(jnp.int32, (SGU_BLOCK, SGU_HEADS * SGU_BLOCK), 1)
    w_keep = ((w_cols % SGU_BLOCK) // CHUNK) <= (w_rows // CHUNK)
    w_cat = jnp.where(w_keep, sgu_w_ref[...], 0.0).astype(jnp.bfloat16)
    head_of_col = lax.broadcasted_iota(jnp.int32, (SGU_BLOCK, SGU_WIDTH), 1) // SGU_HEAD_DIM
    n_blk = tm // SGU_BLOCK
    v_stacks = []
    for r in range(n_blk):
        vb = vn[r * SGU_BLOCK:(r + 1) * SGU_BLOCK, :]
        v_stacks.append(jnp.concatenate(
            [jnp.where(head_of_col == h, vb, 0.0) for h in range(SGU_HEADS)], axis=0).astype(jnp.bfloat16))
    mixed_wide = _dot(w_cat, jnp.concatenate(v_stacks, axis=1))
    mixed = jnp.concatenate(
        [mixed_wide[:, r * SGU_WIDTH:(r + 1) * SGU_WIDTH] + sgu_bias_ref[...] for r in range(n_blk)], axis=0)
    ya = proj(OFF_U, SGU_WIDTH) * mixed * _silu(proj(OFF_GA, SGU_WIDTH))
    yab_ref[:, 0:SGU_WIDTH] = ya.astype(yab_ref.dtype)

    p = proj(OFF_PIN, POOL_WIDTH)

    @pl.when(i == 0)
    def _():
        ext_ref[0:POOL_HALO, :] = jnp.zeros((POOL_HALO, POOL_WIDTH), jnp.float32)

    ext_ref[POOL_HALO:POOL_HALO + tm, :] = p
    end = POOL_HALO + tm
    a2_ref[8:end, :] = ext_ref[8:end, :] + ext_ref[7:end - 1, :]
    a4_ref[16:end, :] = a2_ref[16:end, :] + a2_ref[14:end - 2, :]
    a8_ref[24:end, :] = a4_ref[24:end, :] + a4_ref[20:end - 4, :]
    a16 = a8_ref[POOL_HALO:end, :] + a8_ref[POOL_HALO - 8:end - 8, :]
    group = lax.broadcasted_iota(jnp.int32, (tm, POOL_WIDTH), 1) // POOL_GROUP_DIM
    sums = jnp.where(group == 0, a2_ref[POOL_HALO:end, :],
                     jnp.where(group == 1, a4_ref[POOL_HALO:end, :],
                               jnp.where(group == 2, a8_ref[POOL_HALO:end, :], a16)))
    window = jnp.where(group == 0, POOL_WINDOWS[0],
                       jnp.where(group == 1, POOL_WINDOWS[1],
                                 jnp.where(group == 2, POOL_WINDOWS[2], POOL_WINDOWS[3])))
    t_glob = i * tm + lax.broadcasted_iota(jnp.int32, (tm, POOL_WIDTH), 0)
    count = jnp.minimum(t_glob + 1, window).astype(jnp.float32)
    pooled = sums / count - p
    ext_ref[0:POOL_HALO, :] = ext_ref[tm:tm + POOL_HALO, :]
    yb = _dot(pooled.astype(jnp.bfloat16), pool_w_ref[...]) * pool_scale_ref[...] * _silu(proj(OFF_PG, POOL_WIDTH))
    yab_ref[:, SGU_WIDTH:SGU_WIDTH + POOL_WIDTH] = yb.astype(yab_ref.dtype)

    gate_ref[...] = _silu(proj(OFF_MG, MLA_WIDTH)).astype(gate_ref.dtype)

    cqn = _rms(proj(OFF_CQ, Q_LORA_RANK), qn_g_ref[...]).astype(jnp.bfloat16)
    q_t = _dot_nt(w_uq_t_ref[...], cqn) * Q_PRESCALE
    cos_t = cos_t_ref[...]
    sin_t = sin_t_ref[...]
    for h in range(MLA_HEADS):
        base = h * QK_HEAD_DIM
        x1 = q_t[base + QK_NOPE_DIM:base + QK_NOPE_DIM + ROPE_HALF, :]
        x2 = q_t[base + QK_NOPE_DIM + ROPE_HALF:base + QK_HEAD_DIM, :]
        qt_ref[h, 0:QK_NOPE_DIM, :] = q_t[base:base + QK_NOPE_DIM, :].astype(qt_ref.dtype)
        qt_ref[h, QK_NOPE_DIM:QK_NOPE_DIM + ROPE_HALF, :] = (x1 * cos_t - x2 * sin_t).astype(qt_ref.dtype)
        qt_ref[h, QK_NOPE_DIM + ROPE_HALF:QK_HEAD_DIM, :] = (x2 * cos_t + x1 * sin_t).astype(qt_ref.dtype)
        qt_ref[h, QK_HEAD_DIM:QK_PAD_DIM, :] = jnp.zeros((QK_PAD_DIM - QK_HEAD_DIM, tm), qt_ref.dtype)

    ckvn = _rms(proj(OFF_CKV, KV_LORA_RANK), kvn_g_ref[...]).astype(jnp.bfloat16)
    k_nope = _dot(ckvn, w_k_ref[...])
    v_t = _dot_nt(w_v_t_ref[...], ckvn)
    kr = proj(OFF_KR, 2 * QK_ROPE_DIM)
    kr_swapped = pltpu.roll(kr, QK_ROPE_DIM, axis=1)
    k_pe = (kr * cos_k_ref[...] + kr_swapped * sin_k_ref[...]).astype(k_ref.dtype)
    n_sub = tm // ATT_TILE
    for h in range(MLA_HEADS):
        for c in range(n_sub):
            rows = slice(c * ATT_TILE, (c + 1) * ATT_TILE)
            k_ref[h, c, :, 0:QK_NOPE_DIM] = k_nope[rows, h * QK_NOPE_DIM:(h + 1) * QK_NOPE_DIM].astype(k_ref.dtype)
            k_ref[h, c, :, QK_NOPE_DIM:QK_PAD_DIM] = k_pe[rows, :]
            vt_ref[h, c, 0:V_HEAD_DIM, :] = v_t[h * V_HEAD_DIM:(h + 1) * V_HEAD_DIM, rows].astype(vt_ref.dtype)
            ones_row = lax.broadcasted_iota(jnp.int32, (VT_ROWS - V_HEAD_DIM, ATT_TILE), 0) == 0
            vt_ref[h, c, V_HEAD_DIM:VT_ROWS, :] = jnp.where(ones_row, 1.0, 0.0).astype(vt_ref.dtype)


def _mixer_in(x, lw, tables):
    seq = x.shape[0]
    tm = ROW_TILE
    n = seq // tm
    n_sub = tm // ATT_TILE
    cos_t, sin_t, cos_k, sin_k = tables

    def const(shape):
        return pl.BlockSpec(shape, lambda i: (0,) * len(shape))

    return pl.pallas_call(
        _mixer_in_kernel,
        grid=(n,),
        in_specs=[
            pl.BlockSpec((tm, D_MODEL), lambda i: (i, 0)),
            const((1, D_MODEL)),
            const((D_MODEL, D_IN_EXT)),
            const((SGU_BLOCK, SGU_HEADS * SGU_BLOCK)),
            const((SGU_BLOCK, SGU_WIDTH)),
            const((1, SGU_WIDTH)),
            const((1, SGU_WIDTH)),
            const((POOL_WIDTH, POOL_WIDTH)),
            const((1, POOL_WIDTH)),
            const((1, Q_LORA_RANK)),
            const((MLA_HEADS * QK_HEAD_DIM, Q_LORA_RANK)),
            const((1, KV_LORA_RANK)),
            const((KV_LORA_RANK, MLA_HEADS * QK_NOPE_DIM)),
            const((MLA_HEADS * V_HEAD_DIM, KV_LORA_RANK)),
            pl.BlockSpec((ROPE_HALF, tm), lambda i: (0, i)),
            pl.BlockSpec((ROPE_HALF, tm), lambda i: (0, i)),
            pl.BlockSpec((tm, LANES), lambda i: (i, 0)),
            pl.BlockSpec((tm, LANES), lambda i: (i, 0)),
        ],
        out_specs=[
            pl.BlockSpec((tm, SGU_WIDTH + POOL_WIDTH), lambda i: (i, 0)),
            pl.BlockSpec((tm, MLA_WIDTH), lambda i: (i, 0)),
            pl.BlockSpec((MLA_HEADS, QK_PAD_DIM, tm), lambda i: (0, 0, i)),
            pl.BlockSpec((MLA_HEADS, n_sub, ATT_TILE, QK_PAD_DIM), lambda i: (0, i, 0, 0)),
            pl.BlockSpec((MLA_HEADS, n_sub, VT_ROWS, ATT_TILE), lambda i: (0, i, 0, 0)),
        ],
        out_shape=[
            jax.ShapeDtypeStruct((seq, SGU_WIDTH + POOL_WIDTH), jnp.bfloat16),
            jax.ShapeDtypeStruct((seq, MLA_WIDTH), jnp.bfloat16),
            jax.ShapeDtypeStruct((MLA_HEADS, QK_PAD_DIM, seq), jnp.bfloat16),
            jax.ShapeDtypeStruct((MLA_HEADS, seq // ATT_TILE, ATT_TILE, QK_PAD_DIM), jnp.bfloat16),
            jax.ShapeDtypeStruct((MLA_HEADS, seq // ATT_TILE, VT_ROWS, ATT_TILE), jnp.bfloat16),
        ],
        scratch_shapes=[
            pltpu.VMEM((tm, D_IN_EXT), jnp.float32),
            pltpu.VMEM((POOL_HALO + tm, POOL_WIDTH), jnp.float32),
            pltpu.VMEM((POOL_HALO + tm, POOL_WIDTH), jnp.float32),
            pltpu.VMEM((POOL_HALO + tm, POOL_WIDTH), jnp.float32),
            pltpu.VMEM((POOL_HALO + tm, POOL_WIDTH), jnp.float32),
        ],
        compiler_params=pltpu.CompilerParams(
            dimension_semantics=("arbitrary",), vmem_limit_bytes=VMEM_LIMIT_BYTES),
        name="mixer_in",
    )(x, lw["pre_g"], lw["w_in"], lw["sgu_w"], lw["sgu_bias"], lw["ln_g"], lw["ln_b"],
      lw["pool_w"], lw["pool_scale"], lw["qn_g"], lw["w_uq_t"], lw["kvn_g"], lw["w_k"], lw["w_v_t"],
      cos_t, sin_t, cos_k, sin_k)


def _attention_kernel(qt_ref, qt_next_ref, k_ref, vt_ref, o_ref, m_ref, acc_ref,
                      s0_ref, s1_ref, p0_ref, p1_ref, a0_ref, a1_ref, x0_ref, x1_ref):
    i = pl.program_id(1)
    last = pl.num_programs(1) - 1
    tk = ATT_TILE
    tq = ATT_Q_TILE
    n_heads = qt_ref.shape[0]
    n_tiles = (tq // tk) * (i + 1)
    s_refs = (s0_ref, s1_ref)
    p_refs = (p0_ref, p1_ref)
    a_refs = (a0_ref, a1_ref)
    x_refs = (x0_ref, x1_ref)
    acc_ref[...] = jnp.zeros(acc_ref.shape, jnp.float32)

    def score(slot, j, diag=None, q_ref=qt_ref):
        for h in range(n_heads):
            s = jnp.dot(k_ref[h, j], q_ref[h], preferred_element_type=jnp.float32)
            if diag is not None:
                key_chunk = (diag * tk + lax.broadcasted_iota(jnp.int32, (tk, tq), 0)) // CHUNK
                qry_chunk = lax.broadcasted_iota(jnp.int32, (tk, tq), 1) // CHUNK
                s = jnp.where(key_chunk <= qry_chunk, s, NEG_INF)
            s_refs[slot][h] = s
            x_refs[slot][h] = jnp.max(s, axis=0, keepdims=True)

    def softmax(slot, first=False):
        rc = SOFTMAX_ROWS
        for h in range(n_heads):
            m_old = jnp.full((1, tq), NEG_INF, jnp.float32) if first else m_ref[h]
            m_new = jnp.maximum(m_old, x_refs[slot][h])
            m_ref[h] = m_new
            a_refs[slot][h] = jnp.exp2(m_old - m_new)
            for r in range(0, tk, rc):
                d = s_refs[slot][h, r:r + rc, :] - m_new
                p_refs[slot][h, r:r + rc, :] = jnp.exp2(d.astype(jnp.bfloat16))

    def value(slot, j):
        for h in range(n_heads):
            pv = jnp.dot(vt_ref[h, j], p_refs[slot][h], preferred_element_type=jnp.float32)
            acc_ref[h] = a_refs[slot][h] * acc_ref[h] + pv

    def pair(t0, masked):
        for u in range(2):
            score(u, t0 + u, diag=u if masked else None)
            softmax(1 - u)
            value(u, t0 + u - 2)

    def finish():
        for h in range(n_heads):
            o_t = acc_ref[h, 0:V_HEAD_DIM, :] * (1.0 / acc_ref[h, V_HEAD_DIM:V_HEAD_DIM + 1, :])
            o_ref[:, h * V_HEAD_DIM:(h + 1) * V_HEAD_DIM] = o_t.T.astype(o_ref.dtype)

    @pl.when(i == 0)
    def _():
        score(0, 0, diag=0)
        score(1, 1, diag=1)
        softmax(0, first=True)

    @pl.when(i > 0)
    def _():
        def pair_block(jj, carry):
            pair(2 + 2 * jj, masked=False)
            return carry

        lax.fori_loop(0, i - 1, pair_block, 0)
        pair(n_tiles - 2, masked=True)

    @pl.when(i < last)
    def _():
        score(0, 0, q_ref=qt_next_ref)
        softmax(1)
        value(0, n_tiles - 2)
        score(1, 1, q_ref=qt_next_ref)
        value(1, n_tiles - 1)
        finish()
        softmax(0, first=True)

    @pl.when(i == last)
    def _():
        softmax(1)
        value(0, n_tiles - 2)
        value(1, n_tiles - 1)
        finish()


def _attention(q_t, k, v_t):
    heads, n_kv, tk, _ = k.shape
    seq = n_kv * tk
    tq = ATT_Q_TILE
    hp = ATT_HEADS_PER_STEP
    resident = dict(pipeline_mode=pl.Buffered(1))
    n_q = seq // tq
    return pl.pallas_call(
        _attention_kernel,
        grid=(heads // hp, n_q),
        in_specs=[
            pl.BlockSpec((hp, QK_PAD_DIM, tq), lambda g, i: (g, 0, i)),
            pl.BlockSpec((hp, QK_PAD_DIM, tq), lambda g, i: (g, 0, jnp.minimum(i + 1, n_q - 1))),
            pl.BlockSpec((hp, n_kv, tk, QK_PAD_DIM), lambda g, i: (g, 0, 0, 0), **resident),
            pl.BlockSpec((hp, n_kv, VT_ROWS, tk), lambda g, i: (g, 0, 0, 0), **resident),
        ],
        out_specs=pl.BlockSpec((tq, hp * V_HEAD_DIM), lambda g, i: (i, g)),
        out_shape=jax.ShapeDtypeStruct((seq, heads * V_HEAD_DIM), jnp.bfloat16),
        scratch_shapes=[
            pltpu.VMEM((hp, 1, tq), jnp.float32),
            pltpu.VMEM((hp, VT_ROWS, tq), jnp.float32),
            pltpu.VMEM((hp, tk, tq), jnp.float32),
            pltpu.VMEM((hp, tk, tq), jnp.float32),
            pltpu.VMEM((hp, tk, tq), jnp.bfloat16),
            pltpu.VMEM((hp, tk, tq), jnp.bfloat16),
            pltpu.VMEM((hp, 1, tq), jnp.float32),
            pltpu.VMEM((hp, 1, tq), jnp.float32),
            pltpu.VMEM((hp, 1, tq), jnp.float32),
            pltpu.VMEM((hp, 1, tq), jnp.float32),
        ],
        compiler_params=pltpu.CompilerParams(
            dimension_semantics=("arbitrary", "arbitrary"), vmem_limit_bytes=ATT_VMEM_LIMIT_BYTES),
        name="attention",
    )(q_t, q_t, k, v_t)


def _mixer_out_kernel(x_ref, yab_ref, o_ref, gate_ref, w_out_ref, post_g_ref, out_ref):
    yc = (o_ref[...].astype(jnp.float32) * gate_ref[...].astype(jnp.float32)).astype(jnp.bfloat16)
    half = SGU_WIDTH + POOL_WIDTH
    y = _dot(yab_ref[...], w_out_ref[0:half, :]) + _dot(yc, w_out_ref[half:, :])
    out_ref[...] = x_ref[...] + _rms(y, post_g_ref[...])


def _mixer_out(x, yab, o, gate, lw):
    seq = x.shape[0]
    tm = ROW_TILE
    row = lambda width: pl.BlockSpec((tm, width), lambda i: (i, 0))
    return pl.pallas_call(
        _mixer_out_kernel,
        grid=(seq // tm,),
        in_specs=[
            row(D_MODEL), row(SGU_WIDTH + POOL_WIDTH), row(MLA_WIDTH), row(MLA_WIDTH),
            pl.BlockSpec((D_MODEL, D_MODEL), lambda i: (0, 0)),
            pl.BlockSpec((1, D_MODEL), lambda i: (0, 0)),
        ],
        out_specs=row(D_MODEL),
        out_shape=jax.ShapeDtypeStruct((seq, D_MODEL), jnp.float32),
        compiler_params=pltpu.CompilerParams(
            dimension_semantics=("arbitrary",), vmem_limit_bytes=VMEM_LIMIT_BYTES),
        name="mixer_out",
    )(x, yab, o, gate, lw["w_out"], lw["post_g"])


def _prep_layer(l, pre_norm_g, post_norm_g, w_in, sgu_w, sgu_b, sgu_ln_g, sgu_ln_b, pool_w, pool_scale,
                q_norm_g, w_uq, kv_norm_g, w_ukv, w_out):
    bf = jnp.bfloat16
    splits = (SGU_WIDTH, SGU_WIDTH, SGU_WIDTH, POOL_WIDTH, POOL_WIDTH, Q_LORA_RANK, KV_LORA_RANK,
              QK_ROPE_DIM, MLA_WIDTH)
    offs = [int(o) for o in np.cumsum(splits)[:-1]]
    w_u, w_v, w_ga, w_pin, w_pg, w_cq, w_ckv, w_kr, w_mg = jnp.split(w_in[l], offs, axis=-1)
    w_kr_rot = jnp.concatenate([-w_kr[:, ROPE_HALF:], w_kr[:, :ROPE_HALF]], axis=-1)
    w_in_r = jnp.concatenate([w_u, w_v, w_ga, w_pin, w_pg, w_cq, w_ckv, w_mg, w_kr, w_kr_rot], axis=-1)
    w_ukv_r = w_ukv[l].reshape(KV_LORA_RANK, MLA_HEADS, QK_NOPE_DIM + V_HEAD_DIM)
    w_k = w_ukv_r[:, :, :QK_NOPE_DIM].reshape(KV_LORA_RANK, MLA_HEADS * QK_NOPE_DIM)
    w_v = w_ukv_r[:, :, QK_NOPE_DIM:].reshape(KV_LORA_RANK, MLA_HEADS * V_HEAD_DIM)
    pool_bd = jax.scipy.linalg.block_diag(*[pool_w[l, g] for g in range(len(POOL_WINDOWS))])
    return {
        "pre_g": pre_norm_g[l].reshape(1, D_MODEL),
        "post_g": post_norm_g[l].reshape(1, D_MODEL),
        "w_in": w_in_r.astype(bf),
        "sgu_w": sgu_w[l].transpose(1, 0, 2).reshape(SGU_BLOCK, SGU_HEADS * SGU_BLOCK),
        "sgu_bias": jnp.repeat(sgu_b[l].T, SGU_HEAD_DIM, axis=1),
        "ln_g": sgu_ln_g[l].reshape(1, SGU_WIDTH),
        "ln_b": sgu_ln_b[l].reshape(1, SGU_WIDTH),
        "pool_w": pool_bd.astype(bf),
        "pool_scale": pool_scale[l].reshape(1, POOL_WIDTH),
        "qn_g": q_norm_g[l].reshape(1, Q_LORA_RANK),
        "w_uq_t": w_uq[l].T.astype(bf),
        "kvn_g": kv_norm_g[l].reshape(1, KV_LORA_RANK),
        "w_k": w_k.astype(bf),
        "w_v_t": w_v.T.astype(bf),
        "w_out": w_out[l].astype(bf),
    }


def kernel(x, positions, pre_norm_g, post_norm_g, w_in, sgu_w, sgu_b, sgu_ln_g, sgu_ln_b, pool_w, pool_scale,
           q_norm_g, w_uq, kv_norm_g, w_ukv, w_out):
    bsz, seq, d_model = x.shape
    assert bsz == 1 and seq == SEQ and d_model == D_MODEL
    assert seq % ROW_TILE == 0 and ROW_TILE % ATT_TILE == 0 and seq % TABLE_TILE == 0
    tables = _rope_tables(positions)
    xs = x.reshape(seq, d_model)
    for l in range(pre_norm_g.shape[0]):
        lw = _prep_layer(l, pre_norm_g, post_norm_g, w_in, sgu_w, sgu_b, sgu_ln_g, sgu_ln_b, pool_w,
                         pool_scale, q_norm_g, w_uq, kv_norm_g, w_ukv, w_out)
        yab, gate, q_t, k, v_t = _mixer_in(xs, lw, tables)
        o = _attention(q_t, k, v_t)
        xs = _mixer_out(xs, yab, o, gate, lw)
    return xs.reshape(bsz, seq, d_model)
```

```python
import math

import jax
import jax.numpy as jnp
from jax import lax
from jax.experimental import pallas as pl
from jax.experimental.pallas import tpu as pltpu

D_MODEL = 1024
SEQ = 16384
CHUNK = 64
EPS = 1e-6
NEG_INF = -1e30

SGU_WIDTH = 256
SGU_HEADS = 4
SGU_HEAD_DIM = SGU_WIDTH // SGU_HEADS
SGU_BLOCK = 128

POOL_WIDTH = 256
POOL_WINDOWS = (2, 4, 8, 16)
POOL_GROUP_DIM = POOL_WIDTH // len(POOL_WINDOWS)

MLA_WIDTH = 512
MLA_HEADS = 4
V_HEAD_DIM = MLA_WIDTH // MLA_HEADS
QK_NOPE_DIM = 128
QK_ROPE_DIM = 64
QK_HEAD_DIM = QK_NOPE_DIM + QK_ROPE_DIM
Q_LORA_RANK = 384
KV_LORA_RANK = 256
ROPE_BASE = 10000.0
ROPE_HALF = QK_ROPE_DIM // 2

LANES = 128
SUBLANES = 8
MXU_DIM = 256
VMEM_LIMIT_BYTES = 48 * 1024 * 1024
ATT_VMEM_LIMIT_BYTES = 54 * 1024 * 1024

ROW_TILE = 512
OUT_ROW_TILE = 1024
ATT_TILE = 512
ATT_Q_TILE = 2 * ATT_TILE
ATT_HEADS_PER_STEP = 2
SOFTMAX_ROWS = 64
TABLE_TILE = 2048
QK_PAD_DIM = MXU_DIM
VT_ROWS = V_HEAD_DIM + 16
POOL_HALO = 32

OFF_U = 0
OFF_V = OFF_U + SGU_WIDTH
OFF_GA = OFF_V + SGU_WIDTH
OFF_PIN = OFF_GA + SGU_WIDTH
OFF_PG = OFF_PIN + POOL_WIDTH
OFF_CQ = OFF_PG + POOL_WIDTH
OFF_CKV = OFF_CQ + Q_LORA_RANK
OFF_MG = OFF_CKV + KV_LORA_RANK
OFF_KR = OFF_MG + MLA_WIDTH
D_IN_EXT = OFF_KR + 2 * QK_ROPE_DIM

Q_PRESCALE = (QK_HEAD_DIM ** -0.5) * math.log2(math.e)


def _silu(x):
    return x * (1.0 / (1.0 + jnp.exp(-x)))


def _rms(x, g):
    return x * lax.rsqrt(jnp.mean(x * x, axis=-1, keepdims=True) + EPS) * g


def _dot(a, b):
    return jnp.dot(a, b, preferred_element_type=jnp.float32)


def _dot_nt(a, b):
    return lax.dot_general(a, b, (((1,), (1,)), ((), ())), preferred_element_type=jnp.float32)


def _rope_tables_kernel(pos_row_ref, invf_col_ref, cos_t_ref, sin_t_ref, cos_k_ref, sin_k_ref):
    ang_t = invf_col_ref[...] * pos_row_ref[...].astype(jnp.float32)
    cos_t = jnp.cos(ang_t)
    sin_t = jnp.sin(ang_t)
    cos_t_ref[...] = cos_t
    sin_t_ref[...] = sin_t
    pad = jnp.zeros((cos_t.shape[1], LANES - QK_ROPE_DIM), jnp.float32)
    cos_k_ref[...] = jnp.concatenate([cos_t.T, cos_t.T, pad], axis=1)
    sin_k_ref[...] = jnp.concatenate([sin_t.T, sin_t.T, pad], axis=1)


def _rope_tables(positions):
    seq = positions.shape[-1]
    inv_freq = ROPE_BASE ** (-jnp.arange(0, QK_ROPE_DIM, 2, dtype=jnp.float32) / QK_ROPE_DIM)
    invf_col = inv_freq.reshape(ROPE_HALF, 1)
    pos_row = positions.reshape(1, seq)
    n = seq // TABLE_TILE
    return pl.pallas_call(
        _rope_tables_kernel,
        grid=(n,),
        in_specs=[
            pl.BlockSpec((1, TABLE_TILE), lambda i: (0, i)),
            pl.BlockSpec((ROPE_HALF, 1), lambda i: (0, 0)),
        ],
        out_specs=[
            pl.BlockSpec((ROPE_HALF, TABLE_TILE), lambda i: (0, i)),
            pl.BlockSpec((ROPE_HALF, TABLE_TILE), lambda i: (0, i)),
            pl.BlockSpec((TABLE_TILE, LANES), lambda i: (i, 0)),
            pl.BlockSpec((TABLE_TILE, LANES), lambda i: (i, 0)),
        ],
        out_shape=[
            jax.ShapeDtypeStruct((ROPE_HALF, seq), jnp.float32),
            jax.ShapeDtypeStruct((ROPE_HALF, seq), jnp.float32),
            jax.ShapeDtypeStruct((seq, LANES), jnp.float32),
            jax.ShapeDtypeStruct((seq, LANES), jnp.float32),
        ],
        compiler_params=pltpu.CompilerParams(dimension_semantics=("arbitrary",)),
        name="rope_tables",
    )(pos_row, invf_col)


def _mixer_in_kernel(x_ref, pre_g_ref, w_in_ref, sgu_w_ref, sgu_bias_ref, ln_g_ref, ln_b_ref,
                     pool_w_ref, pool_scale_ref, qn_g_ref, w_uq_t_ref, kvn_g_ref, w_k_ref, w_v_t_ref,
                     cos_t_ref, sin_t_ref, cos_k_ref, sin_k_ref,
                     yab_ref, gate_ref, qt_ref, k_ref, vt_ref,
                     z_ref, ext_ref, a2_ref, a4_ref, a8_ref):
    i = pl.program_id(0)
    tm = x_ref.shape[0]
    hb = _rms(x_ref[...], pre_g_ref[...]).astype(jnp.bfloat16)
    z_ref[...] = _dot(hb, w_in_ref[...])

    def proj(off, width):
        return z_ref[:, off:off + width]

    v = proj(OFF_V, SGU_WIDTH)
    mu = jnp.mean(v, axis=-1, keepdims=True)
    vc = v - mu
    var = jnp.mean(vc * vc, axis=-1, keepdims=True)
    vn = vc * lax.rsqrt(var + EPS) * ln_g_ref[...] + ln_b_ref[...]
    w_rows = lax.broadcasted_iota(jnp.int32, (SGU_BLOCK, SGU_HEADS * SGU_BLOCK), 0)
    w_cols = lax.broadcasted_iota(jnp.int32, (SGU_BLOCK, SGU_HEADS * SGU_BLOCK), 1)
    w_keep = ((w_cols % SGU_BLOCK) // CHUNK) <= (w_rows // CHUNK)
    w_cat = jnp.where(w_keep, sgu_w_ref[...], 0.0).astype(jnp.bfloat16)
    head_of_col = lax.broadcasted_iota(jnp.int32, (SGU_BLOCK, SGU_WIDTH), 1) // SGU_HEAD_DIM
    n_blk = tm // SGU_BLOCK
    v_stacks = []
    for r in range(n_blk):
        vb = vn[r * SGU_BLOCK:(r + 1) * SGU_BLOCK, :]
        v_stacks.append(jnp.concatenate(
            [jnp.where(head_of_col == h, vb, 0.0) for h in range(SGU_HEADS)], axis=0).astype(jnp.bfloat16))
    mixed_wide = _dot(w_cat, jnp.concatenate(v_stacks, axis=1))
    mixed = jnp.concatenate(
        [mixed_wide[:, r * SGU_WIDTH:(r + 1) * SGU_WIDTH] + sgu_bias_ref[...] for r in range(n_blk)], axis=0)
    ya = proj(OFF_U, SGU_WIDTH) * mixed * _silu(proj(OFF_GA, SGU_WIDTH))
    yab_ref[:, 0:SGU_WIDTH] = ya.astype(yab_ref.dtype)

    p = proj(OFF_PIN, POOL_WIDTH)

    @pl.when(i == 0)
    def _():
        ext_ref[0:POOL_HALO, :] = jnp.zeros((POOL_HALO, POOL_WIDTH), jnp.float32)

    ext_ref[POOL_HALO:POOL_HALO + tm, :] = p
    end = POOL_HALO + tm
    a2_ref[8:end, :] = ext_ref[8:end, :] + ext_ref[7:end - 1, :]
    a4_ref[16:end, :] = a2_ref[16:end, :] + a2_ref[14:end - 2, :]
    a8_ref[24:end, :] = a4_ref[24:end, :] + a4_ref[20:end - 4, :]
    a16 = a8_ref[POOL_HALO:end, :] + a8_ref[POOL_HALO - 8:end - 8, :]
    group = lax.broadcasted_iota(jnp.int32, (tm, POOL_WIDTH), 1) // POOL_GROUP_DIM
    sums = jnp.where(group == 0, a2_ref[POOL_HALO:end, :],
                     jnp.where(group == 1, a4_ref[POOL_HALO:end, :],
                               jnp.where(group == 2, a8_ref[POOL_HALO:end, :], a16)))
    window = jnp.where(group == 0, POOL_WINDOWS[0],
                       jnp.where(group == 1, POOL_WINDOWS[1],
                                 jnp.where(group == 2, POOL_WINDOWS[2], POOL_WINDOWS[3])))
    t_glob = i * tm + lax.broadcasted_iota(jnp.int32, (tm, POOL_WIDTH), 0)
    count = jnp.minimum(t_glob + 1, window).astype(jnp.float32)
    pooled = sums / count - p
    ext_ref[0:POOL_HALO, :] = ext_ref[tm:tm + POOL_HALO, :]
    yb = _dot(pooled.astype(jnp.bfloat16), pool_w_ref[...]) * pool_scale_ref[...] * _silu(proj(OFF_PG, POOL_WIDTH))
    yab_ref[:, SGU_WIDTH:SGU_WIDTH + POOL_WIDTH] = yb.astype(yab_ref.dtype)

    gate_ref[...] = _silu(proj(OFF_MG, MLA_WIDTH)).astype(gate_ref.dtype)

    cqn = _rms(proj(OFF_CQ, Q_LORA_RANK), qn_g_ref[...]).astype(jnp.bfloat16)
    q_t = _dot_nt(w_uq_t_ref[...], cqn) * Q_PRESCALE
    cos_t = cos_t_ref[...]
    sin_t = sin_t_ref[...]
    for h in range(MLA_HEADS):
        base = h * QK_HEAD_DIM
        x1 = q_t[base + QK_NOPE_DIM:base + QK_NOPE_DIM + ROPE_HALF, :]
        x2 = q_t[base + QK_NOPE_DIM + ROPE_HALF:base + QK_HEAD_DIM, :]
        qt_ref[h, 0:QK_NOPE_DIM, :] = q_t[base:base + QK_NOPE_DIM, :].astype(qt_ref.dtype)
        qt_ref[h, QK_NOPE_DIM:QK_NOPE_DIM + ROPE_HALF, :] = (x1 * cos_t - x2 * sin_t).astype(qt_ref.dtype)
        qt_ref[h, QK_NOPE_DIM + ROPE_HALF:QK_HEAD_DIM, :] = (x2 * cos_t + x1 * sin_t).astype(qt_ref.dtype)
        qt_ref[h, QK_HEAD_DIM:QK_PAD_DIM, :] = jnp.zeros((QK_PAD_DIM - QK_HEAD_DIM, tm), qt_ref.dtype)

    ckvn = _rms(proj(OFF_CKV, KV_LORA_RANK), kvn_g_ref[...]).astype(jnp.bfloat16)
    k_nope = _dot(ckvn, w_k_ref[...])
    v_t = _dot_nt(w_v_t_ref[...], ckvn)
    kr = proj(OFF_KR, 2 * QK_ROPE_DIM)
    kr_swapped = pltpu.roll(kr, QK_ROPE_DIM, axis=1)
    k_pe = (kr * cos_k_ref[...] + kr_swapped * sin_k_ref[...]).astype(k_ref.dtype)
    n_sub = tm // ATT_TILE
    for h in range(MLA_HEADS):
        for c in range(n_sub):
            rows = slice(c * ATT_TILE, (c + 1) * ATT_TILE)
            k_ref[h, c, :, 0:QK_NOPE_DIM] = k_nope[rows, h * QK_NOPE_DIM:(h + 1) * QK_NOPE_DIM].astype(k_ref.dtype)
            k_ref[h, c, :, QK_NOPE_DIM:QK_PAD_DIM] = k_pe[rows, :]
            vt_ref[h, c, 0:V_HEAD_DIM, :] = v_t[h * V_HEAD_DIM:(h + 1) * V_HEAD_DIM, rows].astype(vt_ref.dtype)
            ones_row = lax.broadcasted_iota(jnp.int32, (VT_ROWS - V_HEAD_DIM, ATT_TILE), 0) == 0
            vt_ref[h, c, V_HEAD_DIM:VT_ROWS, :] = jnp.where(ones_row, 1.0, 0.0).astype(vt_ref.dtype)


def _mixer_in(x, w, layer, tables):
    seq = x.shape[0]
    tm = ROW_TILE
    n = seq // tm
    n_sub = tm // ATT_TILE
    cos_t, sin_t, cos_k, sin_k = tables

    def const(shape):
        return pl.BlockSpec((None,) + shape, lambda i: (layer,) + (0,) * len(shape))

    return pl.pallas_call(
        _mixer_in_kernel,
        grid=(n,),
        in_specs=[
            pl.BlockSpec((tm, D_MODEL), lambda i: (i, 0)),
            const((1, D_MODEL)),
            const((D_MODEL, D_IN_EXT)),
            const((SGU_BLOCK, SGU_HEADS * SGU_BLOCK)),
            const((SGU_BLOCK, SGU_WIDTH)),
            const((1, SGU_WIDTH)),
            const((1, SGU_WIDTH)),
            const((POOL_WIDTH, POOL_WIDTH)),
            const((1, POOL_WIDTH)),
            const((1, Q_LORA_RANK)),
            const((MLA_HEADS * QK_HEAD_DIM, Q_LORA_RANK)),
            const((1, KV_LORA_RANK)),
            const((KV_LORA_RANK, MLA_HEADS * QK_NOPE_DIM)),
            const((MLA_HEADS * V_HEAD_DIM, KV_LORA_RANK)),
            pl.BlockSpec((ROPE_HALF, tm), lambda i: (0, i)),
            pl.BlockSpec((ROPE_HALF, tm), lambda i: (0, i)),
            pl.BlockSpec((tm, LANES), lambda i: (i, 0)),
            pl.BlockSpec((tm, LANES), lambda i: (i, 0)),
        ],
        out_specs=[
            pl.BlockSpec((tm, SGU_WIDTH + POOL_WIDTH), lambda i: (i, 0)),
            pl.BlockSpec((tm, MLA_WIDTH), lambda i: (i, 0)),
            pl.BlockSpec((MLA_HEADS, QK_PAD_DIM, tm), lambda i: (0, 0, i)),
            pl.BlockSpec((MLA_HEADS, n_sub, ATT_TILE, QK_PAD_DIM), lambda i: (0, i, 0, 0)),
            pl.BlockSpec((MLA_HEADS, n_sub, VT_ROWS, ATT_TILE), lambda i: (0, i, 0, 0)),
        ],
        out_shape=[
            jax.ShapeDtypeStruct((seq, SGU_WIDTH + POOL_WIDTH), jnp.bfloat16),
            jax.ShapeDtypeStruct((seq, MLA_WIDTH), jnp.bfloat16),
            jax.ShapeDtypeStruct((MLA_HEADS, QK_PAD_DIM, seq), jnp.bfloat16),
            jax.ShapeDtypeStruct((MLA_HEADS, seq // ATT_TILE, ATT_TILE, QK_PAD_DIM), jnp.bfloat16),
            jax.ShapeDtypeStruct((MLA_HEADS, seq // ATT_TILE, VT_ROWS, ATT_TILE), jnp.bfloat16),
        ],
        scratch_shapes=[
            pltpu.VMEM((tm, D_IN_EXT), jnp.float32),
            pltpu.VMEM((POOL_HALO + tm, POOL_WIDTH), jnp.float32),
            pltpu.VMEM((POOL_HALO + tm, POOL_WIDTH), jnp.float32),
            pltpu.VMEM((POOL_HALO + tm, POOL_WIDTH), jnp.float32),
            pltpu.VMEM((POOL_HALO + tm, POOL_WIDTH), jnp.float32),
        ],
        compiler_params=pltpu.CompilerParams(
            dimension_semantics=("arbitrary",), vmem_limit_bytes=VMEM_LIMIT_BYTES),
        name="mixer_in",
    )(x, w["pre_g"], w["w_in"], w["sgu_w"], w["sgu_bias"], w["ln_g"], w["ln_b"],
      w["pool_w"], w["pool_scale"], w["qn_g"], w["w_uq_t"], w["kvn_g"], w["w_k"], w["w_v_t"],
      cos_t, sin_t, cos_k, sin_k)


def _attention_kernel(qt_ref, qt_next_ref, k_ref, vt_ref, o_ref, m_ref, acc_ref,
                      s0_ref, s1_ref, p0_ref, p1_ref, a0_ref, a1_ref, x0_ref, x1_ref):
    i = pl.program_id(1)
    last = pl.num_programs(1) - 1
    tk = ATT_TILE
    tq = ATT_Q_TILE
    n_heads = qt_ref.shape[0]
    n_tiles = (tq // tk) * (i + 1)
    s_refs = (s0_ref, s1_ref)
    p_refs = (p0_ref, p1_ref)
    a_refs = (a0_ref, a1_ref)
    x_refs = (x0_ref, x1_ref)
    acc_ref[...] = jnp.zeros(acc_ref.shape, jnp.float32)

    def score(slot, j, diag=None, q_ref=qt_ref):
        for h in range(n_heads):
            s = jnp.dot(k_ref[h, j], q_ref[h], preferred_element_type=jnp.float32)
            if diag is not None:
                key_chunk = (diag * tk + lax.broadcasted_iota(jnp.int32, (tk, tq), 0)) // CHUNK
                qry_chunk = lax.broadcasted_iota(jnp.int32, (tk, tq), 1) // CHUNK
                s = jnp.where(key_chunk <= qry_chunk, s, NEG_INF)
            s_refs[slot][h] = s
            x_refs[slot][h] = jnp.max(s, axis=0, keepdims=True)

    def softmax(slot, first=False):
        rc = SOFTMAX_ROWS
        for h in range(n_heads):
            m_old = jnp.full((1, tq), NEG_INF, jnp.float32) if first else m_ref[h]
            m_new = jnp.maximum(m_old, x_refs[slot][h])
            m_ref[h] = m_new
            a_refs[slot][h] = jnp.exp2(m_old - m_new)
            for r in range(0, tk, rc):
                d = s_refs[slot][h, r:r + rc, :] - m_new
                p_refs[slot][h, r:r + rc, :] = jnp.exp2(d.astype(jnp.bfloat16))

    def value(slot, j):
        for h in range(n_heads):
            pv = jnp.dot(vt_ref[h, j], p_refs[slot][h], preferred_element_type=jnp.float32)
            acc_ref[h] = a_refs[slot][h] * acc_ref[h] + pv

    def pair(t0, masked):
        for u in range(2):
            score(u, t0 + u, diag=u if masked else None)
            softmax(1 - u)
            value(u, t0 + u - 2)

    def finish():
        for h in range(n_heads):
            o_t = acc_ref[h, 0:V_HEAD_DIM, :] * (1.0 / acc_ref[h, V_HEAD_DIM:V_HEAD_DIM + 1, :])
            o_ref[:, h * V_HEAD_DIM:(h + 1) * V_HEAD_DIM] = o_t.T.astype(o_ref.dtype)

    @pl.when(i == 0)
    def _():
        score(0, 0, diag=0)
        score(1, 1, diag=1)
        softmax(0, first=True)

    @pl.when(i > 0)
    def _():
        def pair_block(jj, carry):
            pair(2 + 2 * jj, masked=False)
            return carry

        lax.fori_loop(0, i - 1, pair_block, 0)
        pair(n_tiles - 2, masked=True)

    @pl.when(i < last)
    def _():
        score(0, 0, q_ref=qt_next_ref)
        softmax(1)
        value(0, n_tiles - 2)
        score(1, 1, q_ref=qt_next_ref)
        value(1, n_tiles - 1)
        finish()
        softmax(0, first=True)

    @pl.when(i == last)
    def _():
        softmax(1)
        value(0, n_tiles - 2)
        value(1, n_tiles - 1)
        finish()


def _attention(q_t, k, v_t):
    heads, n_kv, tk, _ = k.shape
    seq = n_kv * tk
    tq = ATT_Q_TILE
    hp = ATT_HEADS_PER_STEP
    resident = dict(pipeline_mode=pl.Buffered(1))
    n_q = seq // tq
    return pl.pallas_call(
        _attention_kernel,
        grid=(heads // hp, n_q),
        in_specs=[
            pl.BlockSpec((hp, QK_PAD_DIM, tq), lambda g, i: (g, 0, i)),
            pl.BlockSpec((hp, QK_PAD_DIM, tq), lambda g, i: (g, 0, jnp.minimum(i + 1, n_q - 1))),
            pl.BlockSpec((hp, n_kv, tk, QK_PAD_DIM), lambda g, i: (g, 0, 0, 0), **resident),
            pl.BlockSpec((hp, n_kv, VT_ROWS, tk), lambda g, i: (g, 0, 0, 0), **resident),
        ],
        out_specs=pl.BlockSpec((tq, hp * V_HEAD_DIM), lambda g, i: (i, g)),
        out_shape=jax.ShapeDtypeStruct((seq, heads * V_HEAD_DIM), jnp.bfloat16),
        scratch_shapes=[
            pltpu.VMEM((hp, 1, tq), jnp.float32),
            pltpu.VMEM((hp, VT_ROWS, tq), jnp.float32),
            pltpu.VMEM((hp, tk, tq), jnp.float32),
            pltpu.VMEM((hp, tk, tq), jnp.float32),
            pltpu.VMEM((hp, tk, tq), jnp.bfloat16),
            pltpu.VMEM((hp, tk, tq), jnp.bfloat16),
            pltpu.VMEM((hp, 1, tq), jnp.float32),
            pltpu.VMEM((hp, 1, tq), jnp.float32),
            pltpu.VMEM((hp, 1, tq), jnp.float32),
            pltpu.VMEM((hp, 1, tq), jnp.float32),
        ],
        compiler_params=pltpu.CompilerParams(
            dimension_semantics=("arbitrary", "arbitrary"), vmem_limit_bytes=ATT_VMEM_LIMIT_BYTES),
        name="attention",
    )(q_t, q_t, k, v_t)


def _mixer_out_kernel(x_ref, yab_ref, o_ref, gate_ref, w_out_ref, post_g_ref, out_ref, w_bf_ref):
    @pl.when(pl.program_id(0) == 0)
    def _():
        w_bf_ref[...] = w_out_ref[...].astype(jnp.bfloat16)

    yc = (o_ref[...].astype(jnp.float32) * gate_ref[...].astype(jnp.float32)).astype(jnp.bfloat16)
    y = _dot(jnp.concatenate([yab_ref[...], yc], axis=1), w_bf_ref[...])
    out_ref[...] = x_ref[...] + _rms(y, post_g_ref[...])


def _mixer_out(x, yab, o, gate, w, layer):
    seq = x.shape[0]
    tm = OUT_ROW_TILE
    row = lambda width: pl.BlockSpec((tm, width), lambda i: (i, 0))
    return pl.pallas_call(
        _mixer_out_kernel,
        grid=(seq // tm,),
        in_specs=[
            row(D_MODEL), row(SGU_WIDTH + POOL_WIDTH), row(MLA_WIDTH), row(MLA_WIDTH),
            pl.BlockSpec((None, D_MODEL, D_MODEL), lambda i: (layer, 0, 0)),
            pl.BlockSpec((None, 1, D_MODEL), lambda i: (layer, 0, 0)),
        ],
        out_specs=row(D_MODEL),
        out_shape=jax.ShapeDtypeStruct((seq, D_MODEL), jnp.float32),
        scratch_shapes=[pltpu.VMEM((D_MODEL, D_MODEL), jnp.bfloat16)],
        compiler_params=pltpu.CompilerParams(
            dimension_semantics=("arbitrary",), vmem_limit_bytes=VMEM_LIMIT_BYTES),
        name="mixer_out",
    )(x, yab, o, gate, w["w_out"], w["post_g"])


def _prep_weights(pre_norm_g, post_norm_g, w_in, sgu_w, sgu_b, sgu_ln_g, sgu_ln_b, pool_w, pool_scale,
                  q_norm_g, w_uq, kv_norm_g, w_ukv, w_out):
    bf = jnp.bfloat16
    depth = w_in.shape[0]
    kr0 = OFF_CKV + KV_LORA_RANK
    kr1 = kr0 + QK_ROPE_DIM
    w_in_r = jnp.concatenate(
        [w_in[..., :kr0], w_in[..., kr1:], w_in[..., kr0:kr1],
         -w_in[..., kr0 + ROPE_HALF:kr1], w_in[..., kr0:kr0 + ROPE_HALF]], axis=-1).astype(bf)
    w_ukv_r = w_ukv.reshape(depth, KV_LORA_RANK, MLA_HEADS, QK_NOPE_DIM + V_HEAD_DIM)
    w_k = w_ukv_r[..., :QK_NOPE_DIM].reshape(depth, KV_LORA_RANK, MLA_HEADS * QK_NOPE_DIM)
    w_v = w_ukv_r[..., QK_NOPE_DIM:].reshape(depth, KV_LORA_RANK, MLA_HEADS * V_HEAD_DIM)
    groups = len(POOL_WINDOWS)
    same_group = jnp.eye(groups, dtype=bool)[None, :, None, :, None]
    pool_bd = jnp.where(same_group, pool_w[:, :, :, None, :], 0.0).reshape(depth, POOL_WIDTH, POOL_WIDTH)
    return {
        "pre_g": pre_norm_g.reshape(depth, 1, D_MODEL),
        "post_g": post_norm_g.reshape(depth, 1, D_MODEL),
        "w_in": w_in_r,
        "sgu_w": sgu_w.transpose(0, 2, 1, 3).reshape(depth, SGU_BLOCK, SGU_HEADS * SGU_BLOCK),
        "sgu_bias": jnp.repeat(sgu_b.transpose(0, 2, 1), SGU_HEAD_DIM, axis=2),
        "ln_g": sgu_ln_g.reshape(depth, 1, SGU_WIDTH),
        "ln_b": sgu_ln_b.reshape(depth, 1, SGU_WIDTH),
        "pool_w": pool_bd.astype(bf),
        "pool_scale": pool_scale.reshape(depth, 1, POOL_WIDTH),
        "qn_g": q_norm_g.reshape(depth, 1, Q_LORA_RANK),
        "w_uq_t": w_uq.transpose(0, 2, 1).astype(bf),
        "kvn_g": kv_norm_g.reshape(depth, 1, KV_LORA_RANK),
        "w_k": w_k.astype(bf),
        "w_v_t": w_v.transpose(0, 2, 1).astype(bf),
        "w_out": w_out,
    }


def kernel(x, positions, pre_norm_g, post_norm_g, w_in, sgu_w, sgu_b, sgu_ln_g, sgu_ln_b, pool_w, pool_scale,
           q_norm_g, w_uq, kv_norm_g, w_ukv, w_out):
    bsz, seq, d_model = x.shape
    assert bsz == 1 and seq == SEQ and d_model == D_MODEL
    assert seq % ROW_TILE == 0 and ROW_TILE % ATT_TILE == 0 and seq % TABLE_TILE == 0
    assert seq % OUT_ROW_TILE == 0 and seq % ATT_Q_TILE == 0
    tables = _rope_tables(positions)
    xs = x.reshape(seq, d_model)
    w = _prep_weights(pre_norm_g, post_norm_g, w_in, sgu_w, sgu_b, sgu_ln_g, sgu_ln_b, pool_w, pool_scale,
                      q_norm_g, w_uq, kv_norm_g, w_ukv, w_out)
    for layer in range(pre_norm_g.shape[0]):
        yab, gate, q_t, k, v_t = _mixer_in(xs, w, layer, tables)
        o = _attention(q_t, k, v_t)
        xs = _mixer_out(xs, yab, o, gate, w, layer)
    return xs.reshape(bsz, seq, d_model)
```

```python
import math

import jax
import jax.numpy as jnp
from jax import lax
from jax.experimental import pallas as pl
from jax.experimental.pallas import tpu as pltpu

D_MODEL = 1024
SEQ = 16384
CHUNK = 64
EPS = 1e-6
NEG_INF = -1e30

SGU_WIDTH = 256
SGU_HEADS = 4
SGU_HEAD_DIM = SGU_WIDTH // SGU_HEADS
SGU_BLOCK = 128

POOL_WIDTH = 256
POOL_WINDOWS = (2, 4, 8, 16)
POOL_GROUP_DIM = POOL_WIDTH // len(POOL_WINDOWS)

MLA_WIDTH = 512
MLA_HEADS = 4
V_HEAD_DIM = MLA_WIDTH // MLA_HEADS
QK_NOPE_DIM = 128
QK_ROPE_DIM = 64
QK_HEAD_DIM = QK_NOPE_DIM + QK_ROPE_DIM
Q_LORA_RANK = 384
KV_LORA_RANK = 256
ROPE_BASE = 10000.0
ROPE_HALF = QK_ROPE_DIM // 2

LANES = 128
SUBLANES = 8
MXU_DIM = 256
VMEM_LIMIT_BYTES = 48 * 1024 * 1024
ATT_VMEM_LIMIT_BYTES = 54 * 1024 * 1024

ROW_TILE = 512
OUT_ROW_TILE = 1024
ATT_TILE = 512
ATT_Q_TILE = 2 * ATT_TILE
ATT_HEADS_PER_STEP = 2
SOFTMAX_ROWS = 64
TABLE_TILE = 2048
QK_PAD_DIM = MXU_DIM
VT_ROWS = V_HEAD_DIM + 16
POOL_HALO = 32

OFF_U = 0
OFF_V = OFF_U + SGU_WIDTH
OFF_GA = OFF_V + SGU_WIDTH
OFF_PIN = OFF_GA + SGU_WIDTH
OFF_PG = OFF_PIN + POOL_WIDTH
OFF_CQ = OFF_PG + POOL_WIDTH
OFF_CKV = OFF_CQ + Q_LORA_RANK
OFF_KR = OFF_CKV + KV_LORA_RANK
OFF_MG = OFF_KR + QK_ROPE_DIM
D_IN = OFF_MG + MLA_WIDTH
D_IN_PAD = -(-D_IN // LANES) * LANES

Q_PRESCALE = (QK_HEAD_DIM ** -0.5) * math.log2(math.e)


def _silu(x):
    return x * (1.0 / (1.0 + jnp.exp(-x)))


def _rms(x, g):
    return x * lax.rsqrt(jnp.mean(x * x, axis=-1, keepdims=True) + EPS) * g


def _dot(a, b):
    return jnp.dot(a, b, preferred_element_type=jnp.float32)


def _dot_nt(a, b):
    return lax.dot_general(a, b, (((1,), (1,)), ((), ())), preferred_element_type=jnp.float32)


def _rope_tables_kernel(pos_row_ref, invf_col_ref, cos_t_ref, sin_t_ref, cos_k_ref, sin_k_ref):
    ang_t = invf_col_ref[...] * pos_row_ref[...].astype(jnp.float32)
    cos_t = jnp.cos(ang_t)
    sin_t = jnp.sin(ang_t)
    cos_t_ref[...] = cos_t
    sin_t_ref[...] = sin_t
    pad = jnp.zeros((cos_t.shape[1], LANES - QK_ROPE_DIM), jnp.float32)
    cos_k_ref[...] = jnp.concatenate([cos_t.T, cos_t.T, pad], axis=1)
    sin_k_ref[...] = jnp.concatenate([-sin_t.T, sin_t.T, pad], axis=1)


def _rope_tables(positions):
    seq = positions.shape[-1]
    inv_freq = ROPE_BASE ** (-jnp.arange(0, QK_ROPE_DIM, 2, dtype=jnp.float32) / QK_ROPE_DIM)
    invf_col = inv_freq.reshape(ROPE_HALF, 1)
    pos_row = positions.reshape(1, seq)
    n = seq // TABLE_TILE
    return pl.pallas_call(
        _rope_tables_kernel,
        grid=(n,),
        in_specs=[
            pl.BlockSpec((1, TABLE_TILE), lambda i: (0, i)),
            pl.BlockSpec((ROPE_HALF, 1), lambda i: (0, 0)),
        ],
        out_specs=[
            pl.BlockSpec((ROPE_HALF, TABLE_TILE), lambda i: (0, i)),
            pl.BlockSpec((ROPE_HALF, TABLE_TILE), lambda i: (0, i)),
            pl.BlockSpec((TABLE_TILE, LANES), lambda i: (i, 0)),
            pl.BlockSpec((TABLE_TILE, LANES), lambda i: (i, 0)),
        ],
        out_shape=[
            jax.ShapeDtypeStruct((ROPE_HALF, seq), jnp.float32),
            jax.ShapeDtypeStruct((ROPE_HALF, seq), jnp.float32),
            jax.ShapeDtypeStruct((seq, LANES), jnp.float32),
            jax.ShapeDtypeStruct((seq, LANES), jnp.float32),
        ],
        compiler_params=pltpu.CompilerParams(dimension_semantics=("arbitrary",)),
        name="rope_tables",
    )(pos_row, invf_col)


def _mixer_in_kernel(x_ref, pre_g_ref, w_in_ref, sgu_w_ref, sgu_bias_ref, ln_g_ref, ln_b_ref,
                     pool_w_ref, pool_scale_ref, qn_g_ref, w_uq_t_ref, kvn_g_ref, w_k_ref, w_v_t_ref,
                     cos_t_ref, sin_t_ref, cos_k_ref, sin_k_ref,
                     yab_ref, gate_ref, qt_ref, k_ref, vt_ref,
                     w_bf_ref, z_ref, ext_ref, a2_ref, a4_ref, a8_ref):
    i = pl.program_id(0)
    tm = x_ref.shape[0]

    def mix():
        def proj(off, width):
            return z_ref[:, off:off + width]

        v = proj(OFF_V, SGU_WIDTH)
        mu = jnp.mean(v, axis=-1, keepdims=True)
        vc = v - mu
        var = jnp.mean(vc * vc, axis=-1, keepdims=True)
        vn = vc * lax.rsqrt(var + EPS) * ln_g_ref[...] + ln_b_ref[...]
        w_rows = lax.broadcasted_iota(jnp.int32, (SGU_BLOCK, SGU_HEADS * SGU_BLOCK), 0)
        w_cols = lax.broadcasted_iota(jnp.int32, (SGU_BLOCK, SGU_HEADS * SGU_BLOCK), 1)
        w_keep = ((w_cols % SGU_BLOCK) // CHUNK) <= (w_rows // CHUNK)
        w_cat = jnp.where(w_keep, sgu_w_ref[...], 0.0).astype(jnp.bfloat16)
        head_of_col = lax.broadcasted_iota(jnp.int32, (SGU_BLOCK, SGU_WIDTH), 1) // SGU_HEAD_DIM
        n_blk = tm // SGU_BLOCK
        v_stacks = []
        for r in range(n_blk):
            vb = vn[r * SGU_BLOCK:(r + 1) * SGU_BLOCK, :]
            v_stacks.append(jnp.concatenate(
                [jnp.where(head_of_col == h, vb, 0.0) for h in range(SGU_HEADS)], axis=0).astype(jnp.bfloat16))
        mixed_wide = _dot(w_cat, jnp.concatenate(v_stacks, axis=1))
        mixed = jnp.concatenate(
            [mixed_wide[:, r * SGU_WIDTH:(r + 1) * SGU_WIDTH] + sgu_bias_ref[...] for r in range(n_blk)], axis=0)
        ya = proj(OFF_U, SGU_WIDTH) * mixed * _silu(proj(OFF_GA, SGU_WIDTH))
        yab_ref[:, 0:SGU_WIDTH] = ya.astype(yab_ref.dtype)

        p = proj(OFF_PIN, POOL_WIDTH)

        @pl.when(i == 0)
        def _():
            ext_ref[0:POOL_HALO, :] = jnp.zeros((POOL_HALO, POOL_WIDTH), jnp.float32)

        ext_ref[POOL_HALO:POOL_HALO + tm, :] = p
        end = POOL_HALO + tm
        a2_ref[8:end, :] = ext_ref[8:end, :] + ext_ref[7:end - 1, :]
        a4_ref[16:end, :] = a2_ref[16:end, :] + a2_ref[14:end - 2, :]
        a8_ref[24:end, :] = a4_ref[24:end, :] + a4_ref[20:end - 4, :]
        a16 = a8_ref[POOL_HALO:end, :] + a8_ref[POOL_HALO - 8:end - 8, :]
        group = lax.broadcasted_iota(jnp.int32, (tm, POOL_WIDTH), 1) // POOL_GROUP_DIM
        sums = jnp.where(group == 0, a2_ref[POOL_HALO:end, :],
                         jnp.where(group == 1, a4_ref[POOL_HALO:end, :],
                                   jnp.where(group == 2, a8_ref[POOL_HALO:end, :], a16)))
        window = jnp.where(group == 0, POOL_WINDOWS[0],
                           jnp.where(group == 1, POOL_WINDOWS[1],
                                     jnp.where(group == 2, POOL_WINDOWS[2], POOL_WINDOWS[3])))
        t_glob = i * tm + lax.broadcasted_iota(jnp.int32, (tm, POOL_WIDTH), 0)
        count = jnp.minimum(t_glob + 1, window).astype(jnp.float32)
        pooled = sums / count - p
        ext_ref[0:POOL_HALO, :] = ext_ref[tm:tm + POOL_HALO, :]
        pool_mixed = _dot(pooled.astype(jnp.bfloat16), pool_w_ref[...])
        yb = pool_mixed * pool_scale_ref[...] * _silu(proj(OFF_PG, POOL_WIDTH))
        yab_ref[:, SGU_WIDTH:SGU_WIDTH + POOL_WIDTH] = yb.astype(yab_ref.dtype)

        gate_ref[...] = _silu(proj(OFF_MG, MLA_WIDTH)).astype(gate_ref.dtype)

        cqn = _rms(proj(OFF_CQ, Q_LORA_RANK), qn_g_ref[...]).astype(jnp.bfloat16)
        q_t = _dot_nt(w_uq_t_ref[...], cqn) * Q_PRESCALE
        cos_t = cos_t_ref[...]
        sin_t = sin_t_ref[...]
        for h in range(MLA_HEADS):
            base = h * QK_HEAD_DIM
            x1 = q_t[base + QK_NOPE_DIM:base + QK_NOPE_DIM + ROPE_HALF, :]
            x2 = q_t[base + QK_NOPE_DIM + ROPE_HALF:base + QK_HEAD_DIM, :]
            qt_ref[h, 0:QK_NOPE_DIM, :] = q_t[base:base + QK_NOPE_DIM, :].astype(qt_ref.dtype)
            qt_ref[h, QK_NOPE_DIM:QK_NOPE_DIM + ROPE_HALF, :] = (x1 * cos_t - x2 * sin_t).astype(qt_ref.dtype)
            qt_ref[h, QK_NOPE_DIM + ROPE_HALF:QK_HEAD_DIM, :] = (x2 * cos_t + x1 * sin_t).astype(qt_ref.dtype)
            qt_ref[h, QK_HEAD_DIM:QK_PAD_DIM, :] = jnp.zeros((QK_PAD_DIM - QK_HEAD_DIM, tm), qt_ref.dtype)

        ckvn = _rms(proj(OFF_CKV, KV_LORA_RANK), kvn_g_ref[...]).astype(jnp.bfloat16)
        k_nope = _dot(ckvn, w_k_ref[...])
        v_t = _dot_nt(w_v_t_ref[...], ckvn)
        grp = proj(OFF_KR, LANES)
        lane = lax.broadcasted_iota(jnp.int32, grp.shape, 1)
        partner = jnp.where(lane < ROPE_HALF,
                            pltpu.roll(grp, LANES - ROPE_HALF, axis=1),
                            pltpu.roll(grp, ROPE_HALF, axis=1))
        roped = grp * cos_k_ref[...] + partner * sin_k_ref[...]
        k_pe = jnp.where(lane < QK_ROPE_DIM, roped, 0.0).astype(k_ref.dtype)
        n_sub = tm // ATT_TILE
        for h in range(MLA_HEADS):
            for c in range(n_sub):
                rows = slice(c * ATT_TILE, (c + 1) * ATT_TILE)
                k_ref[h, c, :, 0:QK_NOPE_DIM] = k_nope[rows, h * QK_NOPE_DIM:(h + 1) * QK_NOPE_DIM].astype(k_ref.dtype)
                k_ref[h, c, :, QK_NOPE_DIM:QK_PAD_DIM] = k_pe[rows, :]
                vt_ref[h, c, 0:V_HEAD_DIM, :] = v_t[h * V_HEAD_DIM:(h + 1) * V_HEAD_DIM, rows].astype(vt_ref.dtype)
                ones_row = lax.broadcasted_iota(jnp.int32, (VT_ROWS - V_HEAD_DIM, ATT_TILE), 0) == 0
                vt_ref[h, c, V_HEAD_DIM:VT_ROWS, :] = jnp.where(ones_row, 1.0, 0.0).astype(vt_ref.dtype)

    @pl.when(i == 0)
    def _():
        w_bf_ref[0:D_IN, :] = w_in_ref[...].astype(jnp.bfloat16)
        w_bf_ref[D_IN:D_IN_PAD, :] = jnp.zeros((D_IN_PAD - D_IN, D_MODEL), jnp.bfloat16)

    z_ref[...] = _dot_nt(_rms(x_ref[...], pre_g_ref[...]).astype(jnp.bfloat16), w_bf_ref[...])
    mix()


def _mixer_in(x, w, layer, tables):
    seq = x.shape[0]
    tm = ROW_TILE
    n = seq // tm
    n_sub = tm // ATT_TILE
    cos_t, sin_t, cos_k, sin_k = tables

    def const(shape):
        return pl.BlockSpec((None,) + shape, lambda i: (layer,) + (0,) * len(shape))

    return pl.pallas_call(
        _mixer_in_kernel,
        grid=(n,),
        in_specs=[
            pl.BlockSpec((tm, D_MODEL), lambda i: (i, 0)),
            const((1, D_MODEL)),
            pl.BlockSpec((None, D_IN, D_MODEL), lambda i: (layer, 0, 0), pipeline_mode=pl.Buffered(1)),
            const((SGU_BLOCK, SGU_HEADS * SGU_BLOCK)),
            const((SGU_BLOCK, SGU_WIDTH)),
            const((1, SGU_WIDTH)),
            const((1, SGU_WIDTH)),
            const((POOL_WIDTH, POOL_WIDTH)),
            const((1, POOL_WIDTH)),
            const((1, Q_LORA_RANK)),
            const((MLA_HEADS * QK_HEAD_DIM, Q_LORA_RANK)),
            const((1, KV_LORA_RANK)),
            const((KV_LORA_RANK, MLA_HEADS * QK_NOPE_DIM)),
            const((MLA_HEADS * V_HEAD_DIM, KV_LORA_RANK)),
            pl.BlockSpec((ROPE_HALF, tm), lambda i: (0, i)),
            pl.BlockSpec((ROPE_HALF, tm), lambda i: (0, i)),
            pl.BlockSpec((tm, LANES), lambda i: (i, 0)),
            pl.BlockSpec((tm, LANES), lambda i: (i, 0)),
        ],
        out_specs=[
            pl.BlockSpec((tm, SGU_WIDTH + POOL_WIDTH), lambda i: (i, 0)),
            pl.BlockSpec((tm, MLA_WIDTH), lambda i: (i, 0)),
            pl.BlockSpec((MLA_HEADS, QK_PAD_DIM, tm), lambda i: (0, 0, i)),
            pl.BlockSpec((MLA_HEADS, n_sub, ATT_TILE, QK_PAD_DIM), lambda i: (0, i, 0, 0)),
            pl.BlockSpec((MLA_HEADS, n_sub, VT_ROWS, ATT_TILE), lambda i: (0, i, 0, 0)),
        ],
        out_shape=[
            jax.ShapeDtypeStruct((seq, SGU_WIDTH + POOL_WIDTH), jnp.bfloat16),
            jax.ShapeDtypeStruct((seq, MLA_WIDTH), jnp.bfloat16),
            jax.ShapeDtypeStruct((MLA_HEADS, QK_PAD_DIM, seq), jnp.bfloat16),
            jax.ShapeDtypeStruct((MLA_HEADS, seq // ATT_TILE, ATT_TILE, QK_PAD_DIM), jnp.bfloat16),
            jax.ShapeDtypeStruct((MLA_HEADS, seq // ATT_TILE, VT_ROWS, ATT_TILE), jnp.bfloat16),
        ],
        scratch_shapes=[
            pltpu.VMEM((D_IN_PAD, D_MODEL), jnp.bfloat16),
            pltpu.VMEM((tm, D_IN_PAD), jnp.float32),
            pltpu.VMEM((POOL_HALO + tm, POOL_WIDTH), jnp.float32),
            pltpu.VMEM((POOL_HALO + tm, POOL_WIDTH), jnp.float32),
            pltpu.VMEM((POOL_HALO + tm, POOL_WIDTH), jnp.float32),
            pltpu.VMEM((POOL_HALO + tm, POOL_WIDTH), jnp.float32),
        ],
        compiler_params=pltpu.CompilerParams(
            dimension_semantics=("arbitrary",), vmem_limit_bytes=VMEM_LIMIT_BYTES),
        name="mixer_in",
    )(x, w["pre_g"], w["w_in"], w["sgu_w"], w["sgu_bias"], w["ln_g"], w["ln_b"],
      w["pool_w"], w["pool_scale"], w["qn_g"], w["w_uq_t"], w["kvn_g"], w["w_k"], w["w_v_t"],
      cos_t, sin_t, cos_k, sin_k)


def _attention_kernel(qt_ref, qt_next_ref, k_ref, vt_ref, o_ref, m_ref, acc_ref,
                      s0_ref, s1_ref, p0_ref, p1_ref, a0_ref, a1_ref, x0_ref, x1_ref):
    i = pl.program_id(1)
    last = pl.num_programs(1) - 1
    tk = ATT_TILE
    tq = ATT_Q_TILE
    n_heads = qt_ref.shape[0]
    n_tiles = (tq // tk) * (i + 1)
    s_refs = (s0_ref, s1_ref)
    p_refs = (p0_ref, p1_ref)
    a_refs = (a0_ref, a1_ref)
    x_refs = (x0_ref, x1_ref)
    acc_ref[...] = jnp.zeros(acc_ref.shape, jnp.float32)

    def score(slot, j, diag=None, q_ref=qt_ref):
        for h in range(n_heads):
            s = jnp.dot(k_ref[h, j], q_ref[h], preferred_element_type=jnp.float32)
            if diag is not None:
                key_chunk = (diag * tk + lax.broadcasted_iota(jnp.int32, (tk, tq), 0)) // CHUNK
                qry_chunk = lax.broadcasted_iota(jnp.int32, (tk, tq), 1) // CHUNK
                s = jnp.where(key_chunk <= qry_chunk, s, NEG_INF)
            s_refs[slot][h] = s
            x_refs[slot][h] = jnp.max(s, axis=0, keepdims=True)

    def softmax(slot, first=False):
        rc = SOFTMAX_ROWS
        for h in range(n_heads):
            m_old = jnp.full((1, tq), NEG_INF, jnp.float32) if first else m_ref[h]
            m_new = jnp.maximum(m_old, x_refs[slot][h])
            m_ref[h] = m_new
            a_refs[slot][h] = jnp.exp2(m_old - m_new)
            for r in range(0, tk, rc):
                d = s_refs[slot][h, r:r + rc, :] - m_new
                p_refs[slot][h, r:r + rc, :] = jnp.exp2(d.astype(jnp.bfloat16))

    def value(slot, j):
        for h in range(n_heads):
            pv = jnp.dot(vt_ref[h, j], p_refs[slot][h], preferred_element_type=jnp.float32)
            acc_ref[h] = a_refs[slot][h] * acc_ref[h] + pv

    def pair(t0, masked):
        for u in range(2):
            score(u, t0 + u, diag=u if masked else None)
            softmax(1 - u)
            value(u, t0 + u - 2)

    def finish():
        for h in range(n_heads):
            o_t = acc_ref[h, 0:V_HEAD_DIM, :] * (1.0 / acc_ref[h, V_HEAD_DIM:V_HEAD_DIM + 1, :])
            o_ref[:, h * V_HEAD_DIM:(h + 1) * V_HEAD_DIM] = o_t.T.astype(o_ref.dtype)

    @pl.when(i == 0)
    def _():
        score(0, 0, diag=0)
        score(1, 1, diag=1)
        softmax(0, first=True)

    @pl.when(i > 0)
    def _():
        def pair_block(jj, carry):
            pair(2 + 2 * jj, masked=False)
            return carry

        lax.fori_loop(0, i - 1, pair_block, 0)
        pair(n_tiles - 2, masked=True)

    @pl.when(i < last)
    def _():
        score(0, 0, q_ref=qt_next_ref)
        softmax(1)
        value(0, n_tiles - 2)
        score(1, 1, q_ref=qt_next_ref)
        value(1, n_tiles - 1)
        finish()
        softmax(0, first=True)

    @pl.when(i == last)
    def _():
        softmax(1)
        value(0, n_tiles - 2)
        value(1, n_tiles - 1)
        finish()


def _attention(q_t, k, v_t):
    heads, n_kv, tk, _ = k.shape
    seq = n_kv * tk
    tq = ATT_Q_TILE
    hp = ATT_HEADS_PER_STEP
    resident = dict(pipeline_mode=pl.Buffered(1))
    n_q = seq // tq
    return pl.pallas_call(
        _attention_kernel,
        grid=(heads // hp, n_q),
        in_specs=[
            pl.BlockSpec((hp, QK_PAD_DIM, tq), lambda g, i: (g, 0, i)),
            pl.BlockSpec((hp, QK_PAD_DIM, tq), lambda g, i: (g, 0, jnp.minimum(i + 1, n_q - 1))),
            pl.BlockSpec((hp, n_kv, tk, QK_PAD_DIM), lambda g, i: (g, 0, 0, 0), **resident),
            pl.BlockSpec((hp, n_kv, VT_ROWS, tk), lambda g, i: (g, 0, 0, 0), **resident),
        ],
        out_specs=pl.BlockSpec((tq, hp * V_HEAD_DIM), lambda g, i: (i, g)),
        out_shape=jax.ShapeDtypeStruct((seq, heads * V_HEAD_DIM), jnp.bfloat16),
        scratch_shapes=[
            pltpu.VMEM((hp, 1, tq), jnp.float32),
            pltpu.VMEM((hp, VT_ROWS, tq), jnp.float32),
            pltpu.VMEM((hp, tk, tq), jnp.float32),
            pltpu.VMEM((hp, tk, tq), jnp.float32),
            pltpu.VMEM((hp, tk, tq), jnp.bfloat16),
            pltpu.VMEM((hp, tk, tq), jnp.bfloat16),
            pltpu.VMEM((hp, 1, tq), jnp.float32),
            pltpu.VMEM((hp, 1, tq), jnp.float32),
            pltpu.VMEM((hp, 1, tq), jnp.float32),
            pltpu.VMEM((hp, 1, tq), jnp.float32),
        ],
        compiler_params=pltpu.CompilerParams(
            dimension_semantics=("arbitrary", "arbitrary"), vmem_limit_bytes=ATT_VMEM_LIMIT_BYTES),
        name="attention",
    )(q_t, q_t, k, v_t)


def _mixer_out_kernel(x_ref, yab_ref, o_ref, gate_ref, w_out_ref, post_g_ref, out_ref, w_bf_ref):
    @pl.when(pl.program_id(0) == 0)
    def _():
        w_bf_ref[...] = w_out_ref[...].astype(jnp.bfloat16)

    yc = (o_ref[...].astype(jnp.float32) * gate_ref[...].astype(jnp.float32)).astype(jnp.bfloat16)
    y = _dot(jnp.concatenate([yab_ref[...], yc], axis=1), w_bf_ref[...])
    out_ref[...] = x_ref[...] + _rms(y, post_g_ref[...])


def _mixer_out(x, yab, o, gate, w, layer):
    seq = x.shape[0]
    tm = OUT_ROW_TILE
    row = lambda width: pl.BlockSpec((tm, width), lambda i: (i, 0))
    return pl.pallas_call(
        _mixer_out_kernel,
        grid=(seq // tm,),
        in_specs=[
            row(D_MODEL), row(SGU_WIDTH + POOL_WIDTH), row(MLA_WIDTH), row(MLA_WIDTH),
            pl.BlockSpec((None, D_MODEL, D_MODEL), lambda i: (layer, 0, 0)),
            pl.BlockSpec((None, 1, D_MODEL), lambda i: (layer, 0, 0)),
        ],
        out_specs=row(D_MODEL),
        out_shape=jax.ShapeDtypeStruct((seq, D_MODEL), jnp.float32),
        scratch_shapes=[pltpu.VMEM((D_MODEL, D_MODEL), jnp.bfloat16)],
        compiler_params=pltpu.CompilerParams(
            dimension_semantics=("arbitrary",), vmem_limit_bytes=VMEM_LIMIT_BYTES),
        name="mixer_out",
    )(x, yab, o, gate, w["w_out"], w["post_g"])


def _prep_weights(pre_norm_g, post_norm_g, w_in, sgu_w, sgu_b, sgu_ln_g, sgu_ln_b, pool_w, pool_scale,
                  q_norm_g, w_uq, kv_norm_g, w_ukv, w_out):
    bf = jnp.bfloat16
    depth = w_in.shape[0]
    w_ukv_r = w_ukv.reshape(depth, KV_LORA_RANK, MLA_HEADS, QK_NOPE_DIM + V_HEAD_DIM)
    w_k = w_ukv_r[..., :QK_NOPE_DIM].reshape(depth, KV_LORA_RANK, MLA_HEADS * QK_NOPE_DIM)
    w_v = w_ukv_r[..., QK_NOPE_DIM:].reshape(depth, KV_LORA_RANK, MLA_HEADS * V_HEAD_DIM)
    groups = len(POOL_WINDOWS)
    same_group = jnp.eye(groups, dtype=bool)[None, :, None, :, None]
    pool_bd = jnp.where(same_group, pool_w[:, :, :, None, :], 0.0).reshape(depth, POOL_WIDTH, POOL_WIDTH)
    return {
        "pre_g": pre_norm_g.reshape(depth, 1, D_MODEL),
        "post_g": post_norm_g.reshape(depth, 1, D_MODEL),
        "w_in": w_in.transpose(0, 2, 1),
        "sgu_w": sgu_w.transpose(0, 2, 1, 3).reshape(depth, SGU_BLOCK, SGU_HEADS * SGU_BLOCK),
        "sgu_bias": jnp.repeat(sgu_b.transpose(0, 2, 1), SGU_HEAD_DIM, axis=2),
        "ln_g": sgu_ln_g.reshape(depth, 1, SGU_WIDTH),
        "ln_b": sgu_ln_b.reshape(depth, 1, SGU_WIDTH),
        "pool_w": pool_bd.astype(bf),
        "pool_scale": pool_scale.reshape(depth, 1, POOL_WIDTH),
        "qn_g": q_norm_g.reshape(depth, 1, Q_LORA_RANK),
        "w_uq_t": w_uq.transpose(0, 2, 1).astype(bf),
        "kvn_g": kv_norm_g.reshape(depth, 1, KV_LORA_RANK),
        "w_k": w_k.astype(bf),
        "w_v_t": w_v.transpose(0, 2, 1).astype(bf),
        "w_out": w_out,
    }


def kernel(x, positions, pre_norm_g, post_norm_g, w_in, sgu_w, sgu_b, sgu_ln_g, sgu_ln_b, pool_w, pool_scale,
           q_norm_g, w_uq, kv_norm_g, w_ukv, w_out):
    bsz, seq, d_model = x.shape
    assert bsz == 1 and seq == SEQ and d_model == D_MODEL
    assert seq % ROW_TILE == 0 and ROW_TILE % ATT_TILE == 0 and seq % TABLE_TILE == 0
    assert seq % OUT_ROW_TILE == 0 and seq % ATT_Q_TILE == 0
    tables = _rope_tables(positions)
    xs = x.reshape(seq, d_model)
    w = _prep_weights(pre_norm_g, post_norm_g, w_in, sgu_w, sgu_b, sgu_ln_g, sgu_ln_b, pool_w, pool_scale,
                      q_norm_g, w_uq, kv_norm_g, w_ukv, w_out)
    for layer in range(pre_norm_g.shape[0]):
        yab, gate, q_t, k, v_t = _mixer_in(xs, w, layer, tables)
        o = _attention(q_t, k, v_t)
        xs = _mixer_out(xs, yab, o, gate, w, layer)
    return xs.reshape(bsz, seq, d_model)
```

```python
import math

import jax
import jax.numpy as jnp
from jax import lax
from jax.experimental import pallas as pl
from jax.experimental.pallas import tpu as pltpu

D_MODEL = 1024
SEQ = 16384
CHUNK = 64
EPS = 1e-6
NEG_INF = -1e30

SGU_WIDTH = 256
SGU_HEADS = 4
SGU_HEAD_DIM = SGU_WIDTH // SGU_HEADS
SGU_BLOCK = 128

POOL_WIDTH = 256
POOL_WINDOWS = (2, 4, 8, 16)
POOL_GROUP_DIM = POOL_WIDTH // len(POOL_WINDOWS)

MLA_WIDTH = 512
MLA_HEADS = 4
V_HEAD_DIM = MLA_WIDTH // MLA_HEADS
QK_NOPE_DIM = 128
QK_ROPE_DIM = 64
QK_HEAD_DIM = QK_NOPE_DIM + QK_ROPE_DIM
Q_LORA_RANK = 384
KV_LORA_RANK = 256
ROPE_BASE = 10000.0
ROPE_HALF = QK_ROPE_DIM // 2

LANES = 128
SUBLANES = 8
MXU_DIM = 256
VMEM_LIMIT_BYTES = 48 * 1024 * 1024
ATT_VMEM_LIMIT_BYTES = 54 * 1024 * 1024

ROW_TILE = 512
OUT_ROW_TILE = 1024
ATT_TILE = 512
ATT_Q_TILE = 2 * ATT_TILE
ATT_HEADS_PER_STEP = 2
SOFTMAX_ROWS = 64
TABLE_TILE = 2048
QK_PAD_DIM = MXU_DIM
VT_ROWS = V_HEAD_DIM + 16
POOL_HALO = 32

OFF_U = 0
OFF_V = OFF_U + SGU_WIDTH
OFF_GA = OFF_V + SGU_WIDTH
OFF_PIN = OFF_GA + SGU_WIDTH
OFF_PG = OFF_PIN + POOL_WIDTH
OFF_CQ = OFF_PG + POOL_WIDTH
OFF_CKV = OFF_CQ + Q_LORA_RANK
OFF_KR = OFF_CKV + KV_LORA_RANK
OFF_MG = OFF_KR + QK_ROPE_DIM
D_IN = OFF_MG + MLA_WIDTH
D_IN_PAD = -(-D_IN // LANES) * LANES

Q_PRESCALE = (QK_HEAD_DIM ** -0.5) * math.log2(math.e)


def _silu(x):
    return x * (1.0 / (1.0 + jnp.exp(-x)))


def _rms(x, g):
    return x * lax.rsqrt(jnp.mean(x * x, axis=-1, keepdims=True) + EPS) * g


def _dot(a, b):
    return jnp.dot(a, b, preferred_element_type=jnp.float32)


def _dot_nt(a, b):
    return lax.dot_general(a, b, (((1,), (1,)), ((), ())), preferred_element_type=jnp.float32)


def _rope_tables_kernel(pos_row_ref, invf_col_ref, cos_t_ref, sin_t_ref, cos_k_ref, sin_k_ref):
    ang_t = invf_col_ref[...] * pos_row_ref[...].astype(jnp.float32)
    cos_t = jnp.cos(ang_t)
    sin_t = jnp.sin(ang_t)
    cos_t_ref[...] = cos_t
    sin_t_ref[...] = sin_t
    pad = jnp.zeros((cos_t.shape[1], LANES - QK_ROPE_DIM), jnp.float32)
    cos_k_ref[...] = jnp.concatenate([cos_t.T, cos_t.T, pad], axis=1)
    sin_k_ref[...] = jnp.concatenate([-sin_t.T, sin_t.T, pad], axis=1)


def _rope_tables(positions):
    seq = positions.shape[-1]
    inv_freq = ROPE_BASE ** (-jnp.arange(0, QK_ROPE_DIM, 2, dtype=jnp.float32) / QK_ROPE_DIM)
    invf_col = inv_freq.reshape(ROPE_HALF, 1)
    pos_row = positions.reshape(1, seq)
    n = seq // TABLE_TILE
    return pl.pallas_call(
        _rope_tables_kernel,
        grid=(n,),
        in_specs=[
            pl.BlockSpec((1, TABLE_TILE), lambda i: (0, i)),
            pl.BlockSpec((ROPE_HALF, 1), lambda i: (0, 0)),
        ],
        out_specs=[
            pl.BlockSpec((ROPE_HALF, TABLE_TILE), lambda i: (0, i)),
            pl.BlockSpec((ROPE_HALF, TABLE_TILE), lambda i: (0, i)),
            pl.BlockSpec((TABLE_TILE, LANES), lambda i: (i, 0)),
            pl.BlockSpec((TABLE_TILE, LANES), lambda i: (i, 0)),
        ],
        out_shape=[
            jax.ShapeDtypeStruct((ROPE_HALF, seq), jnp.float32),
            jax.ShapeDtypeStruct((ROPE_HALF, seq), jnp.float32),
            jax.ShapeDtypeStruct((seq, LANES), jnp.float32),
            jax.ShapeDtypeStruct((seq, LANES), jnp.float32),
        ],
        compiler_params=pltpu.CompilerParams(dimension_semantics=("arbitrary",)),
        name="rope_tables",
    )(pos_row, invf_col)


def _mixer_in_kernel(x_ref, pre_g_ref, w_in_ref, sgu_w_ref, sgu_bias_ref, ln_g_ref, ln_b_ref,
                     pool_w_ref, pool_scale_ref, qn_g_ref, w_uq_t_ref, kvn_g_ref, w_k_ref, w_v_t_ref,
                     cos_t_ref, sin_t_ref, cos_k_ref, sin_k_ref,
                     yab_ref, gate_ref, qt_ref, k_ref, vt_ref,
                     w_bf_ref, z_ref, ext_ref, a2_ref, a4_ref, a8_ref):
    i = pl.program_id(0)
    tm = x_ref.shape[0]

    def mix():
        def proj(off, width):
            return z_ref[:, off:off + width]

        v = proj(OFF_V, SGU_WIDTH)
        mu = jnp.mean(v, axis=-1, keepdims=True)
        vc = v - mu
        var = jnp.mean(vc * vc, axis=-1, keepdims=True)
        vn = vc * lax.rsqrt(var + EPS) * ln_g_ref[...] + ln_b_ref[...]
        w_rows = lax.broadcasted_iota(jnp.int32, (SGU_BLOCK, SGU_HEADS * SGU_BLOCK), 0)
        w_cols = lax.broadcasted_iota(jnp.int32, (SGU_BLOCK, SGU_HEADS * SGU_BLOCK), 1)
        w_keep = ((w_cols % SGU_BLOCK) // CHUNK) <= (w_rows // CHUNK)
        w_cat = jnp.where(w_keep, sgu_w_ref[...], 0.0).astype(jnp.bfloat16)
        head_of_col = lax.broadcasted_iota(jnp.int32, (SGU_BLOCK, SGU_WIDTH), 1) // SGU_HEAD_DIM
        n_blk = tm // SGU_BLOCK
        v_stacks = []
        for r in range(n_blk):
            vb = vn[r * SGU_BLOCK:(r + 1) * SGU_BLOCK, :]
            v_stacks.append(jnp.concatenate(
                [jnp.where(head_of_col == h, vb, 0.0) for h in range(SGU_HEADS)], axis=0).astype(jnp.bfloat16))
        mixed_wide = _dot(w_cat, jnp.concatenate(v_stacks, axis=1))
        mixed = jnp.concatenate(
            [mixed_wide[:, r * SGU_WIDTH:(r + 1) * SGU_WIDTH] + sgu_bias_ref[...] for r in range(n_blk)], axis=0)
        ya = proj(OFF_U, SGU_WIDTH) * mixed * _silu(proj(OFF_GA, SGU_WIDTH))
        yab_ref[:, 0:SGU_WIDTH] = ya.astype(yab_ref.dtype)

        p = proj(OFF_PIN, POOL_WIDTH)

        @pl.when(i == 0)
        def _():
            ext_ref[0:POOL_HALO, :] = jnp.zeros((POOL_HALO, POOL_WIDTH), jnp.float32)

        ext_ref[POOL_HALO:POOL_HALO + tm, :] = p
        end = POOL_HALO + tm
        a2_ref[8:end, :] = ext_ref[8:end, :] + ext_ref[7:end - 1, :]
        a4_ref[16:end, :] = a2_ref[16:end, :] + a2_ref[14:end - 2, :]
        a8_ref[24:end, :] = a4_ref[24:end, :] + a4_ref[20:end - 4, :]
        a16 = a8_ref[POOL_HALO:end, :] + a8_ref[POOL_HALO - 8:end - 8, :]
        group = lax.broadcasted_iota(jnp.int32, (tm, POOL_WIDTH), 1) // POOL_GROUP_DIM
        sums = jnp.where(group == 0, a2_ref[POOL_HALO:end, :],
                         jnp.where(group == 1, a4_ref[POOL_HALO:end, :],
                                   jnp.where(group == 2, a8_ref[POOL_HALO:end, :], a16)))
        window = jnp.where(group == 0, POOL_WINDOWS[0],
                           jnp.where(group == 1, POOL_WINDOWS[1],
                                     jnp.where(group == 2, POOL_WINDOWS[2], POOL_WINDOWS[3])))
        t_glob = i * tm + lax.broadcasted_iota(jnp.int32, (tm, POOL_WIDTH), 0)
        count = jnp.minimum(t_glob + 1, window).astype(jnp.float32)
        pooled = sums / count - p
        ext_ref[0:POOL_HALO, :] = ext_ref[tm:tm + POOL_HALO, :]
        pool_mixed = _dot(pooled.astype(jnp.bfloat16), pool_w_ref[...])
        yb = pool_mixed * pool_scale_ref[...] * _silu(proj(OFF_PG, POOL_WIDTH))
        yab_ref[:, SGU_WIDTH:SGU_WIDTH + POOL_WIDTH] = yb.astype(yab_ref.dtype)

        gate_ref[...] = _silu(proj(OFF_MG, MLA_WIDTH)).astype(gate_ref.dtype)

        cqn = _rms(proj(OFF_CQ, Q_LORA_RANK), qn_g_ref[...]).astype(jnp.bfloat16)
        q_t = _dot_nt(w_uq_t_ref[...], cqn) * Q_PRESCALE
        cos_t = cos_t_ref[...]
        sin_t = sin_t_ref[...]
        for h in range(MLA_HEADS):
            base = h * QK_HEAD_DIM
            x1 = q_t[base + QK_NOPE_DIM:base + QK_NOPE_DIM + ROPE_HALF, :]
            x2 = q_t[base + QK_NOPE_DIM + ROPE_HALF:base + QK_HEAD_DIM, :]
            qt_ref[h, 0:QK_NOPE_DIM, :] = q_t[base:base + QK_NOPE_DIM, :].astype(qt_ref.dtype)
            qt_ref[h, QK_NOPE_DIM:QK_NOPE_DIM + ROPE_HALF, :] = (x1 * cos_t - x2 * sin_t).astype(qt_ref.dtype)
            qt_ref[h, QK_NOPE_DIM + ROPE_HALF:QK_HEAD_DIM, :] = (x2 * cos_t + x1 * sin_t).astype(qt_ref.dtype)
            qt_ref[h, QK_HEAD_DIM:QK_PAD_DIM, :] = jnp.zeros((QK_PAD_DIM - QK_HEAD_DIM, tm), qt_ref.dtype)

        ckvn = _rms(proj(OFF_CKV, KV_LORA_RANK), kvn_g_ref[...]).astype(jnp.bfloat16)
        k_nope = _dot(ckvn, w_k_ref[...])
        v_t = _dot_nt(w_v_t_ref[...], ckvn)
        grp = proj(OFF_KR, LANES)
        lane = lax.broadcasted_iota(jnp.int32, grp.shape, 1)
        partner = jnp.where(lane < ROPE_HALF,
                            pltpu.roll(grp, LANES - ROPE_HALF, axis=1),
                            pltpu.roll(grp, ROPE_HALF, axis=1))
        roped = grp * cos_k_ref[...] + partner * sin_k_ref[...]
        k_pe = jnp.where(lane < QK_ROPE_DIM, roped, 0.0).astype(k_ref.dtype)
        n_sub = tm // ATT_TILE
        for h in range(MLA_HEADS):
            for c in range(n_sub):
                rows = slice(c * ATT_TILE, (c + 1) * ATT_TILE)
                k_ref[h, c, :, 0:QK_NOPE_DIM] = k_nope[rows, h * QK_NOPE_DIM:(h + 1) * QK_NOPE_DIM].astype(k_ref.dtype)
                k_ref[h, c, :, QK_NOPE_DIM:QK_PAD_DIM] = k_pe[rows, :]
                vt_ref[h, c, 0:V_HEAD_DIM, :] = v_t[h * V_HEAD_DIM:(h + 1) * V_HEAD_DIM, rows].astype(vt_ref.dtype)
                ones_row = lax.broadcasted_iota(jnp.int32, (VT_ROWS - V_HEAD_DIM, ATT_TILE), 0) == 0
                vt_ref[h, c, V_HEAD_DIM:VT_ROWS, :] = jnp.where(ones_row, 1.0, 0.0).astype(vt_ref.dtype)

    @pl.when(i == 0)
    def _():
        w_bf_ref[0:D_IN, :] = w_in_ref[...].astype(jnp.bfloat16)
        w_bf_ref[D_IN:D_IN_PAD, :] = jnp.zeros((D_IN_PAD - D_IN, D_MODEL), jnp.bfloat16)

    z_ref[...] = _dot_nt(_rms(x_ref[...], pre_g_ref[...]).astype(jnp.bfloat16), w_bf_ref[...])
    mix()


def _mixer_in(x, w, layer, tables):
    seq = x.shape[0]
    tm = ROW_TILE
    n = seq // tm
    n_sub = tm // ATT_TILE
    cos_t, sin_t, cos_k, sin_k = tables

    def const(shape):
        return pl.BlockSpec((None,) + shape, lambda i: (layer,) + (0,) * len(shape))

    return pl.pallas_call(
        _mixer_in_kernel,
        grid=(n,),
        in_specs=[
            pl.BlockSpec((tm, D_MODEL), lambda i: (i, 0)),
            const((1, D_MODEL)),
            pl.BlockSpec((None, D_IN, D_MODEL), lambda i: (layer, 0, 0), pipeline_mode=pl.Buffered(1)),
            const((SGU_BLOCK, SGU_HEADS * SGU_BLOCK)),
            const((SGU_BLOCK, SGU_WIDTH)),
            const((1, SGU_WIDTH)),
            const((1, SGU_WIDTH)),
            const((POOL_WIDTH, POOL_WIDTH)),
            const((1, POOL_WIDTH)),
            const((1, Q_LORA_RANK)),
            const((MLA_HEADS * QK_HEAD_DIM, Q_LORA_RANK)),
            const((1, KV_LORA_RANK)),
            const((KV_LORA_RANK, MLA_HEADS * QK_NOPE_DIM)),
            const((MLA_HEADS * V_HEAD_DIM, KV_LORA_RANK)),
            pl.BlockSpec((ROPE_HALF, tm), lambda i: (0, i)),
            pl.BlockSpec((ROPE_HALF, tm), lambda i: (0, i)),
            pl.BlockSpec((tm, LANES), lambda i: (i, 0)),
            pl.BlockSpec((tm, LANES), lambda i: (i, 0)),
        ],
        out_specs=[
            pl.BlockSpec((tm, SGU_WIDTH + POOL_WIDTH), lambda i: (i, 0)),
            pl.BlockSpec((tm, MLA_WIDTH), lambda i: (i, 0)),
            pl.BlockSpec((MLA_HEADS, QK_PAD_DIM, tm), lambda i: (0, 0, i)),
            pl.BlockSpec((MLA_HEADS, n_sub, ATT_TILE, QK_PAD_DIM), lambda i: (0, i, 0, 0)),
            pl.BlockSpec((MLA_HEADS, n_sub, VT_ROWS, ATT_TILE), lambda i: (0, i, 0, 0)),
        ],
        out_shape=[
            jax.ShapeDtypeStruct((seq, SGU_WIDTH + POOL_WIDTH), jnp.bfloat16),
            jax.ShapeDtypeStruct((seq, MLA_WIDTH), jnp.bfloat16),
            jax.ShapeDtypeStruct((MLA_HEADS, QK_PAD_DIM, seq), jnp.bfloat16),
            jax.ShapeDtypeStruct((MLA_HEADS, seq // ATT_TILE, ATT_TILE, QK_PAD_DIM), jnp.bfloat16),
            jax.ShapeDtypeStruct((MLA_HEADS, seq // ATT_TILE, VT_ROWS, ATT_TILE), jnp.bfloat16),
        ],
        scratch_shapes=[
            pltpu.VMEM((D_IN_PAD, D_MODEL), jnp.bfloat16),
            pltpu.VMEM((tm, D_IN_PAD), jnp.float32),
            pltpu.VMEM((POOL_HALO + tm, POOL_WIDTH), jnp.float32),
            pltpu.VMEM((POOL_HALO + tm, POOL_WIDTH), jnp.float32),
            pltpu.VMEM((POOL_HALO + tm, POOL_WIDTH), jnp.float32),
            pltpu.VMEM((POOL_HALO + tm, POOL_WIDTH), jnp.float32),
        ],
        compiler_params=pltpu.CompilerParams(
            dimension_semantics=("arbitrary",), vmem_limit_bytes=VMEM_LIMIT_BYTES),
        name="mixer_in",
    )(x, w["pre_g"], w["w_in"], w["sgu_w"], w["sgu_bias"], w["ln_g"], w["ln_b"],
      w["pool_w"], w["pool_scale"], w["qn_g"], w["w_uq_t"], w["kvn_g"], w["w_k"], w["w_v_t"],
      cos_t, sin_t, cos_k, sin_k)


def _attention_kernel(qt_ref, qt_next_ref, k_ref, vt_ref, o_ref, m_ref, acc_ref,
                      s0_ref, s1_ref, p0_ref, p1_ref, a0_ref, a1_ref, x0_ref, x1_ref):
    i = pl.program_id(1)
    last = pl.num_programs(1) - 1
    tk = ATT_TILE
    tq = ATT_Q_TILE
    n_heads = qt_ref.shape[0]
    n_tiles = (tq // tk) * (i + 1)
    s_refs = (s0_ref, s1_ref)
    p_refs = (p0_ref, p1_ref)
    a_refs = (a0_ref, a1_ref)
    x_refs = (x0_ref, x1_ref)
    acc_ref[...] = jnp.zeros(acc_ref.shape, jnp.float32)

    LATE = slice(tq // 2, tq)
    ALL = slice(0, tq)

    def score(slot, j, diag=None, q_ref=qt_ref):
        qs = LATE if diag == 1 else ALL
        for h in range(n_heads):
            s = jnp.dot(k_ref[h, j], q_ref[h, :, qs], preferred_element_type=jnp.float32)
            if diag is not None:
                key_chunk = (diag * tk + lax.broadcasted_iota(jnp.int32, s.shape, 0)) // CHUNK
                qry_chunk = (qs.start + lax.broadcasted_iota(jnp.int32, s.shape, 1)) // CHUNK
                s = jnp.where(key_chunk <= qry_chunk, s, NEG_INF)
            s_refs[slot][h, :, qs] = s
            x_refs[slot][h, :, qs] = jnp.max(s, axis=0, keepdims=True)

    def softmax(slot, first=False, qs=ALL):
        rc = SOFTMAX_ROWS
        for h in range(n_heads):
            m_old = jnp.full((1, tq), NEG_INF, jnp.float32) if first else m_ref[h, :, qs]
            m_new = jnp.maximum(m_old, x_refs[slot][h, :, qs])
            m_ref[h, :, qs] = m_new
            a_refs[slot][h, :, qs] = jnp.exp2(m_old - m_new)
            for r in range(0, tk, rc):
                d = s_refs[slot][h, r:r + rc, qs] - m_new
                p_refs[slot][h, r:r + rc, qs] = jnp.exp2(d.astype(jnp.bfloat16))

    def value(slot, j, qs=ALL):
        for h in range(n_heads):
            pv = jnp.dot(vt_ref[h, j], p_refs[slot][h, :, qs], preferred_element_type=jnp.float32)
            acc_ref[h, :, qs] = a_refs[slot][h, :, qs] * acc_ref[h, :, qs] + pv

    def pair(t0, masked):
        for u in range(2):
            score(u, t0 + u, diag=u if masked else None)
            softmax(1 - u)
            value(u, t0 + u - 2)

    def finish():
        for h in range(n_heads):
            o_t = acc_ref[h, 0:V_HEAD_DIM, :] * (1.0 / acc_ref[h, V_HEAD_DIM:V_HEAD_DIM + 1, :])
            o_ref[:, h * V_HEAD_DIM:(h + 1) * V_HEAD_DIM] = o_t.T.astype(o_ref.dtype)

    @pl.when(i == 0)
    def _():
        score(0, 0, diag=0)
        score(1, 1, diag=1)
        softmax(0, first=True)

    @pl.when(i > 0)
    def _():
        def pair_block(jj, carry):
            pair(2 + 2 * jj, masked=False)
            return carry

        lax.fori_loop(0, i - 1, pair_block, 0)
        pair(n_tiles - 2, masked=True)

    @pl.when(i < last)
    def _():
        score(0, 0, q_ref=qt_next_ref)
        softmax(1, qs=LATE)
        value(0, n_tiles - 2)
        score(1, 1, q_ref=qt_next_ref)
        value(1, n_tiles - 1, qs=LATE)
        finish()
        softmax(0, first=True)

    @pl.when(i == last)
    def _():
        softmax(1, qs=LATE)
        value(0, n_tiles - 2)
        value(1, n_tiles - 1, qs=LATE)
        finish()


def _attention(q_t, k, v_t):
    heads, n_kv, tk, _ = k.shape
    seq = n_kv * tk
    tq = ATT_Q_TILE
    hp = ATT_HEADS_PER_STEP
    resident = dict(pipeline_mode=pl.Buffered(1))
    n_q = seq // tq
    return pl.pallas_call(
        _attention_kernel,
        grid=(heads // hp, n_q),
        in_specs=[
            pl.BlockSpec((hp, QK_PAD_DIM, tq), lambda g, i: (g, 0, i)),
            pl.BlockSpec((hp, QK_PAD_DIM, tq), lambda g, i: (g, 0, jnp.minimum(i + 1, n_q - 1))),
            pl.BlockSpec((hp, n_kv, tk, QK_PAD_DIM), lambda g, i: (g, 0, 0, 0), **resident),
            pl.BlockSpec((hp, n_kv, VT_ROWS, tk), lambda g, i: (g, 0, 0, 0), **resident),
        ],
        out_specs=pl.BlockSpec((tq, hp * V_HEAD_DIM), lambda g, i: (i, g)),
        out_shape=jax.ShapeDtypeStruct((seq, heads * V_HEAD_DIM), jnp.bfloat16),
        scratch_shapes=[
            pltpu.VMEM((hp, 1, tq), jnp.float32),
            pltpu.VMEM((hp, VT_ROWS, tq), jnp.float32),
            pltpu.VMEM((hp, tk, tq), jnp.float32),
            pltpu.VMEM((hp, tk, tq), jnp.float32),
            pltpu.VMEM((hp, tk, tq), jnp.bfloat16),
            pltpu.VMEM((hp, tk, tq), jnp.bfloat16),
            pltpu.VMEM((hp, 1, tq), jnp.float32),
            pltpu.VMEM((hp, 1, tq), jnp.float32),
            pltpu.VMEM((hp, 1, tq), jnp.float32),
            pltpu.VMEM((hp, 1, tq), jnp.float32),
        ],
        compiler_params=pltpu.CompilerParams(
            dimension_semantics=("arbitrary", "arbitrary"), vmem_limit_bytes=ATT_VMEM_LIMIT_BYTES),
        name="attention",
    )(q_t, q_t, k, v_t)


def _mixer_out_kernel(x_ref, yab_ref, o_ref, gate_ref, w_out_ref, post_g_ref, out_ref, w_bf_ref):
    @pl.when(pl.program_id(0) == 0)
    def _():
        w_bf_ref[...] = w_out_ref[...].astype(jnp.bfloat16)

    yc = (o_ref[...].astype(jnp.float32) * gate_ref[...].astype(jnp.float32)).astype(jnp.bfloat16)
    y = _dot(jnp.concatenate([yab_ref[...], yc], axis=1), w_bf_ref[...])
    out_ref[...] = x_ref[...] + _rms(y, post_g_ref[...])


def _mixer_out(x, yab, o, gate, w, layer):
    seq = x.shape[0]
    tm = OUT_ROW_TILE
    row = lambda width: pl.BlockSpec((tm, width), lambda i: (i, 0))
    return pl.pallas_call(
        _mixer_out_kernel,
        grid=(seq // tm,),
        in_specs=[
            row(D_MODEL), row(SGU_WIDTH + POOL_WIDTH), row(MLA_WIDTH), row(MLA_WIDTH),
            pl.BlockSpec((None, D_MODEL, D_MODEL), lambda i: (layer, 0, 0)),
            pl.BlockSpec((None, 1, D_MODEL), lambda i: (layer, 0, 0)),
        ],
        out_specs=row(D_MODEL),
        out_shape=jax.ShapeDtypeStruct((seq, D_MODEL), jnp.float32),
        scratch_shapes=[pltpu.VMEM((D_MODEL, D_MODEL), jnp.bfloat16)],
        compiler_params=pltpu.CompilerParams(
            dimension_semantics=("arbitrary",), vmem_limit_bytes=VMEM_LIMIT_BYTES),
        name="mixer_out",
    )(x, yab, o, gate, w["w_out"], w["post_g"])


def _prep_weights(pre_norm_g, post_norm_g, w_in, sgu_w, sgu_b, sgu_ln_g, sgu_ln_b, pool_w, pool_scale,
                  q_norm_g, w_uq, kv_norm_g, w_ukv, w_out):
    bf = jnp.bfloat16
    depth = w_in.shape[0]
    w_ukv_r = w_ukv.reshape(depth, KV_LORA_RANK, MLA_HEADS, QK_NOPE_DIM + V_HEAD_DIM)
    w_k = w_ukv_r[..., :QK_NOPE_DIM].reshape(depth, KV_LORA_RANK, MLA_HEADS * QK_NOPE_DIM)
    w_v = w_ukv_r[..., QK_NOPE_DIM:].reshape(depth, KV_LORA_RANK, MLA_HEADS * V_HEAD_DIM)
    groups = len(POOL_WINDOWS)
    same_group = jnp.eye(groups, dtype=bool)[None, :, None, :, None]
    pool_bd = jnp.where(same_group, pool_w[:, :, :, None, :], 0.0).reshape(depth, POOL_WIDTH, POOL_WIDTH)
    return {
        "pre_g": pre_norm_g.reshape(depth, 1, D_MODEL),
        "post_g": post_norm_g.reshape(depth, 1, D_MODEL),
        "w_in": w_in.transpose(0, 2, 1),
        "sgu_w": sgu_w.transpose(0, 2, 1, 3).reshape(depth, SGU_BLOCK, SGU_HEADS * SGU_BLOCK),
        "sgu_bias": jnp.repeat(sgu_b.transpose(0, 2, 1), SGU_HEAD_DIM, axis=2),
        "ln_g": sgu_ln_g.reshape(depth, 1, SGU_WIDTH),
        "ln_b": sgu_ln_b.reshape(depth, 1, SGU_WIDTH),
        "pool_w": pool_bd.astype(bf),
        "pool_scale": pool_scale.reshape(depth, 1, POOL_WIDTH),
        "qn_g": q_norm_g.reshape(depth, 1, Q_LORA_RANK),
        "w_uq_t": w_uq.transpose(0, 2, 1).astype(bf),
        "kvn_g": kv_norm_g.reshape(depth, 1, KV_LORA_RANK),
        "w_k": w_k.astype(bf),
        "w_v_t": w_v.transpose(0, 2, 1).astype(bf),
        "w_out": w_out,
    }


def kernel(x, positions, pre_norm_g, post_norm_g, w_in, sgu_w, sgu_b, sgu_ln_g, sgu_ln_b, pool_w, pool_scale,
           q_norm_g, w_uq, kv_norm_g, w_ukv, w_out):
    bsz, seq, d_model = x.shape
    assert bsz == 1 and seq == SEQ and d_model == D_MODEL
    assert seq % ROW_TILE == 0 and ROW_TILE % ATT_TILE == 0 and seq % TABLE_TILE == 0
    assert seq % OUT_ROW_TILE == 0 and seq % ATT_Q_TILE == 0
    tables = _rope_tables(positions)
    xs = x.reshape(seq, d_model)
    w = _prep_weights(pre_norm_g, post_norm_g, w_in, sgu_w, sgu_b, sgu_ln_g, sgu_ln_b, pool_w, pool_scale,
                      q_norm_g, w_uq, kv_norm_g, w_ukv, w_out)
    for layer in range(pre_norm_g.shape[0]):
        yab, gate, q_t, k, v_t = _mixer_in(xs, w, layer, tables)
        o = _attention(q_t, k, v_t)
        xs = _mixer_out(xs, yab, o, gate, w, layer)
    return xs.reshape(bsz, seq, d_model)
```

```python
import math

import jax
import jax.numpy as jnp
from jax import lax
from jax.experimental import pallas as pl
from jax.experimental.pallas import tpu as pltpu

D_MODEL = 1024
SEQ = 16384
CHUNK = 64
EPS = 1e-6
NEG_INF = -1e30

SGU_WIDTH = 256
SGU_HEADS = 4
SGU_HEAD_DIM = SGU_WIDTH // SGU_HEADS
SGU_BLOCK = 128

POOL_WIDTH = 256
POOL_WINDOWS = (2, 4, 8, 16)
POOL_GROUP_DIM = POOL_WIDTH // len(POOL_WINDOWS)

MLA_WIDTH = 512
MLA_HEADS = 4
V_HEAD_DIM = MLA_WIDTH // MLA_HEADS
QK_NOPE_DIM = 128
QK_ROPE_DIM = 64
QK_HEAD_DIM = QK_NOPE_DIM + QK_ROPE_DIM
Q_LORA_RANK = 384
KV_LORA_RANK = 256
ROPE_BASE = 10000.0
ROPE_HALF = QK_ROPE_DIM // 2

LANES = 128
SUBLANES = 8
MXU_DIM = 256
VMEM_LIMIT_BYTES = 48 * 1024 * 1024
ATT_VMEM_LIMIT_BYTES = 54 * 1024 * 1024

ROW_TILE = 512
OUT_ROW_TILE = 1024
ATT_TILE = 512
ATT_Q_TILE = 2 * ATT_TILE
ATT_HEADS_PER_STEP = 2
SOFTMAX_ROWS = 64
TABLE_TILE = 2048
QK_PAD_DIM = MXU_DIM
VT_ROWS = V_HEAD_DIM + 16
POOL_HALO = 32

OFF_U = 0
OFF_V = OFF_U + SGU_WIDTH
OFF_GA = OFF_V + SGU_WIDTH
OFF_PIN = OFF_GA + SGU_WIDTH
OFF_PG = OFF_PIN + POOL_WIDTH
OFF_CQ = OFF_PG + POOL_WIDTH
OFF_CKV = OFF_CQ + Q_LORA_RANK
OFF_KR = OFF_CKV + KV_LORA_RANK
OFF_MG = OFF_KR + QK_ROPE_DIM
D_IN = OFF_MG + MLA_WIDTH
D_IN_PAD = -(-D_IN // LANES) * LANES

Q_PRESCALE = (QK_HEAD_DIM ** -0.5) * math.log2(math.e)


def _silu(x):
    return x * (1.0 / (1.0 + jnp.exp(-x)))


def _rms(x, g):
    return x * lax.rsqrt(jnp.mean(x * x, axis=-1, keepdims=True) + EPS) * g


def _dot(a, b):
    return jnp.dot(a, b, preferred_element_type=jnp.float32)


def _dot_nt(a, b):
    return lax.dot_general(a, b, (((1,), (1,)), ((), ())), preferred_element_type=jnp.float32)


def _rope_tables_kernel(pos_row_ref, invf_col_ref, cos_t_ref, sin_t_ref, cos_k_ref, sin_k_ref):
    ang_t = invf_col_ref[...] * pos_row_ref[...].astype(jnp.float32)
    cos_t = jnp.cos(ang_t)
    sin_t = jnp.sin(ang_t)
    cos_t_ref[...] = cos_t
    sin_t_ref[...] = sin_t
    pad = jnp.zeros((cos_t.shape[1], LANES - QK_ROPE_DIM), jnp.float32)
    cos_k_ref[...] = jnp.concatenate([cos_t.T, cos_t.T, pad], axis=1)
    sin_k_ref[...] = jnp.concatenate([-sin_t.T, sin_t.T, pad], axis=1)


def _rope_tables(positions):
    seq = positions.shape[-1]
    inv_freq = ROPE_BASE ** (-jnp.arange(0, QK_ROPE_DIM, 2, dtype=jnp.float32) / QK_ROPE_DIM)
    invf_col = inv_freq.reshape(ROPE_HALF, 1)
    pos_row = positions.reshape(1, seq)
    n = seq // TABLE_TILE
    return pl.pallas_call(
        _rope_tables_kernel,
        grid=(n,),
        in_specs=[
            pl.BlockSpec((1, TABLE_TILE), lambda i: (0, i)),
            pl.BlockSpec((ROPE_HALF, 1), lambda i: (0, 0)),
        ],
        out_specs=[
            pl.BlockSpec((ROPE_HALF, TABLE_TILE), lambda i: (0, i)),
            pl.BlockSpec((ROPE_HALF, TABLE_TILE), lambda i: (0, i)),
            pl.BlockSpec((TABLE_TILE, LANES), lambda i: (i, 0)),
            pl.BlockSpec((TABLE_TILE, LANES), lambda i: (i, 0)),
        ],
        out_shape=[
            jax.ShapeDtypeStruct((ROPE_HALF, seq), jnp.float32),
            jax.ShapeDtypeStruct((ROPE_HALF, seq), jnp.float32),
            jax.ShapeDtypeStruct((seq, LANES), jnp.float32),
            jax.ShapeDtypeStruct((seq, LANES), jnp.float32),
        ],
        compiler_params=pltpu.CompilerParams(dimension_semantics=("arbitrary",)),
        name="rope_tables",
    )(pos_row, invf_col)


def _mixer_in_kernel(x_ref, pre_g_ref, w_in_ref, sgu_w_ref, sgu_bias_ref, ln_g_ref, ln_b_ref,
                     pool_w_ref, pool_scale_ref, qn_g_ref, w_uq_t_ref, kvn_g_ref, w_k_ref, w_v_t_ref,
                     cos_t_ref, sin_t_ref, cos_k_ref, sin_k_ref,
                     yab_ref, gate_ref, qt_ref, k_ref, vt_ref,
                     w_bf_ref, z_ref, ext_ref, a2_ref, a4_ref, a8_ref):
    i = pl.program_id(0)
    tm = x_ref.shape[0]

    def mix():
        def proj(off, width):
            return z_ref[:, off:off + width]

        v = proj(OFF_V, SGU_WIDTH)
        mu = jnp.mean(v, axis=-1, keepdims=True)
        vc = v - mu
        var = jnp.mean(vc * vc, axis=-1, keepdims=True)
        vn = vc * lax.rsqrt(var + EPS) * ln_g_ref[...] + ln_b_ref[...]
        w_rows = lax.broadcasted_iota(jnp.int32, (SGU_BLOCK, SGU_HEADS * SGU_BLOCK), 0)
        w_cols = lax.broadcasted_iota(jnp.int32, (SGU_BLOCK, SGU_HEADS * SGU_BLOCK), 1)
        w_keep = ((w_cols % SGU_BLOCK) // CHUNK) <= (w_rows // CHUNK)
        w_cat = jnp.where(w_keep, sgu_w_ref[...], 0.0).astype(jnp.bfloat16)
        head_of_col = lax.broadcasted_iota(jnp.int32, (SGU_BLOCK, SGU_WIDTH), 1) // SGU_HEAD_DIM
        n_blk = tm // SGU_BLOCK
        v_stacks = []
        for r in range(n_blk):
            vb = vn[r * SGU_BLOCK:(r + 1) * SGU_BLOCK, :]
            v_stacks.append(jnp.concatenate(
                [jnp.where(head_of_col == h, vb, 0.0) for h in range(SGU_HEADS)], axis=0).astype(jnp.bfloat16))
        mixed_wide = _dot(w_cat, jnp.concatenate(v_stacks, axis=1))
        mixed = jnp.concatenate(
            [mixed_wide[:, r * SGU_WIDTH:(r + 1) * SGU_WIDTH] + sgu_bias_ref[...] for r in range(n_blk)], axis=0)
        ya = proj(OFF_U, SGU_WIDTH) * mixed * _silu(proj(OFF_GA, SGU_WIDTH))
        yab_ref[:, 0:SGU_WIDTH] = ya.astype(yab_ref.dtype)

        p = proj(OFF_PIN, POOL_WIDTH)

        @pl.when(i == 0)
        def _():
            ext_ref[0:POOL_HALO, :] = jnp.zeros((POOL_HALO, POOL_WIDTH), jnp.float32)

        ext_ref[POOL_HALO:POOL_HALO + tm, :] = p
        end = POOL_HALO + tm
        a2_ref[8:end, :] = ext_ref[8:end, :] + ext_ref[7:end - 1, :]
        a4_ref[16:end, :] = a2_ref[16:end, :] + a2_ref[14:end - 2, :]
        a8_ref[24:end, :] = a4_ref[24:end, :] + a4_ref[20:end - 4, :]
        a16 = a8_ref[POOL_HALO:end, :] + a8_ref[POOL_HALO - 8:end - 8, :]
        group = lax.broadcasted_iota(jnp.int32, (tm, POOL_WIDTH), 1) // POOL_GROUP_DIM
        sums = jnp.where(group == 0, a2_ref[POOL_HALO:end, :],
                         jnp.where(group == 1, a4_ref[POOL_HALO:end, :],
                                   jnp.where(group == 2, a8_ref[POOL_HALO:end, :], a16)))
        window = jnp.where(group == 0, POOL_WINDOWS[0],
                           jnp.where(group == 1, POOL_WINDOWS[1],
                                     jnp.where(group == 2, POOL_WINDOWS[2], POOL_WINDOWS[3])))
        t_glob = i * tm + lax.broadcasted_iota(jnp.int32, (tm, POOL_WIDTH), 0)
        count = jnp.minimum(t_glob + 1, window).astype(jnp.float32)
        pooled = sums / count - p
        ext_ref[0:POOL_HALO, :] = ext_ref[tm:tm + POOL_HALO, :]
        pool_mixed = _dot(pooled.astype(jnp.bfloat16), pool_w_ref[...])
        yb = pool_mixed * pool_scale_ref[...] * _silu(proj(OFF_PG, POOL_WIDTH))
        yab_ref[:, SGU_WIDTH:SGU_WIDTH + POOL_WIDTH] = yb.astype(yab_ref.dtype)

        gate_ref[...] = _silu(proj(OFF_MG, MLA_WIDTH)).astype(gate_ref.dtype)

        cqn = _rms(proj(OFF_CQ, Q_LORA_RANK), qn_g_ref[...]).astype(jnp.bfloat16)
        q_t = _dot_nt(w_uq_t_ref[...], cqn) * Q_PRESCALE
        cos_t = cos_t_ref[...]
        sin_t = sin_t_ref[...]
        for h in range(MLA_HEADS):
            base = h * QK_HEAD_DIM
            x1 = q_t[base + QK_NOPE_DIM:base + QK_NOPE_DIM + ROPE_HALF, :]
            x2 = q_t[base + QK_NOPE_DIM + ROPE_HALF:base + QK_HEAD_DIM, :]
            qt_ref[h, 0:QK_NOPE_DIM, :] = q_t[base:base + QK_NOPE_DIM, :].astype(qt_ref.dtype)
            qt_ref[h, QK_NOPE_DIM:QK_NOPE_DIM + ROPE_HALF, :] = (x1 * cos_t - x2 * sin_t).astype(qt_ref.dtype)
            qt_ref[h, QK_NOPE_DIM + ROPE_HALF:QK_HEAD_DIM, :] = (x2 * cos_t + x1 * sin_t).astype(qt_ref.dtype)
            qt_ref[h, QK_HEAD_DIM:QK_PAD_DIM, :] = jnp.zeros((QK_PAD_DIM - QK_HEAD_DIM, tm), qt_ref.dtype)

        ckvn = _rms(proj(OFF_CKV, KV_LORA_RANK), kvn_g_ref[...]).astype(jnp.bfloat16)
        k_nope = _dot(ckvn, w_k_ref[...])
        v_t = _dot_nt(w_v_t_ref[...], ckvn)
        grp = proj(OFF_KR, LANES)
        lane = lax.broadcasted_iota(jnp.int32, grp.shape, 1)
        partner = jnp.where(lane < ROPE_HALF,
                            pltpu.roll(grp, LANES - ROPE_HALF, axis=1),
                            pltpu.roll(grp, ROPE_HALF, axis=1))
        roped = grp * cos_k_ref[...] + partner * sin_k_ref[...]
        k_pe = jnp.where(lane < QK_ROPE_DIM, roped, 0.0).astype(k_ref.dtype)
        n_sub = tm // ATT_TILE
        for h in range(MLA_HEADS):
            for c in range(n_sub):
                rows = slice(c * ATT_TILE, (c + 1) * ATT_TILE)
                k_ref[h, c, :, 0:QK_NOPE_DIM] = k_nope[rows, h * QK_NOPE_DIM:(h + 1) * QK_NOPE_DIM].astype(k_ref.dtype)
                k_ref[h, c, :, QK_NOPE_DIM:QK_PAD_DIM] = k_pe[rows, :]
                vt_ref[h, c, 0:V_HEAD_DIM, :] = v_t[h * V_HEAD_DIM:(h + 1) * V_HEAD_DIM, rows].astype(vt_ref.dtype)
                ones_row = lax.broadcasted_iota(jnp.int32, (VT_ROWS - V_HEAD_DIM, ATT_TILE), 0) == 0
                vt_ref[h, c, V_HEAD_DIM:VT_ROWS, :] = jnp.where(ones_row, 1.0, 0.0).astype(vt_ref.dtype)

    @pl.when(i == 0)
    def _():
        aligned = (D_IN // LANES) * LANES
        w_bf_ref[:, 0:aligned] = w_in_ref[:, 0:aligned].astype(jnp.bfloat16)
        tail = jnp.concatenate(
            [w_in_ref[:, aligned:D_IN], jnp.zeros((D_MODEL, D_IN_PAD - D_IN), jnp.float32)], axis=1)
        w_bf_ref[:, aligned:D_IN_PAD] = tail.astype(jnp.bfloat16)

    z_ref[...] = _dot(_rms(x_ref[...], pre_g_ref[...]).astype(jnp.bfloat16), w_bf_ref[...])
    mix()


def _mixer_in(x, w, layer, tables):
    seq = x.shape[0]
    tm = ROW_TILE
    n = seq // tm
    n_sub = tm // ATT_TILE
    cos_t, sin_t, cos_k, sin_k = tables

    def const(shape):
        return pl.BlockSpec((None,) + shape, lambda i: (layer,) + (0,) * len(shape))

    return pl.pallas_call(
        _mixer_in_kernel,
        grid=(n,),
        in_specs=[
            pl.BlockSpec((tm, D_MODEL), lambda i: (i, 0)),
            const((1, D_MODEL)),
            pl.BlockSpec((None, D_MODEL, D_IN), lambda i: (layer, 0, 0), pipeline_mode=pl.Buffered(1)),
            const((SGU_BLOCK, SGU_HEADS * SGU_BLOCK)),
            const((SGU_BLOCK, SGU_WIDTH)),
            const((1, SGU_WIDTH)),
            const((1, SGU_WIDTH)),
            const((POOL_WIDTH, POOL_WIDTH)),
            const((1, POOL_WIDTH)),
            const((1, Q_LORA_RANK)),
            const((MLA_HEADS * QK_HEAD_DIM, Q_LORA_RANK)),
            const((1, KV_LORA_RANK)),
            const((KV_LORA_RANK, MLA_HEADS * QK_NOPE_DIM)),
            const((MLA_HEADS * V_HEAD_DIM, KV_LORA_RANK)),
            pl.BlockSpec((ROPE_HALF, tm), lambda i: (0, i)),
            pl.BlockSpec((ROPE_HALF, tm), lambda i: (0, i)),
            pl.BlockSpec((tm, LANES), lambda i: (i, 0)),
            pl.BlockSpec((tm, LANES), lambda i: (i, 0)),
        ],
        out_specs=[
            pl.BlockSpec((tm, SGU_WIDTH + POOL_WIDTH), lambda i: (i, 0)),
            pl.BlockSpec((tm, MLA_WIDTH), lambda i: (i, 0)),
            pl.BlockSpec((MLA_HEADS, QK_PAD_DIM, tm), lambda i: (0, 0, i)),
            pl.BlockSpec((MLA_HEADS, n_sub, ATT_TILE, QK_PAD_DIM), lambda i: (0, i, 0, 0)),
            pl.BlockSpec((MLA_HEADS, n_sub, VT_ROWS, ATT_TILE), lambda i: (0, i, 0, 0)),
        ],
        out_shape=[
            jax.ShapeDtypeStruct((seq, SGU_WIDTH + POOL_WIDTH), jnp.bfloat16),
            jax.ShapeDtypeStruct((seq, MLA_WIDTH), jnp.bfloat16),
            jax.ShapeDtypeStruct((MLA_HEADS, QK_PAD_DIM, seq), jnp.bfloat16),
            jax.ShapeDtypeStruct((MLA_HEADS, seq // ATT_TILE, ATT_TILE, QK_PAD_DIM), jnp.bfloat16),
            jax.ShapeDtypeStruct((MLA_HEADS, seq // ATT_TILE, VT_ROWS, ATT_TILE), jnp.bfloat16),
        ],
        scratch_shapes=[
            pltpu.VMEM((D_MODEL, D_IN_PAD), jnp.bfloat16),
            pltpu.VMEM((tm, D_IN_PAD), jnp.float32),
            pltpu.VMEM((POOL_HALO + tm, POOL_WIDTH), jnp.float32),
            pltpu.VMEM((POOL_HALO + tm, POOL_WIDTH), jnp.float32),
            pltpu.VMEM((POOL_HALO + tm, POOL_WIDTH), jnp.float32),
            pltpu.VMEM((POOL_HALO + tm, POOL_WIDTH), jnp.float32),
        ],
        compiler_params=pltpu.CompilerParams(
            dimension_semantics=("arbitrary",), vmem_limit_bytes=VMEM_LIMIT_BYTES),
        name="mixer_in",
    )(x, w["pre_g"], w["w_in"], w["sgu_w"], w["sgu_bias"], w["ln_g"], w["ln_b"],
      w["pool_w"], w["pool_scale"], w["qn_g"], w["w_uq_t"], w["kvn_g"], w["w_k"], w["w_v_t"],
      cos_t, sin_t, cos_k, sin_k)


def _attention_kernel(qt_ref, qt_next_ref, k_ref, vt_ref, o_ref, m_ref, acc_ref,
                      s0_ref, s1_ref, p0_ref, p1_ref, a0_ref, a1_ref, x0_ref, x1_ref):
    i = pl.program_id(1)
    last = pl.num_programs(1) - 1
    tk = ATT_TILE
    tq = ATT_Q_TILE
    n_heads = qt_ref.shape[0]
    n_tiles = (tq // tk) * (i + 1)
    s_refs = (s0_ref, s1_ref)
    p_refs = (p0_ref, p1_ref)
    a_refs = (a0_ref, a1_ref)
    x_refs = (x0_ref, x1_ref)
    acc_ref[...] = jnp.zeros(acc_ref.shape, jnp.float32)

    LATE = slice(tq // 2, tq)
    ALL = slice(0, tq)

    def score(slot, j, diag=None, q_ref=qt_ref):
        qs = LATE if diag == 1 else ALL
        for h in range(n_heads):
            s = jnp.dot(k_ref[h, j], q_ref[h, :, qs], preferred_element_type=jnp.float32)
            if diag is not None:
                key_chunk = (diag * tk + lax.broadcasted_iota(jnp.int32, s.shape, 0)) // CHUNK
                qry_chunk = (qs.start + lax.broadcasted_iota(jnp.int32, s.shape, 1)) // CHUNK
                s = jnp.where(key_chunk <= qry_chunk, s, NEG_INF)
            s_refs[slot][h, :, qs] = s
            x_refs[slot][h, :, qs] = jnp.max(s, axis=0, keepdims=True)

    def softmax(slot, first=False, qs=ALL):
        rc = SOFTMAX_ROWS
        for h in range(n_heads):
            m_old = jnp.full((1, tq), NEG_INF, jnp.float32) if first else m_ref[h, :, qs]
            m_new = jnp.maximum(m_old, x_refs[slot][h, :, qs])
            m_ref[h, :, qs] = m_new
            a_refs[slot][h, :, qs] = jnp.exp2(m_old - m_new)
            for r in range(0, tk, rc):
                d = s_refs[slot][h, r:r + rc, qs] - m_new
                p_refs[slot][h, r:r + rc, qs] = jnp.exp2(d.astype(jnp.bfloat16))

    def value(slot, j, qs=ALL):
        for h in range(n_heads):
            pv = jnp.dot(vt_ref[h, j], p_refs[slot][h, :, qs], preferred_element_type=jnp.float32)
            acc_ref[h, :, qs] = a_refs[slot][h, :, qs] * acc_ref[h, :, qs] + pv

    def pair(t0, masked):
        for u in range(2):
            score(u, t0 + u, diag=u if masked else None)
            softmax(1 - u)
            value(u, t0 + u - 2)

    def finish():
        for h in range(n_heads):
            o_t = acc_ref[h, 0:V_HEAD_DIM, :] * (1.0 / acc_ref[h, V_HEAD_DIM:V_HEAD_DIM + 1, :])
            o_ref[:, h * V_HEAD_DIM:(h + 1) * V_HEAD_DIM] = o_t.T.astype(o_ref.dtype)

    @pl.when(i == 0)
    def _():
        score(0, 0, diag=0)
        score(1, 1, diag=1)
        softmax(0, first=True)

    @pl.when(i > 0)
    def _():
        def pair_block(jj, carry):
            pair(2 + 2 * jj, masked=False)
            return carry

        lax.fori_loop(0, i - 1, pair_block, 0)
        pair(n_tiles - 2, masked=True)

    @pl.when(i < last)
    def _():
        score(0, 0, q_ref=qt_next_ref)
        softmax(1, qs=LATE)
        value(0, n_tiles - 2)
        score(1, 1, q_ref=qt_next_ref)
        value(1, n_tiles - 1, qs=LATE)
        finish()
        softmax(0, first=True)

    @pl.when(i == last)
    def _():
        softmax(1, qs=LATE)
        value(0, n_tiles - 2)
        value(1, n_tiles - 1, qs=LATE)
        finish()


def _attention(q_t, k, v_t):
    heads, n_kv, tk, _ = k.shape
    seq = n_kv * tk
    tq = ATT_Q_TILE
    hp = ATT_HEADS_PER_STEP
    resident = dict(pipeline_mode=pl.Buffered(1))
    n_q = seq // tq
    return pl.pallas_call(
        _attention_kernel,
        grid=(heads // hp, n_q),
        in_specs=[
            pl.BlockSpec((hp, QK_PAD_DIM, tq), lambda g, i: (g, 0, i)),
            pl.BlockSpec((hp, QK_PAD_DIM, tq), lambda g, i: (g, 0, jnp.minimum(i + 1, n_q - 1))),
            pl.BlockSpec((hp, n_kv, tk, QK_PAD_DIM), lambda g, i: (g, 0, 0, 0), **resident),
            pl.BlockSpec((hp, n_kv, VT_ROWS, tk), lambda g, i: (g, 0, 0, 0), **resident),
        ],
        out_specs=pl.BlockSpec((tq, hp * V_HEAD_DIM), lambda g, i: (i, g)),
        out_shape=jax.ShapeDtypeStruct((seq, heads * V_HEAD_DIM), jnp.bfloat16),
        scratch_shapes=[
            pltpu.VMEM((hp, 1, tq), jnp.float32),
            pltpu.VMEM((hp, VT_ROWS, tq), jnp.float32),
            pltpu.VMEM((hp, tk, tq), jnp.float32),
            pltpu.VMEM((hp, tk, tq), jnp.float32),
            pltpu.VMEM((hp, tk, tq), jnp.bfloat16),
            pltpu.VMEM((hp, tk, tq), jnp.bfloat16),
            pltpu.VMEM((hp, 1, tq), jnp.float32),
            pltpu.VMEM((hp, 1, tq), jnp.float32),
            pltpu.VMEM((hp, 1, tq), jnp.float32),
            pltpu.VMEM((hp, 1, tq), jnp.float32),
        ],
        compiler_params=pltpu.CompilerParams(
            dimension_semantics=("arbitrary", "arbitrary"), vmem_limit_bytes=ATT_VMEM_LIMIT_BYTES),
        name="attention",
    )(q_t, q_t, k, v_t)


def _mixer_out_kernel(x_ref, yab_ref, o_ref, gate_ref, w_out_ref, post_g_ref, out_ref, w_bf_ref):
    @pl.when(pl.program_id(0) == 0)
    def _():
        w_bf_ref[...] = w_out_ref[...].astype(jnp.bfloat16)

    yc = (o_ref[...].astype(jnp.float32) * gate_ref[...].astype(jnp.float32)).astype(jnp.bfloat16)
    y = _dot(jnp.concatenate([yab_ref[...], yc], axis=1), w_bf_ref[...])
    out_ref[...] = x_ref[...] + _rms(y, post_g_ref[...])


def _mixer_out(x, yab, o, gate, w, layer):
    seq = x.shape[0]
    tm = OUT_ROW_TILE
    row = lambda width: pl.BlockSpec((tm, width), lambda i: (i, 0))
    return pl.pallas_call(
        _mixer_out_kernel,
        grid=(seq // tm,),
        in_specs=[
            row(D_MODEL), row(SGU_WIDTH + POOL_WIDTH), row(MLA_WIDTH), row(MLA_WIDTH),
            pl.BlockSpec((None, D_MODEL, D_MODEL), lambda i: (layer, 0, 0)),
            pl.BlockSpec((None, 1, D_MODEL), lambda i: (layer, 0, 0)),
        ],
        out_specs=row(D_MODEL),
        out_shape=jax.ShapeDtypeStruct((seq, D_MODEL), jnp.float32),
        scratch_shapes=[pltpu.VMEM((D_MODEL, D_MODEL), jnp.bfloat16)],
        compiler_params=pltpu.CompilerParams(
            dimension_semantics=("arbitrary",), vmem_limit_bytes=VMEM_LIMIT_BYTES),
        name="mixer_out",
    )(x, yab, o, gate, w["w_out"], w["post_g"])


def _prep_weights(pre_norm_g, post_norm_g, w_in, sgu_w, sgu_b, sgu_ln_g, sgu_ln_b, pool_w, pool_scale,
                  q_norm_g, w_uq, kv_norm_g, w_ukv, w_out):
    bf = jnp.bfloat16
    depth = w_in.shape[0]
    w_ukv_r = w_ukv.reshape(depth, KV_LORA_RANK, MLA_HEADS, QK_NOPE_DIM + V_HEAD_DIM)
    w_k = w_ukv_r[..., :QK_NOPE_DIM].reshape(depth, KV_LORA_RANK, MLA_HEADS * QK_NOPE_DIM)
    w_v = w_ukv_r[..., QK_NOPE_DIM:].reshape(depth, KV_LORA_RANK, MLA_HEADS * V_HEAD_DIM)
    groups = len(POOL_WINDOWS)
    same_group = jnp.eye(groups, dtype=bool)[None, :, None, :, None]
    pool_bd = jnp.where(same_group, pool_w[:, :, :, None, :], 0.0).reshape(depth, POOL_WIDTH, POOL_WIDTH)
    return {
        "pre_g": pre_norm_g.reshape(depth, 1, D_MODEL),
        "post_g": post_norm_g.reshape(depth, 1, D_MODEL),
        "w_in": w_in,
        "sgu_w": sgu_w.transpose(0, 2, 1, 3).reshape(depth, SGU_BLOCK, SGU_HEADS * SGU_BLOCK),
        "sgu_bias": jnp.repeat(sgu_b.transpose(0, 2, 1), SGU_HEAD_DIM, axis=2),
        "ln_g": sgu_ln_g.reshape(depth, 1, SGU_WIDTH),
        "ln_b": sgu_ln_b.reshape(depth, 1, SGU_WIDTH),
        "pool_w": pool_bd.astype(bf),
        "pool_scale": pool_scale.reshape(depth, 1, POOL_WIDTH),
        "qn_g": q_norm_g.reshape(depth, 1, Q_LORA_RANK),
        "w_uq_t": w_uq.transpose(0, 2, 1).astype(bf),
        "kvn_g": kv_norm_g.reshape(depth, 1, KV_LORA_RANK),
        "w_k": w_k.astype(bf),
        "w_v_t": w_v.transpose(0, 2, 1).astype(bf),
        "w_out": w_out,
    }


def kernel(x, positions, pre_norm_g, post_norm_g, w_in, sgu_w, sgu_b, sgu_ln_g, sgu_ln_b, pool_w, pool_scale,
           q_norm_g, w_uq, kv_norm_g, w_ukv, w_out):
    bsz, seq, d_model = x.shape
    assert bsz == 1 and seq == SEQ and d_model == D_MODEL
    assert seq % ROW_TILE == 0 and ROW_TILE % ATT_TILE == 0 and seq % TABLE_TILE == 0
    assert seq % OUT_ROW_TILE == 0 and seq % ATT_Q_TILE == 0
    tables = _rope_tables(positions)
    xs = x.reshape(seq, d_model)
    w = _prep_weights(pre_norm_g, post_norm_g, w_in, sgu_w, sgu_b, sgu_ln_g, sgu_ln_b, pool_w, pool_scale,
                      q_norm_g, w_uq, kv_norm_g, w_ukv, w_out)
    for layer in range(pre_norm_g.shape[0]):
        yab, gate, q_t, k, v_t = _mixer_in(xs, w, layer, tables)
        o = _attention(q_t, k, v_t)
        xs = _mixer_out(xs, yab, o, gate, w, layer)
    return xs.reshape(bsz, seq, d_model)
```

```python
import math

import jax
import jax.numpy as jnp
from jax import lax
from jax.experimental import pallas as pl
from jax.experimental.pallas import tpu as pltpu

D_MODEL = 1024
SEQ = 16384
CHUNK = 64
EPS = 1e-6
NEG_INF = -1e30

SGU_WIDTH = 256
SGU_HEADS = 4
SGU_HEAD_DIM = SGU_WIDTH // SGU_HEADS
SGU_BLOCK = 128

POOL_WIDTH = 256
POOL_WINDOWS = (2, 4, 8, 16)
POOL_GROUP_DIM = POOL_WIDTH // len(POOL_WINDOWS)

MLA_WIDTH = 512
MLA_HEADS = 4
V_HEAD_DIM = MLA_WIDTH // MLA_HEADS
QK_NOPE_DIM = 128
QK_ROPE_DIM = 64
QK_HEAD_DIM = QK_NOPE_DIM + QK_ROPE_DIM
Q_LORA_RANK = 384
KV_LORA_RANK = 256
ROPE_BASE = 10000.0
ROPE_HALF = QK_ROPE_DIM // 2

LANES = 128
SUBLANES = 8
MXU_DIM = 256
VMEM_LIMIT_BYTES = 48 * 1024 * 1024
ATT_VMEM_LIMIT_BYTES = 54 * 1024 * 1024

ROW_TILE = 512
OUT_ROW_TILE = 1024
ATT_TILE = 512
ATT_Q_TILE = 2 * ATT_TILE
ATT_HEADS_PER_STEP = 2
SOFTMAX_ROWS = 64
TABLE_TILE = 2048
QK_PAD_DIM = MXU_DIM
VT_ROWS = V_HEAD_DIM + 16
POOL_HALO = 32

OFF_U = 0
OFF_V = OFF_U + SGU_WIDTH
OFF_GA = OFF_V + SGU_WIDTH
OFF_PIN = OFF_GA + SGU_WIDTH
OFF_PG = OFF_PIN + POOL_WIDTH
OFF_CQ = OFF_PG + POOL_WIDTH
OFF_CKV = OFF_CQ + Q_LORA_RANK
OFF_KR = OFF_CKV + KV_LORA_RANK
OFF_MG = OFF_KR + QK_ROPE_DIM
D_IN = OFF_MG + MLA_WIDTH
D_IN_PAD = -(-D_IN // LANES) * LANES

Q_PRESCALE = (QK_HEAD_DIM ** -0.5) * math.log2(math.e)


def _silu(x):
    return x * (1.0 / (1.0 + jnp.exp(-x)))


def _rms(x, g):
    return x * lax.rsqrt(jnp.mean(x * x, axis=-1, keepdims=True) + EPS) * g


def _dot(a, b):
    return jnp.dot(a, b, preferred_element_type=jnp.float32)


def _dot_nt(a, b):
    return lax.dot_general(a, b, (((1,), (1,)), ((), ())), preferred_element_type=jnp.float32)


def _rope_tables_kernel(pos_row_ref, invf_col_ref, cos_t_ref, sin_t_ref, cos_k_ref, sin_k_ref):
    ang_t = invf_col_ref[...] * pos_row_ref[...].astype(jnp.float32)
    cos_t = jnp.cos(ang_t)
    sin_t = jnp.sin(ang_t)
    cos_t_ref[...] = cos_t
    sin_t_ref[...] = sin_t
    pad = jnp.zeros((cos_t.shape[1], LANES - QK_ROPE_DIM), jnp.float32)
    cos_k_ref[...] = jnp.concatenate([cos_t.T, cos_t.T, pad], axis=1)
    sin_k_ref[...] = jnp.concatenate([-sin_t.T, sin_t.T, pad], axis=1)


def _rope_tables(positions):
    seq = positions.shape[-1]
    inv_freq = ROPE_BASE ** (-jnp.arange(0, QK_ROPE_DIM, 2, dtype=jnp.float32) / QK_ROPE_DIM)
    invf_col = inv_freq.reshape(ROPE_HALF, 1)
    pos_row = positions.reshape(1, seq)
    n = seq // TABLE_TILE
    return pl.pallas_call(
        _rope_tables_kernel,
        grid=(n,),
        in_specs=[
            pl.BlockSpec((1, TABLE_TILE), lambda i: (0, i)),
            pl.BlockSpec((ROPE_HALF, 1), lambda i: (0, 0)),
        ],
        out_specs=[
            pl.BlockSpec((ROPE_HALF, TABLE_TILE), lambda i: (0, i)),
            pl.BlockSpec((ROPE_HALF, TABLE_TILE), lambda i: (0, i)),
            pl.BlockSpec((TABLE_TILE, LANES), lambda i: (i, 0)),
            pl.BlockSpec((TABLE_TILE, LANES), lambda i: (i, 0)),
        ],
        out_shape=[
            jax.ShapeDtypeStruct((ROPE_HALF, seq), jnp.float32),
            jax.ShapeDtypeStruct((ROPE_HALF, seq), jnp.float32),
            jax.ShapeDtypeStruct((seq, LANES), jnp.float32),
            jax.ShapeDtypeStruct((seq, LANES), jnp.float32),
        ],
        compiler_params=pltpu.CompilerParams(dimension_semantics=("arbitrary",)),
        name="rope_tables",
    )(pos_row, invf_col)


def _mixer_in_kernel(x_ref, pre_g_ref, w_in_ref, sgu_w_ref, sgu_bias_ref, ln_g_ref, ln_b_ref,
                     pool_w_ref, pool_scale_ref, qn_g_ref, w_uq_t_ref, kvn_g_ref, w_k_ref, w_v_t_ref,
                     cos_t_ref, sin_t_ref, cos_k_ref, sin_k_ref,
                     yab_ref, gate_ref, qt_ref, k_ref, vt_ref,
                     w_bf_ref, z_ref, ext_ref, a2_ref, a4_ref, a8_ref):
    i = pl.program_id(0)
    tm = x_ref.shape[0]

    def mix():
        def proj(off, width):
            return z_ref[:, off:off + width]

        v = proj(OFF_V, SGU_WIDTH)
        mu = jnp.mean(v, axis=-1, keepdims=True)
        vc = v - mu
        var = jnp.mean(vc * vc, axis=-1, keepdims=True)
        vn = vc * lax.rsqrt(var + EPS) * ln_g_ref[...] + ln_b_ref[...]
        w_rows = lax.broadcasted_iota(jnp.int32, (SGU_BLOCK, SGU_HEADS * SGU_BLOCK), 0)
        w_cols = lax.broadcasted_iota(jnp.int32, (SGU_BLOCK, SGU_HEADS * SGU_BLOCK), 1)
        w_keep = ((w_cols % SGU_BLOCK) // CHUNK) <= (w_rows // CHUNK)
        w_cat = jnp.where(w_keep, sgu_w_ref[...], 0.0).astype(jnp.bfloat16)
        head_of_col = lax.broadcasted_iota(jnp.int32, (SGU_BLOCK, SGU_WIDTH), 1) // SGU_HEAD_DIM
        n_blk = tm // SGU_BLOCK
        v_stacks = []
        for r in range(n_blk):
            vb = vn[r * SGU_BLOCK:(r + 1) * SGU_BLOCK, :]
            v_stacks.append(jnp.concatenate(
                [jnp.where(head_of_col == h, vb, 0.0) for h in range(SGU_HEADS)], axis=0).astype(jnp.bfloat16))
        mixed_wide = _dot(w_cat, jnp.concatenate(v_stacks, axis=1))
        mixed = jnp.concatenate(
            [mixed_wide[:, r * SGU_WIDTH:(r + 1) * SGU_WIDTH] + sgu_bias_ref[...] for r in range(n_blk)], axis=0)
        ya = proj(OFF_U, SGU_WIDTH) * mixed * _silu(proj(OFF_GA, SGU_WIDTH))
        yab_ref[:, 0:SGU_WIDTH] = ya.astype(yab_ref.dtype)

        p = proj(OFF_PIN, POOL_WIDTH)

        @pl.when(i == 0)
        def _():
            ext_ref[0:POOL_HALO, :] = jnp.zeros((POOL_HALO, POOL_WIDTH), jnp.float32)

        ext_ref[POOL_HALO:POOL_HALO + tm, :] = p
        end = POOL_HALO + tm
        a2_ref[8:end, :] = ext_ref[8:end, :] + ext_ref[7:end - 1, :]
        a4_ref[16:end, :] = a2_ref[16:end, :] + a2_ref[14:end - 2, :]
        a8_ref[24:end, :] = a4_ref[24:end, :] + a4_ref[20:end - 4, :]
        a16 = a8_ref[POOL_HALO:end, :] + a8_ref[POOL_HALO - 8:end - 8, :]
        group = lax.broadcasted_iota(jnp.int32, (tm, POOL_WIDTH), 1) // POOL_GROUP_DIM
        sums = jnp.where(group == 0, a2_ref[POOL_HALO:end, :],
                         jnp.where(group == 1, a4_ref[POOL_HALO:end, :],
                                   jnp.where(group == 2, a8_ref[POOL_HALO:end, :], a16)))
        window = jnp.where(group == 0, POOL_WINDOWS[0],
                           jnp.where(group == 1, POOL_WINDOWS[1],
                                     jnp.where(group == 2, POOL_WINDOWS[2], POOL_WINDOWS[3])))
        t_glob = i * tm + lax.broadcasted_iota(jnp.int32, (tm, POOL_WIDTH), 0)
        count = jnp.minimum(t_glob + 1, window).astype(jnp.float32)
        pooled = sums / count - p
        ext_ref[0:POOL_HALO, :] = ext_ref[tm:tm + POOL_HALO, :]
        pool_mixed = _dot(pooled.astype(jnp.bfloat16), pool_w_ref[...])
        yb = pool_mixed * pool_scale_ref[...] * _silu(proj(OFF_PG, POOL_WIDTH))
        yab_ref[:, SGU_WIDTH:SGU_WIDTH + POOL_WIDTH] = yb.astype(yab_ref.dtype)

        gate_ref[...] = _silu(proj(OFF_MG, MLA_WIDTH)).astype(gate_ref.dtype)

        cqn = _rms(proj(OFF_CQ, Q_LORA_RANK), qn_g_ref[...]).astype(jnp.bfloat16)
        q_t = _dot_nt(w_uq_t_ref[...], cqn) * Q_PRESCALE
        cos_t = cos_t_ref[...]
        sin_t = sin_t_ref[...]
        for h in range(MLA_HEADS):
            base = h * QK_HEAD_DIM
            x1 = q_t[base + QK_NOPE_DIM:base + QK_NOPE_DIM + ROPE_HALF, :]
            x2 = q_t[base + QK_NOPE_DIM + ROPE_HALF:base + QK_HEAD_DIM, :]
            qt_ref[h, 0:QK_NOPE_DIM, :] = q_t[base:base + QK_NOPE_DIM, :].astype(qt_ref.dtype)
            qt_ref[h, QK_NOPE_DIM:QK_NOPE_DIM + ROPE_HALF, :] = (x1 * cos_t - x2 * sin_t).astype(qt_ref.dtype)
            qt_ref[h, QK_NOPE_DIM + ROPE_HALF:QK_HEAD_DIM, :] = (x2 * cos_t + x1 * sin_t).astype(qt_ref.dtype)
            qt_ref[h, QK_HEAD_DIM:QK_PAD_DIM, :] = jnp.zeros((QK_PAD_DIM - QK_HEAD_DIM, tm), qt_ref.dtype)

        ckvn = _rms(proj(OFF_CKV, KV_LORA_RANK), kvn_g_ref[...]).astype(jnp.bfloat16)
        k_nope = _dot(ckvn, w_k_ref[...])
        v_t = _dot_nt(w_v_t_ref[...], ckvn)
        grp = proj(OFF_KR, LANES)
        lane = lax.broadcasted_iota(jnp.int32, grp.shape, 1)
        partner = jnp.where(lane < ROPE_HALF,
                            pltpu.roll(grp, LANES - ROPE_HALF, axis=1),
                            pltpu.roll(grp, ROPE_HALF, axis=1))
        roped = grp * cos_k_ref[...] + partner * sin_k_ref[...]
        k_pe = jnp.where(lane < QK_ROPE_DIM, roped, 0.0).astype(k_ref.dtype)
        n_sub = tm // ATT_TILE
        for h in range(MLA_HEADS):
            for c in range(n_sub):
                rows = slice(c * ATT_TILE, (c + 1) * ATT_TILE)
                k_ref[h, c, :, 0:QK_NOPE_DIM] = k_nope[rows, h * QK_NOPE_DIM:(h + 1) * QK_NOPE_DIM].astype(k_ref.dtype)
                k_ref[h, c, :, QK_NOPE_DIM:QK_PAD_DIM] = k_pe[rows, :]
                vt_ref[h, c, 0:V_HEAD_DIM, :] = v_t[h * V_HEAD_DIM:(h + 1) * V_HEAD_DIM, rows].astype(vt_ref.dtype)
                ones_row = lax.broadcasted_iota(jnp.int32, (VT_ROWS - V_HEAD_DIM, ATT_TILE), 0) == 0
                vt_ref[h, c, V_HEAD_DIM:VT_ROWS, :] = jnp.where(ones_row, 1.0, 0.0).astype(vt_ref.dtype)

    @pl.when(i == 0)
    def _():
        aligned = (D_IN // LANES) * LANES
        w_bf_ref[:, 0:aligned] = w_in_ref[:, 0:aligned].astype(jnp.bfloat16)
        tail = jnp.concatenate(
            [w_in_ref[:, aligned:D_IN], jnp.zeros((D_MODEL, D_IN_PAD - D_IN), jnp.float32)], axis=1)
        w_bf_ref[:, aligned:D_IN_PAD] = tail.astype(jnp.bfloat16)

    z_ref[...] = _dot(_rms(x_ref[...], pre_g_ref[...]).astype(jnp.bfloat16), w_bf_ref[...])
    mix()


def _mixer_in(x, w, layer, tables):
    seq = x.shape[0]
    tm = ROW_TILE
    n = seq // tm
    n_sub = tm // ATT_TILE
    cos_t, sin_t, cos_k, sin_k = tables

    def const(shape):
        return pl.BlockSpec((None,) + shape, lambda i: (layer,) + (0,) * len(shape))

    return pl.pallas_call(
        _mixer_in_kernel,
        grid=(n,),
        in_specs=[
            pl.BlockSpec((tm, D_MODEL), lambda i: (i, 0)),
            const((1, D_MODEL)),
            pl.BlockSpec((None, D_MODEL, D_IN), lambda i: (layer, 0, 0), pipeline_mode=pl.Buffered(1)),
            const((SGU_BLOCK, SGU_HEADS * SGU_BLOCK)),
            const((SGU_BLOCK, SGU_WIDTH)),
            const((1, SGU_WIDTH)),
            const((1, SGU_WIDTH)),
            const((POOL_WIDTH, POOL_WIDTH)),
            const((1, POOL_WIDTH)),
            const((1, Q_LORA_RANK)),
            const((MLA_HEADS * QK_HEAD_DIM, Q_LORA_RANK)),
            const((1, KV_LORA_RANK)),
            const((KV_LORA_RANK, MLA_HEADS * QK_NOPE_DIM)),
            const((MLA_HEADS * V_HEAD_DIM, KV_LORA_RANK)),
            pl.BlockSpec((ROPE_HALF, tm), lambda i: (0, i)),
            pl.BlockSpec((ROPE_HALF, tm), lambda i: (0, i)),
            pl.BlockSpec((tm, LANES), lambda i: (i, 0)),
            pl.BlockSpec((tm, LANES), lambda i: (i, 0)),
        ],
        out_specs=[
            pl.BlockSpec((tm, SGU_WIDTH + POOL_WIDTH), lambda i: (i, 0)),
            pl.BlockSpec((tm, MLA_WIDTH), lambda i: (i, 0)),
            pl.BlockSpec((MLA_HEADS, QK_PAD_DIM, tm), lambda i: (0, 0, i)),
            pl.BlockSpec((MLA_HEADS, n_sub, ATT_TILE, QK_PAD_DIM), lambda i: (0, i, 0, 0)),
            pl.BlockSpec((MLA_HEADS, n_sub, VT_ROWS, ATT_TILE), lambda i: (0, i, 0, 0)),
        ],
        out_shape=[
            jax.ShapeDtypeStruct((seq, SGU_WIDTH + POOL_WIDTH), jnp.bfloat16),
            jax.ShapeDtypeStruct((seq, MLA_WIDTH), jnp.bfloat16),
            jax.ShapeDtypeStruct((MLA_HEADS, QK_PAD_DIM, seq), jnp.bfloat16),
            jax.ShapeDtypeStruct((MLA_HEADS, seq // ATT_TILE, ATT_TILE, QK_PAD_DIM), jnp.bfloat16),
            jax.ShapeDtypeStruct((MLA_HEADS, seq // ATT_TILE, VT_ROWS, ATT_TILE), jnp.bfloat16),
        ],
        scratch_shapes=[
            pltpu.VMEM((D_MODEL, D_IN_PAD), jnp.bfloat16),
            pltpu.VMEM((tm, D_IN_PAD), jnp.float32),
            pltpu.VMEM((POOL_HALO + tm, POOL_WIDTH), jnp.float32),
            pltpu.VMEM((POOL_HALO + tm, POOL_WIDTH), jnp.float32),
            pltpu.VMEM((POOL_HALO + tm, POOL_WIDTH), jnp.float32),
            pltpu.VMEM((POOL_HALO + tm, POOL_WIDTH), jnp.float32),
        ],
        compiler_params=pltpu.CompilerParams(
            dimension_semantics=("arbitrary",), vmem_limit_bytes=VMEM_LIMIT_BYTES),
        name="mixer_in",
    )(x, w["pre_g"], w["w_in"], w["sgu_w"], w["sgu_bias"], w["ln_g"], w["ln_b"],
      w["pool_w"], w["pool_scale"], w["qn_g"], w["w_uq_t"], w["kvn_g"], w["w_k"], w["w_v_t"],
      cos_t, sin_t, cos_k, sin_k)


def _attention_kernel(qt_ref, qt_next_ref, k_hbm_ref, vt_hbm_ref, o_ref, k_ref, vt_ref, kv_sem, m_ref, acc_ref,
                      s0_ref, s1_ref, p0_ref, p1_ref, a0_ref, a1_ref, x0_ref, x1_ref):
    i = pl.program_id(1)
    last = pl.num_programs(1) - 1
    tk = ATT_TILE
    tq = ATT_Q_TILE
    n_heads = qt_ref.shape[0]
    n_tiles = (tq // tk) * (i + 1)
    s_refs = (s0_ref, s1_ref)
    p_refs = (p0_ref, p1_ref)
    a_refs = (a0_ref, a1_ref)
    x_refs = (x0_ref, x1_ref)
    acc_ref[...] = jnp.zeros(acc_ref.shape, jnp.float32)

    grp = pl.program_id(0)

    def kv_copies(first_tile):
        heads = pl.ds(grp * n_heads, n_heads)
        tiles = pl.ds(first_tile, tq // tk)
        return (pltpu.make_async_copy(k_hbm_ref.at[heads, tiles], k_ref.at[:, tiles], kv_sem.at[0]),
                pltpu.make_async_copy(vt_hbm_ref.at[heads, tiles], vt_ref.at[:, tiles], kv_sem.at[1]))

    @pl.when(i == 0)
    def _():
        for c in kv_copies(0):
            c.start()

    for c in kv_copies(n_tiles - tq // tk):
        c.wait()

    @pl.when(i < last)
    def _():
        for c in kv_copies(n_tiles):
            c.start()

    LATE = slice(tq // 2, tq)
    ALL = slice(0, tq)

    def score(slot, j, diag=None, q_ref=qt_ref):
        qs = LATE if diag == 1 else ALL
        for h in range(n_heads):
            s = jnp.dot(k_ref[h, j], q_ref[h, :, qs], preferred_element_type=jnp.float32)
            if diag is not None:
                key_chunk = (diag * tk + lax.broadcasted_iota(jnp.int32, s.shape, 0)) // CHUNK
                qry_chunk = (qs.start + lax.broadcasted_iota(jnp.int32, s.shape, 1)) // CHUNK
                s = jnp.where(key_chunk <= qry_chunk, s, NEG_INF)
            s_refs[slot][h, :, qs] = s
            x_refs[slot][h, :, qs] = jnp.max(s, axis=0, keepdims=True)

    def softmax(slot, first=False, qs=ALL):
        rc = SOFTMAX_ROWS
        for h in range(n_heads):
            m_old = jnp.full((1, tq), NEG_INF, jnp.float32) if first else m_ref[h, :, qs]
            m_new = jnp.maximum(m_old, x_refs[slot][h, :, qs])
            m_ref[h, :, qs] = m_new
            a_refs[slot][h, :, qs] = jnp.exp2(m_old - m_new)
            for r in range(0, tk, rc):
                d = s_refs[slot][h, r:r + rc, qs] - m_new
                p_refs[slot][h, r:r + rc, qs] = jnp.exp2(d.astype(jnp.bfloat16))

    def value(slot, j, qs=ALL):
        for h in range(n_heads):
            pv = jnp.dot(vt_ref[h, j], p_refs[slot][h, :, qs], preferred_element_type=jnp.float32)
            acc_ref[h, :, qs] = a_refs[slot][h, :, qs] * acc_ref[h, :, qs] + pv

    def pair(t0, masked):
        for u in range(2):
            score(u, t0 + u, diag=u if masked else None)
            softmax(1 - u)
            value(u, t0 + u - 2)

    def finish():
        for h in range(n_heads):
            o_t = acc_ref[h, 0:V_HEAD_DIM, :] * (1.0 / acc_ref[h, V_HEAD_DIM:V_HEAD_DIM + 1, :])
            o_ref[:, h * V_HEAD_DIM:(h + 1) * V_HEAD_DIM] = o_t.T.astype(o_ref.dtype)

    @pl.when(i == 0)
    def _():
        score(0, 0, diag=0)
        score(1, 1, diag=1)
        softmax(0, first=True)

    @pl.when(i > 0)
    def _():
        def pair_block(jj, carry):
            pair(2 + 2 * jj, masked=False)
            return carry

        lax.fori_loop(0, i - 1, pair_block, 0)
        pair(n_tiles - 2, masked=True)

    @pl.when(i < last)
    def _():
        score(0, 0, q_ref=qt_next_ref)
        softmax(1, qs=LATE)
        value(0, n_tiles - 2)
        score(1, 1, q_ref=qt_next_ref)
        value(1, n_tiles - 1, qs=LATE)
        finish()
        softmax(0, first=True)

    @pl.when(i == last)
    def _():
        softmax(1, qs=LATE)
        value(0, n_tiles - 2)
        value(1, n_tiles - 1, qs=LATE)
        finish()


def _attention(q_t, k, v_t):
    heads, n_kv, tk, _ = k.shape
    seq = n_kv * tk
    tq = ATT_Q_TILE
    hp = ATT_HEADS_PER_STEP
    n_q = seq // tq
    return pl.pallas_call(
        _attention_kernel,
        grid=(heads // hp, n_q),
        in_specs=[
            pl.BlockSpec((hp, QK_PAD_DIM, tq), lambda g, i: (g, 0, i)),
            pl.BlockSpec((hp, QK_PAD_DIM, tq), lambda g, i: (g, 0, jnp.minimum(i + 1, n_q - 1))),
            pl.BlockSpec(memory_space=pl.ANY),
            pl.BlockSpec(memory_space=pl.ANY),
        ],
        out_specs=pl.BlockSpec((tq, hp * V_HEAD_DIM), lambda g, i: (i, g)),
        out_shape=jax.ShapeDtypeStruct((seq, heads * V_HEAD_DIM), jnp.bfloat16),
        scratch_shapes=[
            pltpu.VMEM((hp, n_kv, tk, QK_PAD_DIM), jnp.bfloat16),
            pltpu.VMEM((hp, n_kv, VT_ROWS, tk), jnp.bfloat16),
            pltpu.SemaphoreType.DMA((2,)),
            pltpu.VMEM((hp, 1, tq), jnp.float32),
            pltpu.VMEM((hp, VT_ROWS, tq), jnp.float32),
            pltpu.VMEM((hp, tk, tq), jnp.float32),
            pltpu.VMEM((hp, tk, tq), jnp.float32),
            pltpu.VMEM((hp, tk, tq), jnp.bfloat16),
            pltpu.VMEM((hp, tk, tq), jnp.bfloat16),
            pltpu.VMEM((hp, 1, tq), jnp.float32),
            pltpu.VMEM((hp, 1, tq), jnp.float32),
            pltpu.VMEM((hp, 1, tq), jnp.float32),
            pltpu.VMEM((hp, 1, tq), jnp.float32),
        ],
        compiler_params=pltpu.CompilerParams(
            dimension_semantics=("arbitrary", "arbitrary"), vmem_limit_bytes=ATT_VMEM_LIMIT_BYTES),
        name="attention",
    )(q_t, q_t, k, v_t)


def _mixer_out_kernel(x_ref, yab_ref, o_ref, gate_ref, w_out_ref, post_g_ref, out_ref, w_bf_ref):
    @pl.when(pl.program_id(0) == 0)
    def _():
        w_bf_ref[...] = w_out_ref[...].astype(jnp.bfloat16)

    yc = (o_ref[...].astype(jnp.float32) * gate_ref[...].astype(jnp.float32)).astype(jnp.bfloat16)
    y = _dot(jnp.concatenate([yab_ref[...], yc], axis=1), w_bf_ref[...])
    out_ref[...] = x_ref[...] + _rms(y, post_g_ref[...])


def _mixer_out(x, yab, o, gate, w, layer):
    seq = x.shape[0]
    tm = OUT_ROW_TILE
    row = lambda width: pl.BlockSpec((tm, width), lambda i: (i, 0))
    return pl.pallas_call(
        _mixer_out_kernel,
        grid=(seq // tm,),
        in_specs=[
            row(D_MODEL), row(SGU_WIDTH + POOL_WIDTH), row(MLA_WIDTH), row(MLA_WIDTH),
            pl.BlockSpec((None, D_MODEL, D_MODEL), lambda i: (layer, 0, 0)),
            pl.BlockSpec((None, 1, D_MODEL), lambda i: (layer, 0, 0)),
        ],
        out_specs=row(D_MODEL),
        out_shape=jax.ShapeDtypeStruct((seq, D_MODEL), jnp.float32),
        scratch_shapes=[pltpu.VMEM((D_MODEL, D_MODEL), jnp.bfloat16)],
        compiler_params=pltpu.CompilerParams(
            dimension_semantics=("arbitrary",), vmem_limit_bytes=VMEM_LIMIT_BYTES),
        name="mixer_out",
    )(x, yab, o, gate, w["w_out"], w["post_g"])


def _prep_weights(pre_norm_g, post_norm_g, w_in, sgu_w, sgu_b, sgu_ln_g, sgu_ln_b, pool_w, pool_scale,
                  q_norm_g, w_uq, kv_norm_g, w_ukv, w_out):
    bf = jnp.bfloat16
    depth = w_in.shape[0]
    w_ukv_r = w_ukv.reshape(depth, KV_LORA_RANK, MLA_HEADS, QK_NOPE_DIM + V_HEAD_DIM)
    w_k = w_ukv_r[..., :QK_NOPE_DIM].reshape(depth, KV_LORA_RANK, MLA_HEADS * QK_NOPE_DIM)
    w_v = w_ukv_r[..., QK_NOPE_DIM:].reshape(depth, KV_LORA_RANK, MLA_HEADS * V_HEAD_DIM)
    groups = len(POOL_WINDOWS)
    same_group = jnp.eye(groups, dtype=bool)[None, :, None, :, None]
    pool_bd = jnp.where(same_group, pool_w[:, :, :, None, :], 0.0).reshape(depth, POOL_WIDTH, POOL_WIDTH)
    return {
        "pre_g": pre_norm_g.reshape(depth, 1, D_MODEL),
        "post_g": post_norm_g.reshape(depth, 1, D_MODEL),
        "w_in": w_in,
        "sgu_w": sgu_w.transpose(0, 2, 1, 3).reshape(depth, SGU_BLOCK, SGU_HEADS * SGU_BLOCK),
        "sgu_bias": jnp.repeat(sgu_b.transpose(0, 2, 1), SGU_HEAD_DIM, axis=2),
        "ln_g": sgu_ln_g.reshape(depth, 1, SGU_WIDTH),
        "ln_b": sgu_ln_b.reshape(depth, 1, SGU_WIDTH),
        "pool_w": pool_bd.astype(bf),
        "pool_scale": pool_scale.reshape(depth, 1, POOL_WIDTH),
        "qn_g": q_norm_g.reshape(depth, 1, Q_LORA_RANK),
        "w_uq_t": w_uq.transpose(0, 2, 1).astype(bf),
        "kvn_g": kv_norm_g.reshape(depth, 1, KV_LORA_RANK),
        "w_k": w_k.astype(bf),
        "w_v_t": w_v.transpose(0, 2, 1).astype(bf),
        "w_out": w_out,
    }


def kernel(x, positions, pre_norm_g, post_norm_g, w_in, sgu_w, sgu_b, sgu_ln_g, sgu_ln_b, pool_w, pool_scale,
           q_norm_g, w_uq, kv_norm_g, w_ukv, w_out):
    bsz, seq, d_model = x.shape
    assert bsz == 1 and seq == SEQ and d_model == D_MODEL
    assert seq % ROW_TILE == 0 and ROW_TILE % ATT_TILE == 0 and seq % TABLE_TILE == 0
    assert seq % OUT_ROW_TILE == 0 and seq % ATT_Q_TILE == 0
    tables = _rope_tables(positions)
    xs = x.reshape(seq, d_model)
    w = _prep_weights(pre_norm_g, post_norm_g, w_in, sgu_w, sgu_b, sgu_ln_g, sgu_ln_b, pool_w, pool_scale,
                      q_norm_g, w_uq, kv_norm_g, w_ukv, w_out)
    for layer in range(pre_norm_g.shape[0]):
        yab, gate, q_t, k, v_t = _mixer_in(xs, w, layer, tables)
        o = _attention(q_t, k, v_t)
        xs = _mixer_out(xs, yab, o, gate, w, layer)
    return xs.reshape(bsz, seq, d_model)
```

```python
import math

import jax
import jax.numpy as jnp
from jax import lax
from jax.experimental import pallas as pl
from jax.experimental.pallas import tpu as pltpu

D_MODEL = 1024
SEQ = 16384
CHUNK = 64
EPS = 1e-6
NEG_INF = -1e30

SGU_WIDTH = 256
SGU_HEADS = 4
SGU_HEAD_DIM = SGU_WIDTH // SGU_HEADS
SGU_BLOCK = 128

POOL_WIDTH = 256
POOL_WINDOWS = (2, 4, 8, 16)
POOL_GROUP_DIM = POOL_WIDTH // len(POOL_WINDOWS)

MLA_WIDTH = 512
MLA_HEADS = 4
V_HEAD_DIM = MLA_WIDTH // MLA_HEADS
QK_NOPE_DIM = 128
QK_ROPE_DIM = 64
QK_HEAD_DIM = QK_NOPE_DIM + QK_ROPE_DIM
Q_LORA_RANK = 384
KV_LORA_RANK = 256
ROPE_BASE = 10000.0
ROPE_HALF = QK_ROPE_DIM // 2

LANES = 128
SUBLANES = 8
BF16_SUBLANES = 16
MXU_DIM = 256
VMEM_LIMIT_BYTES = 48 * 1024 * 1024
ATT_VMEM_LIMIT_BYTES = 54 * 1024 * 1024

ROW_TILE = 512
OUT_ROW_TILE = 1024
ATT_TILE = 512
ATT_Q_TILE = 2 * ATT_TILE
ATT_HEADS_PER_STEP = 2
SOFTMAX_ROWS = 64
TABLE_TILE = 2048
QK_PAD_DIM = MXU_DIM
VT_ROWS = V_HEAD_DIM + BF16_SUBLANES
POOL_HALO = 32

OFF_U = 0
OFF_V = OFF_U + SGU_WIDTH
OFF_GA = OFF_V + SGU_WIDTH
OFF_PIN = OFF_GA + SGU_WIDTH
OFF_PG = OFF_PIN + POOL_WIDTH
OFF_CQ = OFF_PG + POOL_WIDTH
OFF_CKV = OFF_CQ + Q_LORA_RANK
OFF_KR = OFF_CKV + KV_LORA_RANK
OFF_MG = OFF_KR + QK_ROPE_DIM
D_IN = OFF_MG + MLA_WIDTH
D_IN_PAD = -(-D_IN // LANES) * LANES

Q_PRESCALE = (QK_HEAD_DIM ** -0.5) * math.log2(math.e)


def _silu(x):
    return x * (1.0 / (1.0 + jnp.exp(-x)))


def _rms(x, g):
    return x * lax.rsqrt(jnp.mean(x * x, axis=-1, keepdims=True) + EPS) * g


def _dot(a, b):
    return jnp.dot(a, b, preferred_element_type=jnp.float32)


def _dot_nt(a, b):
    return lax.dot_general(a, b, (((1,), (1,)), ((), ())), preferred_element_type=jnp.float32)


def _rope_tables_kernel(pos_row_ref, invf_col_ref, cos_t_ref, sin_t_ref, cos_k_ref, sin_k_ref):
    ang_t = invf_col_ref[...] * pos_row_ref[...].astype(jnp.float32)
    cos_t = jnp.cos(ang_t)
    sin_t = jnp.sin(ang_t)
    cos_t_ref[...] = cos_t
    sin_t_ref[...] = sin_t
    pad = jnp.zeros((cos_t.shape[1], LANES - QK_ROPE_DIM), jnp.float32)
    cos_k_ref[...] = jnp.concatenate([cos_t.T, cos_t.T, pad], axis=1)
    sin_k_ref[...] = jnp.concatenate([-sin_t.T, sin_t.T, pad], axis=1)


def _rope_tables(positions):
    seq = positions.shape[-1]
    inv_freq = ROPE_BASE ** (-jnp.arange(0, QK_ROPE_DIM, 2, dtype=jnp.float32) / QK_ROPE_DIM)
    invf_col = inv_freq.reshape(ROPE_HALF, 1)
    pos_row = positions.reshape(1, seq)
    n = seq // TABLE_TILE
    return pl.pallas_call(
        _rope_tables_kernel,
        grid=(n,),
        in_specs=[
            pl.BlockSpec((1, TABLE_TILE), lambda i: (0, i)),
            pl.BlockSpec((ROPE_HALF, 1), lambda i: (0, 0)),
        ],
        out_specs=[
            pl.BlockSpec((ROPE_HALF, TABLE_TILE), lambda i: (0, i)),
            pl.BlockSpec((ROPE_HALF, TABLE_TILE), lambda i: (0, i)),
            pl.BlockSpec((TABLE_TILE, LANES), lambda i: (i, 0)),
            pl.BlockSpec((TABLE_TILE, LANES), lambda i: (i, 0)),
        ],
        out_shape=[
            jax.ShapeDtypeStruct((ROPE_HALF, seq), jnp.float32),
            jax.ShapeDtypeStruct((ROPE_HALF, seq), jnp.float32),
            jax.ShapeDtypeStruct((seq, LANES), jnp.float32),
            jax.ShapeDtypeStruct((seq, LANES), jnp.float32),
        ],
        compiler_params=pltpu.CompilerParams(dimension_semantics=("arbitrary",)),
        name="rope_tables",
    )(pos_row, invf_col)


def _mixer_in_kernel(x_ref, pre_g_ref, w_in_ref, sgu_w_ref, sgu_bias_ref, ln_g_ref, ln_b_ref,
                     pool_w_ref, pool_scale_ref, qn_g_ref, w_uq_t_ref, kvn_g_ref, w_k_ref, w_v_t_ref,
                     cos_t_ref, sin_t_ref, cos_k_ref, sin_k_ref,
                     yab_ref, gate_ref, qt_ref, k_ref, vt_ref,
                     w_bf_ref, z_ref, ext_ref, a2_ref, a4_ref, a8_ref):
    i = pl.program_id(0)
    tm = x_ref.shape[0]

    def mix():
        def proj(off, width):
            return z_ref[:, off:off + width]

        v = proj(OFF_V, SGU_WIDTH)
        mu = jnp.mean(v, axis=-1, keepdims=True)
        vc = v - mu
        var = jnp.mean(vc * vc, axis=-1, keepdims=True)
        vn = vc * lax.rsqrt(var + EPS) * ln_g_ref[...] + ln_b_ref[...]
        w_rows = lax.broadcasted_iota(jnp.int32, (SGU_BLOCK, SGU_HEADS * SGU_BLOCK), 0)
        w_cols = lax.broadcasted_iota(jnp.int32, (SGU_BLOCK, SGU_HEADS * SGU_BLOCK), 1)
        w_keep = ((w_cols % SGU_BLOCK) // CHUNK) <= (w_rows // CHUNK)
        w_cat = jnp.where(w_keep, sgu_w_ref[...], 0.0).astype(jnp.bfloat16)
        head_of_col = lax.broadcasted_iota(jnp.int32, (SGU_BLOCK, SGU_WIDTH), 1) // SGU_HEAD_DIM
        n_blk = tm // SGU_BLOCK
        v_stacks = []
        for r in range(n_blk):
            vb = vn[r * SGU_BLOCK:(r + 1) * SGU_BLOCK, :]
            v_stacks.append(jnp.concatenate(
                [jnp.where(head_of_col == h, vb, 0.0) for h in range(SGU_HEADS)], axis=0).astype(jnp.bfloat16))
        mixed_wide = _dot(w_cat, jnp.concatenate(v_stacks, axis=1))
        mixed = jnp.concatenate(
            [mixed_wide[:, r * SGU_WIDTH:(r + 1) * SGU_WIDTH] + sgu_bias_ref[...] for r in range(n_blk)], axis=0)
        ya = proj(OFF_U, SGU_WIDTH) * mixed * _silu(proj(OFF_GA, SGU_WIDTH))
        yab_ref[:, 0:SGU_WIDTH] = ya.astype(yab_ref.dtype)

        p = proj(OFF_PIN, POOL_WIDTH)

        @pl.when(i == 0)
        def _():
            ext_ref[0:POOL_HALO, :] = jnp.zeros((POOL_HALO, POOL_WIDTH), jnp.float32)

        ext_ref[POOL_HALO:POOL_HALO + tm, :] = p
        end = POOL_HALO + tm
        a2_ref[8:end, :] = ext_ref[8:end, :] + ext_ref[7:end - 1, :]
        a4_ref[16:end, :] = a2_ref[16:end, :] + a2_ref[14:end - 2, :]
        a8_ref[24:end, :] = a4_ref[24:end, :] + a4_ref[20:end - 4, :]
        a16 = a8_ref[POOL_HALO:end, :] + a8_ref[POOL_HALO - 8:end - 8, :]
        group = lax.broadcasted_iota(jnp.int32, (tm, POOL_WIDTH), 1) // POOL_GROUP_DIM
        sums = jnp.where(group == 0, a2_ref[POOL_HALO:end, :],
                         jnp.where(group == 1, a4_ref[POOL_HALO:end, :],
                                   jnp.where(group == 2, a8_ref[POOL_HALO:end, :], a16)))
        window = jnp.where(group == 0, POOL_WINDOWS[0],
                           jnp.where(group == 1, POOL_WINDOWS[1],
                                     jnp.where(group == 2, POOL_WINDOWS[2], POOL_WINDOWS[3])))
        t_glob = i * tm + lax.broadcasted_iota(jnp.int32, (tm, POOL_WIDTH), 0)
        count = jnp.minimum(t_glob + 1, window).astype(jnp.float32)
        pooled = sums / count - p
        ext_ref[0:POOL_HALO, :] = ext_ref[tm:tm + POOL_HALO, :]
        pool_mixed = _dot(pooled.astype(jnp.bfloat16), pool_w_ref[...])
        yb = pool_mixed * pool_scale_ref[...] * _silu(proj(OFF_PG, POOL_WIDTH))
        yab_ref[:, SGU_WIDTH:SGU_WIDTH + POOL_WIDTH] = yb.astype(yab_ref.dtype)

        gate_ref[...] = _silu(proj(OFF_MG, MLA_WIDTH)).astype(gate_ref.dtype)

        cqn = _rms(proj(OFF_CQ, Q_LORA_RANK), qn_g_ref[...]).astype(jnp.bfloat16)
        q_t = _dot_nt(w_uq_t_ref[...], cqn) * Q_PRESCALE
        cos_t = cos_t_ref[...]
        sin_t = sin_t_ref[...]
        for h in range(MLA_HEADS):
            base = h * QK_HEAD_DIM
            x1 = q_t[base + QK_NOPE_DIM:base + QK_NOPE_DIM + ROPE_HALF, :]
            x2 = q_t[base + QK_NOPE_DIM + ROPE_HALF:base + QK_HEAD_DIM, :]
            qt_ref[h, 0:QK_NOPE_DIM, :] = q_t[base:base + QK_NOPE_DIM, :].astype(qt_ref.dtype)
            qt_ref[h, QK_NOPE_DIM:QK_NOPE_DIM + ROPE_HALF, :] = (x1 * cos_t - x2 * sin_t).astype(qt_ref.dtype)
            qt_ref[h, QK_NOPE_DIM + ROPE_HALF:QK_HEAD_DIM, :] = (x2 * cos_t + x1 * sin_t).astype(qt_ref.dtype)
            qt_ref[h, QK_HEAD_DIM:QK_PAD_DIM, :] = jnp.zeros((QK_PAD_DIM - QK_HEAD_DIM, tm), qt_ref.dtype)

        ckvn = _rms(proj(OFF_CKV, KV_LORA_RANK), kvn_g_ref[...]).astype(jnp.bfloat16)
        k_nope = _dot(ckvn, w_k_ref[...])
        v_t = _dot_nt(w_v_t_ref[...], ckvn)
        grp = proj(OFF_KR, LANES)
        lane = lax.broadcasted_iota(jnp.int32, grp.shape, 1)
        partner = jnp.where(lane < ROPE_HALF,
                            pltpu.roll(grp, LANES - ROPE_HALF, axis=1),
                            pltpu.roll(grp, ROPE_HALF, axis=1))
        roped = grp * cos_k_ref[...] + partner * sin_k_ref[...]
        k_pe = jnp.where(lane < QK_ROPE_DIM, roped, 0.0).astype(k_ref.dtype)
        n_sub = tm // ATT_TILE
        for h in range(MLA_HEADS):
            for c in range(n_sub):
                rows = slice(c * ATT_TILE, (c + 1) * ATT_TILE)
                k_ref[h, c, :, 0:QK_NOPE_DIM] = k_nope[rows, h * QK_NOPE_DIM:(h + 1) * QK_NOPE_DIM].astype(k_ref.dtype)
                k_ref[h, c, :, QK_NOPE_DIM:QK_PAD_DIM] = k_pe[rows, :]
                vt_ref[h, c, 0:V_HEAD_DIM, :] = v_t[h * V_HEAD_DIM:(h + 1) * V_HEAD_DIM, rows].astype(vt_ref.dtype)
                ones_row = lax.broadcasted_iota(jnp.int32, (VT_ROWS - V_HEAD_DIM, ATT_TILE), 0) == 0
                vt_ref[h, c, V_HEAD_DIM:VT_ROWS, :] = jnp.where(ones_row, 1.0, 0.0).astype(vt_ref.dtype)

    @pl.when(i == 0)
    def _():
        aligned = (D_IN // LANES) * LANES
        w_bf_ref[:, 0:aligned] = w_in_ref[:, 0:aligned].astype(jnp.bfloat16)
        tail = jnp.concatenate(
            [w_in_ref[:, aligned:D_IN], jnp.zeros((D_MODEL, D_IN_PAD - D_IN), jnp.float32)], axis=1)
        w_bf_ref[:, aligned:D_IN_PAD] = tail.astype(jnp.bfloat16)

    z_ref[...] = _dot(_rms(x_ref[...], pre_g_ref[...]).astype(jnp.bfloat16), w_bf_ref[...])
    mix()


def _mixer_in(x, w, layer, tables):
    seq = x.shape[0]
    tm = ROW_TILE
    n = seq // tm
    n_sub = tm // ATT_TILE
    cos_t, sin_t, cos_k, sin_k = tables

    def const(shape):
        return pl.BlockSpec((None,) + shape, lambda i: (layer,) + (0,) * len(shape))

    return pl.pallas_call(
        _mixer_in_kernel,
        grid=(n,),
        in_specs=[
            pl.BlockSpec((tm, D_MODEL), lambda i: (i, 0)),
            const((1, D_MODEL)),
            pl.BlockSpec((None, D_MODEL, D_IN), lambda i: (layer, 0, 0), pipeline_mode=pl.Buffered(1)),
            const((SGU_BLOCK, SGU_HEADS * SGU_BLOCK)),
            const((SGU_BLOCK, SGU_WIDTH)),
            const((1, SGU_WIDTH)),
            const((1, SGU_WIDTH)),
            const((POOL_WIDTH, POOL_WIDTH)),
            const((1, POOL_WIDTH)),
            const((1, Q_LORA_RANK)),
            const((MLA_HEADS * QK_HEAD_DIM, Q_LORA_RANK)),
            const((1, KV_LORA_RANK)),
            const((KV_LORA_RANK, MLA_HEADS * QK_NOPE_DIM)),
            const((MLA_HEADS * V_HEAD_DIM, KV_LORA_RANK)),
            pl.BlockSpec((ROPE_HALF, tm), lambda i: (0, i)),
            pl.BlockSpec((ROPE_HALF, tm), lambda i: (0, i)),
            pl.BlockSpec((tm, LANES), lambda i: (i, 0)),
            pl.BlockSpec((tm, LANES), lambda i: (i, 0)),
        ],
        out_specs=[
            pl.BlockSpec((tm, SGU_WIDTH + POOL_WIDTH), lambda i: (i, 0)),
            pl.BlockSpec((tm, MLA_WIDTH), lambda i: (i, 0)),
            pl.BlockSpec((MLA_HEADS, QK_PAD_DIM, tm), lambda i: (0, 0, i)),
            pl.BlockSpec((MLA_HEADS, n_sub, ATT_TILE, QK_PAD_DIM), lambda i: (0, i, 0, 0)),
            pl.BlockSpec((MLA_HEADS, n_sub, VT_ROWS, ATT_TILE), lambda i: (0, i, 0, 0)),
        ],
        out_shape=[
            jax.ShapeDtypeStruct((seq, SGU_WIDTH + POOL_WIDTH), jnp.bfloat16),
            jax.ShapeDtypeStruct((seq, MLA_WIDTH), jnp.bfloat16),
            jax.ShapeDtypeStruct((MLA_HEADS, QK_PAD_DIM, seq), jnp.bfloat16),
            jax.ShapeDtypeStruct((MLA_HEADS, seq // ATT_TILE, ATT_TILE, QK_PAD_DIM), jnp.bfloat16),
            jax.ShapeDtypeStruct((MLA_HEADS, seq // ATT_TILE, VT_ROWS, ATT_TILE), jnp.bfloat16),
        ],
        scratch_shapes=[
            pltpu.VMEM((D_MODEL, D_IN_PAD), jnp.bfloat16),
            pltpu.VMEM((tm, D_IN_PAD), jnp.float32),
            pltpu.VMEM((POOL_HALO + tm, POOL_WIDTH), jnp.float32),
            pltpu.VMEM((POOL_HALO + tm, POOL_WIDTH), jnp.float32),
            pltpu.VMEM((POOL_HALO + tm, POOL_WIDTH), jnp.float32),
            pltpu.VMEM((POOL_HALO + tm, POOL_WIDTH), jnp.float32),
        ],
        compiler_params=pltpu.CompilerParams(
            dimension_semantics=("arbitrary",), vmem_limit_bytes=VMEM_LIMIT_BYTES),
        name="mixer_in",
    )(x, w["pre_g"], w["w_in"], w["sgu_w"], w["sgu_bias"], w["ln_g"], w["ln_b"],
      w["pool_w"], w["pool_scale"], w["qn_g"], w["w_uq_t"], w["kvn_g"], w["w_k"], w["w_v_t"],
      cos_t, sin_t, cos_k, sin_k)


def _attention_kernel(qt_ref, qt_next_ref, k_hbm_ref, vt_hbm_ref, o_ref, k_ref, vt_ref, kv_sem, m_ref, acc_ref,
                      s0_ref, s1_ref, p0_ref, p1_ref, a0_ref, a1_ref, x0_ref, x1_ref):
    i = pl.program_id(1)
    last = pl.num_programs(1) - 1
    tk = ATT_TILE
    tq = ATT_Q_TILE
    n_heads = qt_ref.shape[0]
    n_tiles = (tq // tk) * (i + 1)
    s_refs = (s0_ref, s1_ref)
    p_refs = (p0_ref, p1_ref)
    a_refs = (a0_ref, a1_ref)
    x_refs = (x0_ref, x1_ref)
    acc_ref[...] = jnp.zeros(acc_ref.shape, jnp.float32)

    grp = pl.program_id(0)

    def kv_copies(first_tile):
        heads = pl.ds(grp * n_heads, n_heads)
        tiles = pl.ds(first_tile, tq // tk)
        return (pltpu.make_async_copy(k_hbm_ref.at[heads, tiles], k_ref.at[:, tiles], kv_sem.at[0]),
                pltpu.make_async_copy(vt_hbm_ref.at[heads, tiles], vt_ref.at[:, tiles], kv_sem.at[1]))

    @pl.when(i == 0)
    def _():
        for c in kv_copies(0):
            c.start()

    for c in kv_copies(n_tiles - tq // tk):
        c.wait()

    @pl.when(i < last)
    def _():
        for c in kv_copies(n_tiles):
            c.start()

    LATE = slice(tq // 2, tq)
    ALL = slice(0, tq)

    def score(slot, j, diag=None, q_ref=qt_ref):
        qs = LATE if diag == 1 else ALL
        for h in range(n_heads):
            s = jnp.dot(k_ref[h, j], q_ref[h, :, qs], preferred_element_type=jnp.float32)
            if diag is not None:
                key_chunk = (diag * tk + lax.broadcasted_iota(jnp.int32, s.shape, 0)) // CHUNK
                qry_chunk = (qs.start + lax.broadcasted_iota(jnp.int32, s.shape, 1)) // CHUNK
                s = jnp.where(key_chunk <= qry_chunk, s, NEG_INF)
            s_refs[slot][h, :, qs] = s
            x_refs[slot][h, :, qs] = jnp.max(s, axis=0, keepdims=True)

    def softmax(slot, first=False, qs=ALL):
        rc = SOFTMAX_ROWS
        for h in range(n_heads):
            m_old = jnp.full((1, tq), NEG_INF, jnp.float32) if first else m_ref[h, :, qs]
            m_new = jnp.maximum(m_old, x_refs[slot][h, :, qs])
            m_ref[h, :, qs] = m_new
            a_refs[slot][h, :, qs] = jnp.exp2(m_old - m_new)
            for r in range(0, tk, rc):
                d = s_refs[slot][h, r:r + rc, qs] - m_new
                p_refs[slot][h, r:r + rc, qs] = jnp.exp2(d.astype(jnp.bfloat16))

    def value(slot, j, qs=ALL):
        for h in range(n_heads):
            pv = jnp.dot(vt_ref[h, j], p_refs[slot][h, :, qs], preferred_element_type=jnp.float32)
            acc_ref[h, :, qs] = a_refs[slot][h, :, qs] * acc_ref[h, :, qs] + pv

    def pair(t0, masked):
        for u in range(2):
            score(u, t0 + u, diag=u if masked else None)
            softmax(1 - u)
            value(u, t0 + u - 2)

    def finish():
        for h in range(n_heads):
            o_t = acc_ref[h, 0:V_HEAD_DIM, :] * (1.0 / acc_ref[h, V_HEAD_DIM:V_HEAD_DIM + 1, :])
            o_ref[:, h * V_HEAD_DIM:(h + 1) * V_HEAD_DIM] = o_t.T.astype(o_ref.dtype)

    @pl.when(i == 0)
    def _():
        score(0, 0, diag=0)
        score(1, 1, diag=1)
        softmax(0, first=True)

    @pl.when(i > 0)
    def _():
        def pair_block(jj, carry):
            pair(2 + 2 * jj, masked=False)
            return carry

        lax.fori_loop(0, i - 1, pair_block, 0)
        pair(n_tiles - 2, masked=True)

    @pl.when(i < last)
    def _():
        score(0, 0, q_ref=qt_next_ref)
        softmax(1, qs=LATE)
        value(0, n_tiles - 2)
        score(1, 1, q_ref=qt_next_ref)
        value(1, n_tiles - 1, qs=LATE)
        finish()
        softmax(0, first=True)

    @pl.when(i == last)
    def _():
        softmax(1, qs=LATE)
        value(0, n_tiles - 2)
        value(1, n_tiles - 1, qs=LATE)
        finish()


def _attention(q_t, k, v_t):
    heads, n_kv, tk, _ = k.shape
    seq = n_kv * tk
    tq = ATT_Q_TILE
    hp = ATT_HEADS_PER_STEP
    n_q = seq // tq
    return pl.pallas_call(
        _attention_kernel,
        grid=(heads // hp, n_q),
        in_specs=[
            pl.BlockSpec((hp, QK_PAD_DIM, tq), lambda g, i: (g, 0, i)),
            pl.BlockSpec((hp, QK_PAD_DIM, tq), lambda g, i: (g, 0, jnp.minimum(i + 1, n_q - 1))),
            pl.BlockSpec(memory_space=pl.ANY),
            pl.BlockSpec(memory_space=pl.ANY),
        ],
        out_specs=pl.BlockSpec((tq, hp * V_HEAD_DIM), lambda g, i: (i, g)),
        out_shape=jax.ShapeDtypeStruct((seq, heads * V_HEAD_DIM), jnp.bfloat16),
        scratch_shapes=[
            pltpu.VMEM((hp, n_kv, tk, QK_PAD_DIM), jnp.bfloat16),
            pltpu.VMEM((hp, n_kv, VT_ROWS, tk), jnp.bfloat16),
            pltpu.SemaphoreType.DMA((2,)),
            pltpu.VMEM((hp, 1, tq), jnp.float32),
            pltpu.VMEM((hp, VT_ROWS, tq), jnp.float32),
            pltpu.VMEM((hp, tk, tq), jnp.float32),
            pltpu.VMEM((hp, tk, tq), jnp.float32),
            pltpu.VMEM((hp, tk, tq), jnp.bfloat16),
            pltpu.VMEM((hp, tk, tq), jnp.bfloat16),
            pltpu.VMEM((hp, 1, tq), jnp.float32),
            pltpu.VMEM((hp, 1, tq), jnp.float32),
            pltpu.VMEM((hp, 1, tq), jnp.float32),
            pltpu.VMEM((hp, 1, tq), jnp.float32),
        ],
        compiler_params=pltpu.CompilerParams(
            dimension_semantics=("arbitrary", "arbitrary"), vmem_limit_bytes=ATT_VMEM_LIMIT_BYTES),
        name="attention",
    )(q_t, q_t, k, v_t)


def _mixer_out_kernel(x_ref, yab_ref, o_ref, gate_ref, w_out_ref, post_g_ref, out_ref, w_bf_ref):
    @pl.when(pl.program_id(0) == 0)
    def _():
        w_bf_ref[...] = w_out_ref[...].astype(jnp.bfloat16)

    yc = (o_ref[...].astype(jnp.float32) * gate_ref[...].astype(jnp.float32)).astype(jnp.bfloat16)
    y = _dot(jnp.concatenate([yab_ref[...], yc], axis=1), w_bf_ref[...])
    out_ref[...] = x_ref[...] + _rms(y, post_g_ref[...])


def _mixer_out(x, yab, o, gate, w, layer):
    seq = x.shape[0]
    tm = OUT_ROW_TILE
    row = lambda width: pl.BlockSpec((tm, width), lambda i: (i, 0))
    return pl.pallas_call(
        _mixer_out_kernel,
        grid=(seq // tm,),
        in_specs=[
            row(D_MODEL), row(SGU_WIDTH + POOL_WIDTH), row(MLA_WIDTH), row(MLA_WIDTH),
            pl.BlockSpec((None, D_MODEL, D_MODEL), lambda i: (layer, 0, 0)),
            pl.BlockSpec((None, 1, D_MODEL), lambda i: (layer, 0, 0)),
        ],
        out_specs=row(D_MODEL),
        out_shape=jax.ShapeDtypeStruct((seq, D_MODEL), jnp.float32),
        scratch_shapes=[pltpu.VMEM((D_MODEL, D_MODEL), jnp.bfloat16)],
        compiler_params=pltpu.CompilerParams(
            dimension_semantics=("arbitrary",), vmem_limit_bytes=VMEM_LIMIT_BYTES),
        name="mixer_out",
    )(x, yab, o, gate, w["w_out"], w["post_g"])


def _prep_weights(pre_norm_g, post_norm_g, w_in, sgu_w, sgu_b, sgu_ln_g, sgu_ln_b, pool_w, pool_scale,
                  q_norm_g, w_uq, kv_norm_g, w_ukv, w_out):
    bf = jnp.bfloat16
    depth = w_in.shape[0]
    w_ukv_r = w_ukv.reshape(depth, KV_LORA_RANK, MLA_HEADS, QK_NOPE_DIM + V_HEAD_DIM)
    w_k = w_ukv_r[..., :QK_NOPE_DIM].reshape(depth, KV_LORA_RANK, MLA_HEADS * QK_NOPE_DIM)
    w_v = w_ukv_r[..., QK_NOPE_DIM:].reshape(depth, KV_LORA_RANK, MLA_HEADS * V_HEAD_DIM)
    groups = len(POOL_WINDOWS)
    same_group = jnp.eye(groups, dtype=bool)[None, :, None, :, None]
    pool_bd = jnp.where(same_group, pool_w[:, :, :, None, :], 0.0).reshape(depth, POOL_WIDTH, POOL_WIDTH)
    return {
        "pre_g": pre_norm_g.reshape(depth, 1, D_MODEL),
        "post_g": post_norm_g.reshape(depth, 1, D_MODEL),
        "w_in": w_in,
        "sgu_w": sgu_w.transpose(0, 2, 1, 3).reshape(depth, SGU_BLOCK, SGU_HEADS * SGU_BLOCK),
        "sgu_bias": jnp.repeat(sgu_b.transpose(0, 2, 1), SGU_HEAD_DIM, axis=2),
        "ln_g": sgu_ln_g.reshape(depth, 1, SGU_WIDTH),
        "ln_b": sgu_ln_b.reshape(depth, 1, SGU_WIDTH),
        "pool_w": pool_bd.astype(bf),
        "pool_scale": pool_scale.reshape(depth, 1, POOL_WIDTH),
        "qn_g": q_norm_g.reshape(depth, 1, Q_LORA_RANK),
        "w_uq_t": w_uq.transpose(0, 2, 1).astype(bf),
        "kvn_g": kv_norm_g.reshape(depth, 1, KV_LORA_RANK),
        "w_k": w_k.astype(bf),
        "w_v_t": w_v.transpose(0, 2, 1).astype(bf),
        "w_out": w_out,
    }


def kernel(x, positions, pre_norm_g, post_norm_g, w_in, sgu_w, sgu_b, sgu_ln_g, sgu_ln_b, pool_w, pool_scale,
           q_norm_g, w_uq, kv_norm_g, w_ukv, w_out):
    bsz, seq, d_model = x.shape
    assert bsz == 1 and seq == SEQ and d_model == D_MODEL
    assert seq % ROW_TILE == 0 and ROW_TILE % ATT_TILE == 0 and seq % TABLE_TILE == 0
    assert seq % OUT_ROW_TILE == 0 and seq % ATT_Q_TILE == 0
    tables = _rope_tables(positions)
    xs = x.reshape(seq, d_model)
    w = _prep_weights(pre_norm_g, post_norm_g, w_in, sgu_w, sgu_b, sgu_ln_g, sgu_ln_b, pool_w, pool_scale,
                      q_norm_g, w_uq, kv_norm_g, w_ukv, w_out)
    for layer in range(pre_norm_g.shape[0]):
        yab, gate, q_t, k, v_t = _mixer_in(xs, w, layer, tables)
        o = _attention(q_t, k, v_t)
        xs = _mixer_out(xs, yab, o, gate, w, layer)
    return xs.reshape(bsz, seq, d_model)
```

```python
import math

import jax
import jax.numpy as jnp
from jax import lax
from jax.experimental import pallas as pl
from jax.experimental.pallas import tpu as pltpu

D_MODEL = 1024
SEQ = 16384
CHUNK = 64
EPS = 1e-6
NEG_INF = -1e30

SGU_WIDTH = 256
SGU_HEADS = 4
SGU_HEAD_DIM = SGU_WIDTH // SGU_HEADS
SGU_BLOCK = 128

POOL_WIDTH = 256
POOL_WINDOWS = (2, 4, 8, 16)
POOL_GROUP_DIM = POOL_WIDTH // len(POOL_WINDOWS)

MLA_WIDTH = 512
MLA_HEADS = 4
V_HEAD_DIM = MLA_WIDTH // MLA_HEADS
QK_NOPE_DIM = 128
QK_ROPE_DIM = 64
QK_HEAD_DIM = QK_NOPE_DIM + QK_ROPE_DIM
Q_LORA_RANK = 384
KV_LORA_RANK = 256
ROPE_BASE = 10000.0
ROPE_HALF = QK_ROPE_DIM // 2

LANES = 128
SUBLANES = 8
BF16_SUBLANES = 16
MXU_DIM = 256
VMEM_LIMIT_BYTES = 48 * 1024 * 1024
ATT_VMEM_LIMIT_BYTES = 54 * 1024 * 1024

ROW_TILE = 512
OUT_ROW_TILE = 1024
ATT_TILE = 512
ATT_Q_TILE = 2 * ATT_TILE
ATT_HEADS_PER_STEP = 2
SOFTMAX_ROWS = 64
TABLE_TILE = 2048
QK_PAD_DIM = MXU_DIM
VT_ROWS = V_HEAD_DIM + BF16_SUBLANES
POOL_HALO = 32

OFF_U = 0
OFF_V = OFF_U + SGU_WIDTH
OFF_GA = OFF_V + SGU_WIDTH
OFF_PIN = OFF_GA + SGU_WIDTH
OFF_PG = OFF_PIN + POOL_WIDTH
OFF_CQ = OFF_PG + POOL_WIDTH
OFF_CKV = OFF_CQ + Q_LORA_RANK
OFF_KR = OFF_CKV + KV_LORA_RANK
OFF_MG = OFF_KR + QK_ROPE_DIM
D_IN = OFF_MG + MLA_WIDTH
D_IN_PAD = -(-D_IN // LANES) * LANES

Q_PRESCALE = (QK_HEAD_DIM ** -0.5) * math.log2(math.e)


def _silu(x):
    return x * (1.0 / (1.0 + jnp.exp(-x)))


def _rms(x, g):
    return x * lax.rsqrt(jnp.mean(x * x, axis=-1, keepdims=True) + EPS) * g


def _dot(a, b):
    return jnp.dot(a, b, preferred_element_type=jnp.float32)


def _dot_nt(a, b):
    return lax.dot_general(a, b, (((1,), (1,)), ((), ())), preferred_element_type=jnp.float32)


def _rope_tables_kernel(pos_row_ref, invf_col_ref, cos_t_ref, sin_t_ref, cos_k_ref, sin_k_ref):
    ang_t = invf_col_ref[...] * pos_row_ref[...].astype(jnp.float32)
    cos_t = jnp.cos(ang_t)
    sin_t = jnp.sin(ang_t)
    cos_t_ref[...] = cos_t
    sin_t_ref[...] = sin_t
    pad = jnp.zeros((cos_t.shape[1], LANES - QK_ROPE_DIM), jnp.float32)
    cos_k_ref[...] = jnp.concatenate([cos_t.T, cos_t.T, pad], axis=1)
    sin_k_ref[...] = jnp.concatenate([-sin_t.T, sin_t.T, pad], axis=1)


def _rope_tables(positions):
    seq = positions.shape[-1]
    inv_freq = ROPE_BASE ** (-jnp.arange(0, QK_ROPE_DIM, 2, dtype=jnp.float32) / QK_ROPE_DIM)
    invf_col = inv_freq.reshape(ROPE_HALF, 1)
    pos_row = positions.reshape(1, seq)
    n = seq // TABLE_TILE
    return pl.pallas_call(
        _rope_tables_kernel,
        grid=(n,),
        in_specs=[
            pl.BlockSpec((1, TABLE_TILE), lambda i: (0, i)),
            pl.BlockSpec((ROPE_HALF, 1), lambda i: (0, 0)),
        ],
        out_specs=[
            pl.BlockSpec((ROPE_HALF, TABLE_TILE), lambda i: (0, i)),
            pl.BlockSpec((ROPE_HALF, TABLE_TILE), lambda i: (0, i)),
            pl.BlockSpec((TABLE_TILE, LANES), lambda i: (i, 0)),
            pl.BlockSpec((TABLE_TILE, LANES), lambda i: (i, 0)),
        ],
        out_shape=[
            jax.ShapeDtypeStruct((ROPE_HALF, seq), jnp.float32),
            jax.ShapeDtypeStruct((ROPE_HALF, seq), jnp.float32),
            jax.ShapeDtypeStruct((seq, LANES), jnp.float32),
            jax.ShapeDtypeStruct((seq, LANES), jnp.float32),
        ],
        compiler_params=pltpu.CompilerParams(dimension_semantics=("arbitrary",)),
        name="rope_tables",
    )(pos_row, invf_col)


def _mixer_in_kernel(x_ref, pre_g_ref, w_in_ref, sgu_w_ref, sgu_bias_ref, ln_g_ref, ln_b_ref,
                     pool_w_ref, pool_scale_ref, qn_g_ref, w_uq_t_ref, kvn_g_ref, w_k_ref, w_v_t_ref,
                     cos_t_ref, sin_t_ref, cos_k_ref, sin_k_ref,
                     yab_ref, gate_ref, qt_ref, k_ref, vt_ref,
                     w_bf_ref, z_ref, ext_ref, a2_ref, a4_ref, a8_ref):
    i = pl.program_id(0)
    tm = x_ref.shape[0]

    def mix():
        def proj(off, width):
            return z_ref[:, off:off + width]

        v = proj(OFF_V, SGU_WIDTH)
        mu = jnp.mean(v, axis=-1, keepdims=True)
        vc = v - mu
        var = jnp.mean(vc * vc, axis=-1, keepdims=True)
        vn = vc * lax.rsqrt(var + EPS) * ln_g_ref[...] + ln_b_ref[...]
        w_rows = lax.broadcasted_iota(jnp.int32, (SGU_BLOCK, SGU_HEADS * SGU_BLOCK), 0)
        w_cols = lax.broadcasted_iota(jnp.int32, (SGU_BLOCK, SGU_HEADS * SGU_BLOCK), 1)
        w_keep = ((w_cols % SGU_BLOCK) // CHUNK) <= (w_rows // CHUNK)
        w_cat = jnp.where(w_keep, sgu_w_ref[...], 0.0).astype(jnp.bfloat16)
        head_of_col = lax.broadcasted_iota(jnp.int32, (SGU_BLOCK, SGU_WIDTH), 1) // SGU_HEAD_DIM
        n_blk = tm // SGU_BLOCK
        v_stacks = []
        for r in range(n_blk):
            vb = vn[r * SGU_BLOCK:(r + 1) * SGU_BLOCK, :]
            v_stacks.append(jnp.concatenate(
                [jnp.where(head_of_col == h, vb, 0.0) for h in range(SGU_HEADS)], axis=0).astype(jnp.bfloat16))
        mixed_wide = _dot(w_cat, jnp.concatenate(v_stacks, axis=1))
        mixed = jnp.concatenate(
            [mixed_wide[:, r * SGU_WIDTH:(r + 1) * SGU_WIDTH] + sgu_bias_ref[...] for r in range(n_blk)], axis=0)
        ya = proj(OFF_U, SGU_WIDTH) * mixed * _silu(proj(OFF_GA, SGU_WIDTH))
        yab_ref[:, 0:SGU_WIDTH] = ya.astype(yab_ref.dtype)

        p = proj(OFF_PIN, POOL_WIDTH)

        @pl.when(i == 0)
        def _():
            ext_ref[0:POOL_HALO, :] = jnp.zeros((POOL_HALO, POOL_WIDTH), jnp.float32)

        ext_ref[POOL_HALO:POOL_HALO + tm, :] = p
        end = POOL_HALO + tm
        a2_ref[8:end, :] = ext_ref[8:end, :] + ext_ref[7:end - 1, :]
        a4_ref[16:end, :] = a2_ref[16:end, :] + a2_ref[14:end - 2, :]
        a8_ref[24:end, :] = a4_ref[24:end, :] + a4_ref[20:end - 4, :]
        a16 = a8_ref[POOL_HALO:end, :] + a8_ref[POOL_HALO - 8:end - 8, :]
        group = lax.broadcasted_iota(jnp.int32, (tm, POOL_WIDTH), 1) // POOL_GROUP_DIM
        sums = jnp.where(group == 0, a2_ref[POOL_HALO:end, :],
                         jnp.where(group == 1, a4_ref[POOL_HALO:end, :],
                                   jnp.where(group == 2, a8_ref[POOL_HALO:end, :], a16)))
        window = jnp.where(group == 0, POOL_WINDOWS[0],
                           jnp.where(group == 1, POOL_WINDOWS[1],
                                     jnp.where(group == 2, POOL_WINDOWS[2], POOL_WINDOWS[3])))
        t_glob = i * tm + lax.broadcasted_iota(jnp.int32, (tm, POOL_WIDTH), 0)
        count = jnp.minimum(t_glob + 1, window).astype(jnp.float32)
        pooled = sums / count - p
        ext_ref[0:POOL_HALO, :] = ext_ref[tm:tm + POOL_HALO, :]
        pool_mixed = _dot(pooled.astype(jnp.bfloat16), pool_w_ref[...])
        yb = pool_mixed * pool_scale_ref[...] * _silu(proj(OFF_PG, POOL_WIDTH))
        yab_ref[:, SGU_WIDTH:SGU_WIDTH + POOL_WIDTH] = yb.astype(yab_ref.dtype)

        gate_ref[...] = _silu(proj(OFF_MG, MLA_WIDTH)).astype(gate_ref.dtype)

        cqn = _rms(proj(OFF_CQ, Q_LORA_RANK), qn_g_ref[...]).astype(jnp.bfloat16)
        q_t = _dot_nt(w_uq_t_ref[...], cqn) * Q_PRESCALE
        cos_t = cos_t_ref[...]
        sin_t = sin_t_ref[...]
        for h in range(MLA_HEADS):
            base = h * QK_HEAD_DIM
            x1 = q_t[base + QK_NOPE_DIM:base + QK_NOPE_DIM + ROPE_HALF, :]
            x2 = q_t[base + QK_NOPE_DIM + ROPE_HALF:base + QK_HEAD_DIM, :]
            qt_ref[h, 0:QK_NOPE_DIM, :] = q_t[base:base + QK_NOPE_DIM, :].astype(qt_ref.dtype)
            qt_ref[h, QK_NOPE_DIM:QK_NOPE_DIM + ROPE_HALF, :] = (x1 * cos_t - x2 * sin_t).astype(qt_ref.dtype)
            qt_ref[h, QK_NOPE_DIM + ROPE_HALF:QK_HEAD_DIM, :] = (x2 * cos_t + x1 * sin_t).astype(qt_ref.dtype)
            qt_ref[h, QK_HEAD_DIM:QK_PAD_DIM, :] = jnp.zeros((QK_PAD_DIM - QK_HEAD_DIM, tm), qt_ref.dtype)

        ckvn = _rms(proj(OFF_CKV, KV_LORA_RANK), kvn_g_ref[...]).astype(jnp.bfloat16)
        k_nope = _dot(ckvn, w_k_ref[...])
        v_t = _dot_nt(w_v_t_ref[...], ckvn)
        grp = proj(OFF_KR, LANES)
        lane = lax.broadcasted_iota(jnp.int32, grp.shape, 1)
        partner = jnp.where(lane < ROPE_HALF,
                            pltpu.roll(grp, LANES - ROPE_HALF, axis=1),
                            pltpu.roll(grp, ROPE_HALF, axis=1))
        roped = grp * cos_k_ref[...] + partner * sin_k_ref[...]
        k_pe = jnp.where(lane < QK_ROPE_DIM, roped, 0.0).astype(k_ref.dtype)
        n_sub = tm // ATT_TILE
        for h in range(MLA_HEADS):
            for c in range(n_sub):
                rows = slice(c * ATT_TILE, (c + 1) * ATT_TILE)
                k_ref[h, c, :, 0:QK_NOPE_DIM] = k_nope[rows, h * QK_NOPE_DIM:(h + 1) * QK_NOPE_DIM].astype(k_ref.dtype)
                k_ref[h, c, :, QK_NOPE_DIM:QK_PAD_DIM] = k_pe[rows, :]
                vt_ref[h, c, 0:V_HEAD_DIM, :] = v_t[h * V_HEAD_DIM:(h + 1) * V_HEAD_DIM, rows].astype(vt_ref.dtype)
                ones_row = lax.broadcasted_iota(jnp.int32, (VT_ROWS - V_HEAD_DIM, ATT_TILE), 0) == 0
                vt_ref[h, c, V_HEAD_DIM:VT_ROWS, :] = jnp.where(ones_row, 1.0, 0.0).astype(vt_ref.dtype)

    @pl.when(i == 0)
    def _():
        aligned = (D_IN // LANES) * LANES
        w_bf_ref[:, 0:aligned] = w_in_ref[:, 0:aligned].astype(jnp.bfloat16)
        tail = jnp.concatenate(
            [w_in_ref[:, aligned:D_IN], jnp.zeros((D_MODEL, D_IN_PAD - D_IN), jnp.float32)], axis=1)
        w_bf_ref[:, aligned:D_IN_PAD] = tail.astype(jnp.bfloat16)

    z_ref[...] = _dot(_rms(x_ref[...], pre_g_ref[...]).astype(jnp.bfloat16), w_bf_ref[...])
    mix()


def _mixer_in(x, w, layer, tables):
    seq = x.shape[0]
    tm = ROW_TILE
    n = seq // tm
    n_sub = tm // ATT_TILE
    cos_t, sin_t, cos_k, sin_k = tables

    def const(shape):
        return pl.BlockSpec((None,) + shape, lambda i: (layer,) + (0,) * len(shape))

    return pl.pallas_call(
        _mixer_in_kernel,
        grid=(n,),
        in_specs=[
            pl.BlockSpec((tm, D_MODEL), lambda i: (i, 0)),
            const((1, D_MODEL)),
            pl.BlockSpec((None, D_MODEL, D_IN), lambda i: (layer, 0, 0), pipeline_mode=pl.Buffered(1)),
            const((SGU_BLOCK, SGU_HEADS * SGU_BLOCK)),
            const((SGU_BLOCK, SGU_WIDTH)),
            const((1, SGU_WIDTH)),
            const((1, SGU_WIDTH)),
            const((POOL_WIDTH, POOL_WIDTH)),
            const((1, POOL_WIDTH)),
            const((1, Q_LORA_RANK)),
            const((MLA_HEADS * QK_HEAD_DIM, Q_LORA_RANK)),
            const((1, KV_LORA_RANK)),
            const((KV_LORA_RANK, MLA_HEADS * QK_NOPE_DIM)),
            const((MLA_HEADS * V_HEAD_DIM, KV_LORA_RANK)),
            pl.BlockSpec((ROPE_HALF, tm), lambda i: (0, i)),
            pl.BlockSpec((ROPE_HALF, tm), lambda i: (0, i)),
            pl.BlockSpec((tm, LANES), lambda i: (i, 0)),
            pl.BlockSpec((tm, LANES), lambda i: (i, 0)),
        ],
        out_specs=[
            pl.BlockSpec((tm, SGU_WIDTH + POOL_WIDTH), lambda i: (i, 0)),
            pl.BlockSpec((tm, MLA_WIDTH), lambda i: (i, 0)),
            pl.BlockSpec((MLA_HEADS, QK_PAD_DIM, tm), lambda i: (0, 0, i)),
            pl.BlockSpec((MLA_HEADS, n_sub, ATT_TILE, QK_PAD_DIM), lambda i: (0, i, 0, 0)),
            pl.BlockSpec((MLA_HEADS, n_sub, VT_ROWS, ATT_TILE), lambda i: (0, i, 0, 0)),
        ],
        out_shape=[
            jax.ShapeDtypeStruct((seq, SGU_WIDTH + POOL_WIDTH), jnp.bfloat16),
            jax.ShapeDtypeStruct((seq, MLA_WIDTH), jnp.bfloat16),
            jax.ShapeDtypeStruct((MLA_HEADS, QK_PAD_DIM, seq), jnp.bfloat16),
            jax.ShapeDtypeStruct((MLA_HEADS, seq // ATT_TILE, ATT_TILE, QK_PAD_DIM), jnp.bfloat16),
            jax.ShapeDtypeStruct((MLA_HEADS, seq // ATT_TILE, VT_ROWS, ATT_TILE), jnp.bfloat16),
        ],
        scratch_shapes=[
            pltpu.VMEM((D_MODEL, D_IN_PAD), jnp.bfloat16),
            pltpu.VMEM((tm, D_IN_PAD), jnp.float32),
            pltpu.VMEM((POOL_HALO + tm, POOL_WIDTH), jnp.float32),
            pltpu.VMEM((POOL_HALO + tm, POOL_WIDTH), jnp.float32),
            pltpu.VMEM((POOL_HALO + tm, POOL_WIDTH), jnp.float32),
            pltpu.VMEM((POOL_HALO + tm, POOL_WIDTH), jnp.float32),
        ],
        compiler_params=pltpu.CompilerParams(
            dimension_semantics=("arbitrary",), vmem_limit_bytes=VMEM_LIMIT_BYTES),
        name="mixer_in",
    )(x, w["pre_g"], w["w_in"], w["sgu_w"], w["sgu_bias"], w["ln_g"], w["ln_b"],
      w["pool_w"], w["pool_scale"], w["qn_g"], w["w_uq_t"], w["kvn_g"], w["w_k"], w["w_v_t"],
      cos_t, sin_t, cos_k, sin_k)


def _attention_kernel(qt_ref, qt_next_ref, gate_ref, k_hbm_ref, vt_hbm_ref, o_ref, k_ref, vt_ref, kv_sem, m_ref, acc_ref,
                      s0_ref, s1_ref, p0_ref, p1_ref, a0_ref, a1_ref, x0_ref, x1_ref):
    i = pl.program_id(1)
    last = pl.num_programs(1) - 1
    tk = ATT_TILE
    tq = ATT_Q_TILE
    n_heads = qt_ref.shape[0]
    n_tiles = (tq // tk) * (i + 1)
    s_refs = (s0_ref, s1_ref)
    p_refs = (p0_ref, p1_ref)
    a_refs = (a0_ref, a1_ref)
    x_refs = (x0_ref, x1_ref)
    acc_ref[...] = jnp.zeros(acc_ref.shape, jnp.float32)

    grp = pl.program_id(0)

    def kv_copies(first_tile):
        heads = pl.ds(grp * n_heads, n_heads)
        tiles = pl.ds(first_tile, tq // tk)
        return (pltpu.make_async_copy(k_hbm_ref.at[heads, tiles], k_ref.at[:, tiles], kv_sem.at[0]),
                pltpu.make_async_copy(vt_hbm_ref.at[heads, tiles], vt_ref.at[:, tiles], kv_sem.at[1]))

    @pl.when(i == 0)
    def _():
        for c in kv_copies(0):
            c.start()

    for c in kv_copies(n_tiles - tq // tk):
        c.wait()

    @pl.when(i < last)
    def _():
        for c in kv_copies(n_tiles):
            c.start()

    LATE = slice(tq // 2, tq)
    ALL = slice(0, tq)

    def score(slot, j, diag=None, q_ref=qt_ref):
        qs = LATE if diag == 1 else ALL
        for h in range(n_heads):
            s = jnp.dot(k_ref[h, j], q_ref[h, :, qs], preferred_element_type=jnp.float32)
            if diag is not None:
                key_chunk = (diag * tk + lax.broadcasted_iota(jnp.int32, s.shape, 0)) // CHUNK
                qry_chunk = (qs.start + lax.broadcasted_iota(jnp.int32, s.shape, 1)) // CHUNK
                s = jnp.where(key_chunk <= qry_chunk, s, NEG_INF)
            s_refs[slot][h, :, qs] = s
            x_refs[slot][h, :, qs] = jnp.max(s, axis=0, keepdims=True)

    def softmax(slot, first=False, qs=ALL):
        rc = SOFTMAX_ROWS
        for h in range(n_heads):
            m_old = jnp.full((1, tq), NEG_INF, jnp.float32) if first else m_ref[h, :, qs]
            m_new = jnp.maximum(m_old, x_refs[slot][h, :, qs])
            m_ref[h, :, qs] = m_new
            a_refs[slot][h, :, qs] = jnp.exp2(m_old - m_new)
            for r in range(0, tk, rc):
                d = s_refs[slot][h, r:r + rc, qs] - m_new
                p_refs[slot][h, r:r + rc, qs] = jnp.exp2(d.astype(jnp.bfloat16))

    def value(slot, j, qs=ALL):
        for h in range(n_heads):
            pv = jnp.dot(vt_ref[h, j], p_refs[slot][h, :, qs], preferred_element_type=jnp.float32)
            acc_ref[h, :, qs] = a_refs[slot][h, :, qs] * acc_ref[h, :, qs] + pv

    def pair(t0, masked):
        for u in range(2):
            score(u, t0 + u, diag=u if masked else None)
            softmax(1 - u)
            value(u, t0 + u - 2)

    def finish():
        for h in range(n_heads):
            o_t = acc_ref[h, 0:V_HEAD_DIM, :] * (1.0 / acc_ref[h, V_HEAD_DIM:V_HEAD_DIM + 1, :])
            cols = slice(h * V_HEAD_DIM, (h + 1) * V_HEAD_DIM)
            o_ref[:, cols] = (o_t.T * gate_ref[:, cols].astype(jnp.float32)).astype(o_ref.dtype)

    @pl.when(i == 0)
    def _():
        score(0, 0, diag=0)
        score(1, 1, diag=1)
        softmax(0, first=True)

    @pl.when(i > 0)
    def _():
        def pair_block(jj, carry):
            pair(2 + 2 * jj, masked=False)
            return carry

        lax.fori_loop(0, i - 1, pair_block, 0)
        pair(n_tiles - 2, masked=True)

    @pl.when(i < last)
    def _():
        score(0, 0, q_ref=qt_next_ref)
        softmax(1, qs=LATE)
        value(0, n_tiles - 2)
        score(1, 1, q_ref=qt_next_ref)
        value(1, n_tiles - 1, qs=LATE)
        finish()
        softmax(0, first=True)

    @pl.when(i == last)
    def _():
        softmax(1, qs=LATE)
        value(0, n_tiles - 2)
        value(1, n_tiles - 1, qs=LATE)
        finish()


def _attention(q_t, k, v_t, gate):
    heads, n_kv, tk, _ = k.shape
    seq = n_kv * tk
    tq = ATT_Q_TILE
    hp = ATT_HEADS_PER_STEP
    n_q = seq // tq
    return pl.pallas_call(
        _attention_kernel,
        grid=(heads // hp, n_q),
        in_specs=[
            pl.BlockSpec((hp, QK_PAD_DIM, tq), lambda g, i: (g, 0, i)),
            pl.BlockSpec((hp, QK_PAD_DIM, tq), lambda g, i: (g, 0, jnp.minimum(i + 1, n_q - 1))),
            pl.BlockSpec((tq, hp * V_HEAD_DIM), lambda g, i: (i, g)),
            pl.BlockSpec(memory_space=pl.ANY),
            pl.BlockSpec(memory_space=pl.ANY),
        ],
        out_specs=pl.BlockSpec((tq, hp * V_HEAD_DIM), lambda g, i: (i, g)),
        out_shape=jax.ShapeDtypeStruct((seq, heads * V_HEAD_DIM), jnp.bfloat16),
        scratch_shapes=[
            pltpu.VMEM((hp, n_kv, tk, QK_PAD_DIM), jnp.bfloat16),
            pltpu.VMEM((hp, n_kv, VT_ROWS, tk), jnp.bfloat16),
            pltpu.SemaphoreType.DMA((2,)),
            pltpu.VMEM((hp, 1, tq), jnp.float32),
            pltpu.VMEM((hp, VT_ROWS, tq), jnp.float32),
            pltpu.VMEM((hp, tk, tq), jnp.float32),
            pltpu.VMEM((hp, tk, tq), jnp.float32),
            pltpu.VMEM((hp, tk, tq), jnp.bfloat16),
            pltpu.VMEM((hp, tk, tq), jnp.bfloat16),
            pltpu.VMEM((hp, 1, tq), jnp.float32),
            pltpu.VMEM((hp, 1, tq), jnp.float32),
            pltpu.VMEM((hp, 1, tq), jnp.float32),
            pltpu.VMEM((hp, 1, tq), jnp.float32),
        ],
        compiler_params=pltpu.CompilerParams(
            dimension_semantics=("arbitrary", "arbitrary"), vmem_limit_bytes=ATT_VMEM_LIMIT_BYTES),
        name="attention",
    )(q_t, q_t, gate, k, v_t)


def _mixer_out_kernel(x_ref, yab_ref, yc_ref, w_out_ref, post_g_ref, out_ref, w_bf_ref):
    @pl.when(pl.program_id(0) == 0)
    def _():
        w_bf_ref[...] = w_out_ref[...].astype(jnp.bfloat16)

    y = _dot(jnp.concatenate([yab_ref[...], yc_ref[...]], axis=1), w_bf_ref[...])
    out_ref[...] = x_ref[...] + _rms(y, post_g_ref[...])


def _mixer_out(x, yab, yc, w, layer):
    seq = x.shape[0]
    tm = OUT_ROW_TILE
    row = lambda width: pl.BlockSpec((tm, width), lambda i: (i, 0))
    return pl.pallas_call(
        _mixer_out_kernel,
        grid=(seq // tm,),
        in_specs=[
            row(D_MODEL), row(SGU_WIDTH + POOL_WIDTH), row(MLA_WIDTH),
            pl.BlockSpec((None, D_MODEL, D_MODEL), lambda i: (layer, 0, 0)),
            pl.BlockSpec((None, 1, D_MODEL), lambda i: (layer, 0, 0)),
        ],
        out_specs=row(D_MODEL),
        out_shape=jax.ShapeDtypeStruct((seq, D_MODEL), jnp.float32),
        scratch_shapes=[pltpu.VMEM((D_MODEL, D_MODEL), jnp.bfloat16)],
        compiler_params=pltpu.CompilerParams(
            dimension_semantics=("arbitrary",), vmem_limit_bytes=VMEM_LIMIT_BYTES),
        name="mixer_out",
    )(x, yab, yc, w["w_out"], w["post_g"])


def _prep_weights(pre_norm_g, post_norm_g, w_in, sgu_w, sgu_b, sgu_ln_g, sgu_ln_b, pool_w, pool_scale,
                  q_norm_g, w_uq, kv_norm_g, w_ukv, w_out):
    bf = jnp.bfloat16
    depth = w_in.shape[0]
    w_ukv_r = w_ukv.reshape(depth, KV_LORA_RANK, MLA_HEADS, QK_NOPE_DIM + V_HEAD_DIM)
    w_k = w_ukv_r[..., :QK_NOPE_DIM].reshape(depth, KV_LORA_RANK, MLA_HEADS * QK_NOPE_DIM)
    w_v = w_ukv_r[..., QK_NOPE_DIM:].reshape(depth, KV_LORA_RANK, MLA_HEADS * V_HEAD_DIM)
    groups = len(POOL_WINDOWS)
    same_group = jnp.eye(groups, dtype=bool)[None, :, None, :, None]
    pool_bd = jnp.where(same_group, pool_w[:, :, :, None, :], 0.0).reshape(depth, POOL_WIDTH, POOL_WIDTH)
    return {
        "pre_g": pre_norm_g.reshape(depth, 1, D_MODEL),
        "post_g": post_norm_g.reshape(depth, 1, D_MODEL),
        "w_in": w_in,
        "sgu_w": sgu_w.transpose(0, 2, 1, 3).reshape(depth, SGU_BLOCK, SGU_HEADS * SGU_BLOCK),
        "sgu_bias": jnp.repeat(sgu_b.transpose(0, 2, 1), SGU_HEAD_DIM, axis=2),
        "ln_g": sgu_ln_g.reshape(depth, 1, SGU_WIDTH),
        "ln_b": sgu_ln_b.reshape(depth, 1, SGU_WIDTH),
        "pool_w": pool_bd.astype(bf),
        "pool_scale": pool_scale.reshape(depth, 1, POOL_WIDTH),
        "qn_g": q_norm_g.reshape(depth, 1, Q_LORA_RANK),
        "w_uq_t": w_uq.transpose(0, 2, 1).astype(bf),
        "kvn_g": kv_norm_g.reshape(depth, 1, KV_LORA_RANK),
        "w_k": w_k.astype(bf),
        "w_v_t": w_v.transpose(0, 2, 1).astype(bf),
        "w_out": w_out,
    }


def kernel(x, positions, pre_norm_g, post_norm_g, w_in, sgu_w, sgu_b, sgu_ln_g, sgu_ln_b, pool_w, pool_scale,
           q_norm_g, w_uq, kv_norm_g, w_ukv, w_out):
    bsz, seq, d_model = x.shape
    assert bsz == 1 and seq == SEQ and d_model == D_MODEL
    assert seq % ROW_TILE == 0 and ROW_TILE % ATT_TILE == 0 and seq % TABLE_TILE == 0
    assert seq % OUT_ROW_TILE == 0 and seq % ATT_Q_TILE == 0
    tables = _rope_tables(positions)
    xs = x.reshape(seq, d_model)
    w = _prep_weights(pre_norm_g, post_norm_g, w_in, sgu_w, sgu_b, sgu_ln_g, sgu_ln_b, pool_w, pool_scale,
                      q_norm_g, w_uq, kv_norm_g, w_ukv, w_out)
    for layer in range(pre_norm_g.shape[0]):
        yab, gate, q_t, k, v_t = _mixer_in(xs, w, layer, tables)
        yc = _attention(q_t, k, v_t, gate)
        xs = _mixer_out(xs, yab, yc, w, layer)
    return xs.reshape(bsz, seq, d_model)
```

```python
import math

import jax
import jax.numpy as jnp
from jax import lax
from jax.experimental import pallas as pl
from jax.experimental.pallas import tpu as pltpu

D_MODEL = 1024
SEQ = 16384
CHUNK = 64
EPS = 1e-6
NEG_INF = -1e30

SGU_WIDTH = 256
SGU_HEADS = 4
SGU_HEAD_DIM = SGU_WIDTH // SGU_HEADS
SGU_BLOCK = 128

POOL_WIDTH = 256
POOL_WINDOWS = (2, 4, 8, 16)
POOL_GROUP_DIM = POOL_WIDTH // len(POOL_WINDOWS)

MLA_WIDTH = 512
MLA_HEADS = 4
V_HEAD_DIM = MLA_WIDTH // MLA_HEADS
QK_NOPE_DIM = 128
QK_ROPE_DIM = 64
QK_HEAD_DIM = QK_NOPE_DIM + QK_ROPE_DIM
Q_LORA_RANK = 384
KV_LORA_RANK = 256
ROPE_BASE = 10000.0
ROPE_HALF = QK_ROPE_DIM // 2

LANES = 128
SUBLANES = 8
BF16_SUBLANES = 16
MXU_DIM = 256
VMEM_LIMIT_BYTES = 48 * 1024 * 1024
ATT_VMEM_LIMIT_BYTES = 54 * 1024 * 1024

ROW_TILE = 512
OUT_ROW_TILE = 1024
ATT_TILE = 512
ATT_Q_TILE = 2 * ATT_TILE
ATT_HEADS_PER_STEP = 2
SOFTMAX_ROWS = 64
TABLE_TILE = 2048
QK_PAD_DIM = MXU_DIM
VT_ROWS = V_HEAD_DIM + BF16_SUBLANES
POOL_HALO = 32

OFF_U = 0
OFF_V = OFF_U + SGU_WIDTH
OFF_GA = OFF_V + SGU_WIDTH
OFF_PIN = OFF_GA + SGU_WIDTH
OFF_PG = OFF_PIN + POOL_WIDTH
OFF_CQ = OFF_PG + POOL_WIDTH
OFF_CKV = OFF_CQ + Q_LORA_RANK
OFF_KR = OFF_CKV + KV_LORA_RANK
OFF_MG = OFF_KR + QK_ROPE_DIM
D_IN = OFF_MG + MLA_WIDTH
D_IN_PAD = -(-D_IN // LANES) * LANES

Q_PRESCALE = (QK_HEAD_DIM ** -0.5) * math.log2(math.e)


def _silu(x):
    return x * (1.0 / (1.0 + jnp.exp(-x)))


def _rms(x, g):
    return x * lax.rsqrt(jnp.mean(x * x, axis=-1, keepdims=True) + EPS) * g


def _dot(a, b):
    return jnp.dot(a, b, preferred_element_type=jnp.float32)


def _dot_nt(a, b):
    return lax.dot_general(a, b, (((1,), (1,)), ((), ())), preferred_element_type=jnp.float32)


def _rope_tables_kernel(pos_row_ref, invf_col_ref, cos_t_ref, sin_t_ref, cos_k_ref, sin_k_ref):
    ang_t = invf_col_ref[...] * pos_row_ref[...].astype(jnp.float32)
    cos_t = jnp.cos(ang_t)
    sin_t = jnp.sin(ang_t)
    cos_t_ref[...] = cos_t
    sin_t_ref[...] = sin_t
    pad = jnp.zeros((cos_t.shape[1], LANES - QK_ROPE_DIM), jnp.float32)
    cos_k_ref[...] = jnp.concatenate([cos_t.T, cos_t.T, pad], axis=1)
    sin_k_ref[...] = jnp.concatenate([-sin_t.T, sin_t.T, pad], axis=1)


def _rope_tables(positions):
    seq = positions.shape[-1]
    inv_freq = ROPE_BASE ** (-jnp.arange(0, QK_ROPE_DIM, 2, dtype=jnp.float32) / QK_ROPE_DIM)
    invf_col = inv_freq.reshape(ROPE_HALF, 1)
    pos_row = positions.reshape(1, seq)
    n = seq // TABLE_TILE
    return pl.pallas_call(
        _rope_tables_kernel,
        grid=(n,),
        in_specs=[
            pl.BlockSpec((1, TABLE_TILE), lambda i: (0, i)),
            pl.BlockSpec((ROPE_HALF, 1), lambda i: (0, 0)),
        ],
        out_specs=[
            pl.BlockSpec((ROPE_HALF, TABLE_TILE), lambda i: (0, i)),
            pl.BlockSpec((ROPE_HALF, TABLE_TILE), lambda i: (0, i)),
            pl.BlockSpec((TABLE_TILE, LANES), lambda i: (i, 0)),
            pl.BlockSpec((TABLE_TILE, LANES), lambda i: (i, 0)),
        ],
        out_shape=[
            jax.ShapeDtypeStruct((ROPE_HALF, seq), jnp.float32),
            jax.ShapeDtypeStruct((ROPE_HALF, seq), jnp.float32),
            jax.ShapeDtypeStruct((seq, LANES), jnp.float32),
            jax.ShapeDtypeStruct((seq, LANES), jnp.float32),
        ],
        compiler_params=pltpu.CompilerParams(dimension_semantics=("arbitrary",)),
        name="rope_tables",
    )(pos_row, invf_col)


def _mixer_in_kernel(x_ref, pre_g_ref, w_in_ref, sgu_w_ref, sgu_bias_ref, ln_g_ref, ln_b_ref,
                     pool_w_ref, pool_scale_ref, qn_g_ref, w_uq_t_ref, kvn_g_ref, w_k_ref, w_v_t_ref,
                     cos_t_ref, sin_t_ref, cos_k_ref, sin_k_ref,
                     yab_ref, gate_ref, qt_ref, k_ref, vt_ref,
                     z_ref, ext_ref, a2_ref, a4_ref, a8_ref):
    i = pl.program_id(0)
    tm = x_ref.shape[0]

    def mix():
        def proj(off, width):
            return z_ref[:, off:off + width]

        v = proj(OFF_V, SGU_WIDTH)
        mu = jnp.mean(v, axis=-1, keepdims=True)
        vc = v - mu
        var = jnp.mean(vc * vc, axis=-1, keepdims=True)
        vn = vc * lax.rsqrt(var + EPS) * ln_g_ref[...] + ln_b_ref[...]
        w_rows = lax.broadcasted_iota(jnp.int32, (SGU_BLOCK, SGU_HEADS * SGU_BLOCK), 0)
        w_cols = lax.broadcasted_iota(jnp.int32, (SGU_BLOCK, SGU_HEADS * SGU_BLOCK), 1)
        w_keep = ((w_cols % SGU_BLOCK) // CHUNK) <= (w_rows // CHUNK)
        w_cat = jnp.where(w_keep, sgu_w_ref[...], 0.0).astype(jnp.bfloat16)
        head_of_col = lax.broadcasted_iota(jnp.int32, (SGU_BLOCK, SGU_WIDTH), 1) // SGU_HEAD_DIM
        n_blk = tm // SGU_BLOCK
        v_stacks = []
        for r in range(n_blk):
            vb = vn[r * SGU_BLOCK:(r + 1) * SGU_BLOCK, :]
            v_stacks.append(jnp.concatenate(
                [jnp.where(head_of_col == h, vb, 0.0) for h in range(SGU_HEADS)], axis=0).astype(jnp.bfloat16))
        mixed_wide = _dot(w_cat, jnp.concatenate(v_stacks, axis=1))
        mixed = jnp.concatenate(
            [mixed_wide[:, r * SGU_WIDTH:(r + 1) * SGU_WIDTH] + sgu_bias_ref[...] for r in range(n_blk)], axis=0)
        ya = proj(OFF_U, SGU_WIDTH) * mixed * _silu(proj(OFF_GA, SGU_WIDTH))
        yab_ref[:, 0:SGU_WIDTH] = ya.astype(yab_ref.dtype)

        p = proj(OFF_PIN, POOL_WIDTH)

        @pl.when(i == 0)
        def _():
            ext_ref[0:POOL_HALO, :] = jnp.zeros((POOL_HALO, POOL_WIDTH), jnp.float32)

        ext_ref[POOL_HALO:POOL_HALO + tm, :] = p
        end = POOL_HALO + tm
        a2_ref[8:end, :] = ext_ref[8:end, :] + ext_ref[7:end - 1, :]
        a4_ref[16:end, :] = a2_ref[16:end, :] + a2_ref[14:end - 2, :]
        a8_ref[24:end, :] = a4_ref[24:end, :] + a4_ref[20:end - 4, :]
        a16 = a8_ref[POOL_HALO:end, :] + a8_ref[POOL_HALO - 8:end - 8, :]
        group = lax.broadcasted_iota(jnp.int32, (tm, POOL_WIDTH), 1) // POOL_GROUP_DIM
        sums = jnp.where(group == 0, a2_ref[POOL_HALO:end, :],
                         jnp.where(group == 1, a4_ref[POOL_HALO:end, :],
                                   jnp.where(group == 2, a8_ref[POOL_HALO:end, :], a16)))
        window = jnp.where(group == 0, POOL_WINDOWS[0],
                           jnp.where(group == 1, POOL_WINDOWS[1],
                                     jnp.where(group == 2, POOL_WINDOWS[2], POOL_WINDOWS[3])))
        t_glob = i * tm + lax.broadcasted_iota(jnp.int32, (tm, POOL_WIDTH), 0)
        count = jnp.minimum(t_glob + 1, window).astype(jnp.float32)
        pooled = sums / count - p
        ext_ref[0:POOL_HALO, :] = ext_ref[tm:tm + POOL_HALO, :]
        pool_mixed = _dot(pooled.astype(jnp.bfloat16), pool_w_ref[...])
        yb = pool_mixed * pool_scale_ref[...] * _silu(proj(OFF_PG, POOL_WIDTH))
        yab_ref[:, SGU_WIDTH:SGU_WIDTH + POOL_WIDTH] = yb.astype(yab_ref.dtype)

        gate_ref[...] = _silu(proj(OFF_MG, MLA_WIDTH)).astype(gate_ref.dtype)

        cqn = _rms(proj(OFF_CQ, Q_LORA_RANK), qn_g_ref[...]).astype(jnp.bfloat16)
        q_t = _dot_nt(w_uq_t_ref[...], cqn) * Q_PRESCALE
        cos_t = cos_t_ref[...]
        sin_t = sin_t_ref[...]
        for h in range(MLA_HEADS):
            base = h * QK_HEAD_DIM
            x1 = q_t[base + QK_NOPE_DIM:base + QK_NOPE_DIM + ROPE_HALF, :]
            x2 = q_t[base + QK_NOPE_DIM + ROPE_HALF:base + QK_HEAD_DIM, :]
            qt_ref[h, 0:QK_NOPE_DIM, :] = q_t[base:base + QK_NOPE_DIM, :].astype(qt_ref.dtype)
            qt_ref[h, QK_NOPE_DIM:QK_NOPE_DIM + ROPE_HALF, :] = (x1 * cos_t - x2 * sin_t).astype(qt_ref.dtype)
            qt_ref[h, QK_NOPE_DIM + ROPE_HALF:QK_HEAD_DIM, :] = (x2 * cos_t + x1 * sin_t).astype(qt_ref.dtype)
            qt_ref[h, QK_HEAD_DIM:QK_PAD_DIM, :] = jnp.zeros((QK_PAD_DIM - QK_HEAD_DIM, tm), qt_ref.dtype)

        ckvn = _rms(proj(OFF_CKV, KV_LORA_RANK), kvn_g_ref[...]).astype(jnp.bfloat16)
        k_nope = _dot(ckvn, w_k_ref[...])
        v_t = _dot_nt(w_v_t_ref[...], ckvn)
        grp = proj(OFF_KR, LANES)
        lane = lax.broadcasted_iota(jnp.int32, grp.shape, 1)
        partner = jnp.where(lane < ROPE_HALF,
                            pltpu.roll(grp, LANES - ROPE_HALF, axis=1),
                            pltpu.roll(grp, ROPE_HALF, axis=1))
        roped = grp * cos_k_ref[...] + partner * sin_k_ref[...]
        k_pe = jnp.where(lane < QK_ROPE_DIM, roped, 0.0).astype(k_ref.dtype)
        n_sub = tm // ATT_TILE
        for h in range(MLA_HEADS):
            for c in range(n_sub):
                rows = slice(c * ATT_TILE, (c + 1) * ATT_TILE)
                k_ref[h, c, :, 0:QK_NOPE_DIM] = k_nope[rows, h * QK_NOPE_DIM:(h + 1) * QK_NOPE_DIM].astype(k_ref.dtype)
                k_ref[h, c, :, QK_NOPE_DIM:QK_PAD_DIM] = k_pe[rows, :]
                vt_ref[h, c, 0:V_HEAD_DIM, :] = v_t[h * V_HEAD_DIM:(h + 1) * V_HEAD_DIM, rows].astype(vt_ref.dtype)
                ones_row = lax.broadcasted_iota(jnp.int32, (VT_ROWS - V_HEAD_DIM, ATT_TILE), 0) == 0
                vt_ref[h, c, V_HEAD_DIM:VT_ROWS, :] = jnp.where(ones_row, 1.0, 0.0).astype(vt_ref.dtype)

    z_ref[...] = _dot(_rms(x_ref[...], pre_g_ref[...]).astype(jnp.bfloat16), w_in_ref[...])
    mix()


def _mixer_in(x, w, layer, tables):
    seq = x.shape[0]
    tm = ROW_TILE
    n = seq // tm
    n_sub = tm // ATT_TILE
    cos_t, sin_t, cos_k, sin_k = tables

    def const(shape):
        return pl.BlockSpec((None,) + shape, lambda i: (layer,) + (0,) * len(shape))

    return pl.pallas_call(
        _mixer_in_kernel,
        grid=(n,),
        in_specs=[
            pl.BlockSpec((tm, D_MODEL), lambda i: (i, 0)),
            const((1, D_MODEL)),
            pl.BlockSpec((None, D_MODEL, D_IN_PAD), lambda i: (layer, 0, 0), pipeline_mode=pl.Buffered(1)),
            const((SGU_BLOCK, SGU_HEADS * SGU_BLOCK)),
            const((SGU_BLOCK, SGU_WIDTH)),
            const((1, SGU_WIDTH)),
            const((1, SGU_WIDTH)),
            const((POOL_WIDTH, POOL_WIDTH)),
            const((1, POOL_WIDTH)),
            const((1, Q_LORA_RANK)),
            const((MLA_HEADS * QK_HEAD_DIM, Q_LORA_RANK)),
            const((1, KV_LORA_RANK)),
            const((KV_LORA_RANK, MLA_HEADS * QK_NOPE_DIM)),
            const((MLA_HEADS * V_HEAD_DIM, KV_LORA_RANK)),
            pl.BlockSpec((ROPE_HALF, tm), lambda i: (0, i)),
            pl.BlockSpec((ROPE_HALF, tm), lambda i: (0, i)),
            pl.BlockSpec((tm, LANES), lambda i: (i, 0)),
            pl.BlockSpec((tm, LANES), lambda i: (i, 0)),
        ],
        out_specs=[
            pl.BlockSpec((tm, SGU_WIDTH + POOL_WIDTH), lambda i: (i, 0)),
            pl.BlockSpec((tm, MLA_WIDTH), lambda i: (i, 0)),
            pl.BlockSpec((MLA_HEADS, QK_PAD_DIM, tm), lambda i: (0, 0, i)),
            pl.BlockSpec((MLA_HEADS, n_sub, ATT_TILE, QK_PAD_DIM), lambda i: (0, i, 0, 0)),
            pl.BlockSpec((MLA_HEADS, n_sub, VT_ROWS, ATT_TILE), lambda i: (0, i, 0, 0)),
        ],
        out_shape=[
            jax.ShapeDtypeStruct((seq, SGU_WIDTH + POOL_WIDTH), jnp.bfloat16),
            jax.ShapeDtypeStruct((seq, MLA_WIDTH), jnp.bfloat16),
            jax.ShapeDtypeStruct((MLA_HEADS, QK_PAD_DIM, seq), jnp.bfloat16),
            jax.ShapeDtypeStruct((MLA_HEADS, seq // ATT_TILE, ATT_TILE, QK_PAD_DIM), jnp.bfloat16),
            jax.ShapeDtypeStruct((MLA_HEADS, seq // ATT_TILE, VT_ROWS, ATT_TILE), jnp.bfloat16),
        ],
        scratch_shapes=[
            pltpu.VMEM((tm, D_IN_PAD), jnp.float32),
            pltpu.VMEM((POOL_HALO + tm, POOL_WIDTH), jnp.float32),
            pltpu.VMEM((POOL_HALO + tm, POOL_WIDTH), jnp.float32),
            pltpu.VMEM((POOL_HALO + tm, POOL_WIDTH), jnp.float32),
            pltpu.VMEM((POOL_HALO + tm, POOL_WIDTH), jnp.float32),
        ],
        compiler_params=pltpu.CompilerParams(
            dimension_semantics=("arbitrary",), vmem_limit_bytes=VMEM_LIMIT_BYTES),
        name="mixer_in",
    )(x, w["pre_g"], w["w_in"], w["sgu_w"], w["sgu_bias"], w["ln_g"], w["ln_b"],
      w["pool_w"], w["pool_scale"], w["qn_g"], w["w_uq_t"], w["kvn_g"], w["w_k"], w["w_v_t"],
      cos_t, sin_t, cos_k, sin_k)


def _attention_kernel(qt_ref, qt_next_ref, k_hbm_ref, vt_hbm_ref, o_ref, k_ref, vt_ref, kv_sem, m_ref, acc_ref,
                      s0_ref, s1_ref, p0_ref, p1_ref, a0_ref, a1_ref, x0_ref, x1_ref):
    i = pl.program_id(1)
    last = pl.num_programs(1) - 1
    tk = ATT_TILE
    tq = ATT_Q_TILE
    n_heads = qt_ref.shape[0]
    n_tiles = (tq // tk) * (i + 1)
    s_refs = (s0_ref, s1_ref)
    p_refs = (p0_ref, p1_ref)
    a_refs = (a0_ref, a1_ref)
    x_refs = (x0_ref, x1_ref)
    acc_ref[...] = jnp.zeros(acc_ref.shape, jnp.float32)

    grp = pl.program_id(0)

    def kv_copies(first_tile):
        heads = pl.ds(grp * n_heads, n_heads)
        tiles = pl.ds(first_tile, tq // tk)
        return (pltpu.make_async_copy(k_hbm_ref.at[heads, tiles], k_ref.at[:, tiles], kv_sem.at[0]),
                pltpu.make_async_copy(vt_hbm_ref.at[heads, tiles], vt_ref.at[:, tiles], kv_sem.at[1]))

    @pl.when(i == 0)
    def _():
        for c in kv_copies(0):
            c.start()

    for c in kv_copies(n_tiles - tq // tk):
        c.wait()

    @pl.when(i < last)
    def _():
        for c in kv_copies(n_tiles):
            c.start()

    LATE = slice(tq // 2, tq)
    ALL = slice(0, tq)

    def score(slot, j, diag=None, q_ref=qt_ref):
        qs = LATE if diag == 1 else ALL
        for h in range(n_heads):
            s = jnp.dot(k_ref[h, j], q_ref[h, :, qs], preferred_element_type=jnp.float32)
            if diag is not None:
                key_chunk = (diag * tk + lax.broadcasted_iota(jnp.int32, s.shape, 0)) // CHUNK
                qry_chunk = (qs.start + lax.broadcasted_iota(jnp.int32, s.shape, 1)) // CHUNK
                s = jnp.where(key_chunk <= qry_chunk, s, NEG_INF)
            s_refs[slot][h, :, qs] = s
            x_refs[slot][h, :, qs] = jnp.max(s, axis=0, keepdims=True)

    def softmax(slot, first=False, qs=ALL):
        rc = SOFTMAX_ROWS
        for h in range(n_heads):
            m_old = jnp.full((1, tq), NEG_INF, jnp.float32) if first else m_ref[h, :, qs]
            m_new = jnp.maximum(m_old, x_refs[slot][h, :, qs])
            m_ref[h, :, qs] = m_new
            a_refs[slot][h, :, qs] = jnp.exp2(m_old - m_new)
            for r in range(0, tk, rc):
                d = s_refs[slot][h, r:r + rc, qs] - m_new
                p_refs[slot][h, r:r + rc, qs] = jnp.exp2(d.astype(jnp.bfloat16))

    def value(slot, j, qs=ALL):
        for h in range(n_heads):
            pv = jnp.dot(vt_ref[h, j], p_refs[slot][h, :, qs], preferred_element_type=jnp.float32)
            acc_ref[h, :, qs] = a_refs[slot][h, :, qs] * acc_ref[h, :, qs] + pv

    def pair(t0, masked):
        for u in range(2):
            score(u, t0 + u, diag=u if masked else None)
            softmax(1 - u)
            value(u, t0 + u - 2)

    def finish():
        for h in range(n_heads):
            o_t = acc_ref[h, 0:V_HEAD_DIM, :] * (1.0 / acc_ref[h, V_HEAD_DIM:V_HEAD_DIM + 1, :])
            o_ref[:, h * V_HEAD_DIM:(h + 1) * V_HEAD_DIM] = o_t.T.astype(o_ref.dtype)

    @pl.when(i == 0)
    def _():
        score(0, 0, diag=0)
        score(1, 1, diag=1)
        softmax(0, first=True)

    @pl.when(i > 0)
    def _():
        def pair_block(jj, carry):
            pair(2 + 2 * jj, masked=False)
            return carry

        lax.fori_loop(0, i - 1, pair_block, 0)
        pair(n_tiles - 2, masked=True)

    @pl.when(i < last)
    def _():
        score(0, 0, q_ref=qt_next_ref)
        softmax(1, qs=LATE)
        value(0, n_tiles - 2)
        score(1, 1, q_ref=qt_next_ref)
        value(1, n_tiles - 1, qs=LATE)
        finish()
        softmax(0, first=True)

    @pl.when(i == last)
    def _():
        softmax(1, qs=LATE)
        value(0, n_tiles - 2)
        value(1, n_tiles - 1, qs=LATE)
        finish()


def _attention(q_t, k, v_t):
    heads, n_kv, tk, _ = k.shape
    seq = n_kv * tk
    tq = ATT_Q_TILE
    hp = ATT_HEADS_PER_STEP
    n_q = seq // tq
    return pl.pallas_call(
        _attention_kernel,
        grid=(heads // hp, n_q),
        in_specs=[
            pl.BlockSpec((hp, QK_PAD_DIM, tq), lambda g, i: (g, 0, i)),
            pl.BlockSpec((hp, QK_PAD_DIM, tq), lambda g, i: (g, 0, jnp.minimum(i + 1, n_q - 1))),
            pl.BlockSpec(memory_space=pl.ANY),
            pl.BlockSpec(memory_space=pl.ANY),
        ],
        out_specs=pl.BlockSpec((tq, hp * V_HEAD_DIM), lambda g, i: (i, g)),
        out_shape=jax.ShapeDtypeStruct((seq, heads * V_HEAD_DIM), jnp.bfloat16),
        scratch_shapes=[
            pltpu.VMEM((hp, n_kv, tk, QK_PAD_DIM), jnp.bfloat16),
            pltpu.VMEM((hp, n_kv, VT_ROWS, tk), jnp.bfloat16),
            pltpu.SemaphoreType.DMA((2,)),
            pltpu.VMEM((hp, 1, tq), jnp.float32),
            pltpu.VMEM((hp, VT_ROWS, tq), jnp.float32),
            pltpu.VMEM((hp, tk, tq), jnp.float32),
            pltpu.VMEM((hp, tk, tq), jnp.float32),
            pltpu.VMEM((hp, tk, tq), jnp.bfloat16),
            pltpu.VMEM((hp, tk, tq), jnp.bfloat16),
            pltpu.VMEM((hp, 1, tq), jnp.float32),
            pltpu.VMEM((hp, 1, tq), jnp.float32),
            pltpu.VMEM((hp, 1, tq), jnp.float32),
            pltpu.VMEM((hp, 1, tq), jnp.float32),
        ],
        compiler_params=pltpu.CompilerParams(
            dimension_semantics=("arbitrary", "arbitrary"), vmem_limit_bytes=ATT_VMEM_LIMIT_BYTES),
        name="attention",
    )(q_t, q_t, k, v_t)


def _mixer_out_kernel(x_ref, yab_ref, o_ref, gate_ref, w_out_ref, post_g_ref, out_ref, w_bf_ref):
    @pl.when(pl.program_id(0) == 0)
    def _():
        w_bf_ref[...] = w_out_ref[...].astype(jnp.bfloat16)

    yc = (o_ref[...].astype(jnp.float32) * gate_ref[...].astype(jnp.float32)).astype(jnp.bfloat16)
    y = _dot(jnp.concatenate([yab_ref[...], yc], axis=1), w_bf_ref[...])
    out_ref[...] = x_ref[...] + _rms(y, post_g_ref[...])


def _mixer_out(x, yab, o, gate, w, layer):
    seq = x.shape[0]
    tm = OUT_ROW_TILE
    row = lambda width: pl.BlockSpec((tm, width), lambda i: (i, 0))
    return pl.pallas_call(
        _mixer_out_kernel,
        grid=(seq // tm,),
        in_specs=[
            row(D_MODEL), row(SGU_WIDTH + POOL_WIDTH), row(MLA_WIDTH), row(MLA_WIDTH),
            pl.BlockSpec((None, D_MODEL, D_MODEL), lambda i: (layer, 0, 0)),
            pl.BlockSpec((None, 1, D_MODEL), lambda i: (layer, 0, 0)),
        ],
        out_specs=row(D_MODEL),
        out_shape=jax.ShapeDtypeStruct((seq, D_MODEL), jnp.float32),
        scratch_shapes=[pltpu.VMEM((D_MODEL, D_MODEL), jnp.bfloat16)],
        compiler_params=pltpu.CompilerParams(
            dimension_semantics=("arbitrary",), vmem_limit_bytes=VMEM_LIMIT_BYTES),
        name="mixer_out",
    )(x, yab, o, gate, w["w_out"], w["post_g"])


def _prep_weights(pre_norm_g, post_norm_g, w_in, sgu_w, sgu_b, sgu_ln_g, sgu_ln_b, pool_w, pool_scale,
                  q_norm_g, w_uq, kv_norm_g, w_ukv, w_out):
    bf = jnp.bfloat16
    depth = w_in.shape[0]
    w_ukv_r = w_ukv.reshape(depth, KV_LORA_RANK, MLA_HEADS, QK_NOPE_DIM + V_HEAD_DIM)
    w_k = w_ukv_r[..., :QK_NOPE_DIM].reshape(depth, KV_LORA_RANK, MLA_HEADS * QK_NOPE_DIM)
    w_v = w_ukv_r[..., QK_NOPE_DIM:].reshape(depth, KV_LORA_RANK, MLA_HEADS * V_HEAD_DIM)
    groups = len(POOL_WINDOWS)
    same_group = jnp.eye(groups, dtype=bool)[None, :, None, :, None]
    pool_bd = jnp.where(same_group, pool_w[:, :, :, None, :], 0.0).reshape(depth, POOL_WIDTH, POOL_WIDTH)
    return {
        "pre_g": pre_norm_g.reshape(depth, 1, D_MODEL),
        "post_g": post_norm_g.reshape(depth, 1, D_MODEL),
        "w_in": jnp.pad(w_in.astype(bf), ((0, 0), (0, 0), (0, D_IN_PAD - D_IN))),
        "sgu_w": sgu_w.transpose(0, 2, 1, 3).reshape(depth, SGU_BLOCK, SGU_HEADS * SGU_BLOCK),
        "sgu_bias": jnp.repeat(sgu_b.transpose(0, 2, 1), SGU_HEAD_DIM, axis=2),
        "ln_g": sgu_ln_g.reshape(depth, 1, SGU_WIDTH),
        "ln_b": sgu_ln_b.reshape(depth, 1, SGU_WIDTH),
        "pool_w": pool_bd.astype(bf),
        "pool_scale": pool_scale.reshape(depth, 1, POOL_WIDTH),
        "qn_g": q_norm_g.reshape(depth, 1, Q_LORA_RANK),
        "w_uq_t": w_uq.transpose(0, 2, 1).astype(bf),
        "kvn_g": kv_norm_g.reshape(depth, 1, KV_LORA_RANK),
        "w_k": w_k.astype(bf),
        "w_v_t": w_v.transpose(0, 2, 1).astype(bf),
        "w_out": w_out,
    }


def kernel(x, positions, pre_norm_g, post_norm_g, w_in, sgu_w, sgu_b, sgu_ln_g, sgu_ln_b, pool_w, pool_scale,
           q_norm_g, w_uq, kv_norm_g, w_ukv, w_out):
    bsz, seq, d_model = x.shape
    assert bsz == 1 and seq == SEQ and d_model == D_MODEL
    assert seq % ROW_TILE == 0 and ROW_TILE % ATT_TILE == 0 and seq % TABLE_TILE == 0
    assert seq % OUT_ROW_TILE == 0 and seq % ATT_Q_TILE == 0
    tables = _rope_tables(positions)
    xs = x.reshape(seq, d_model)
    w = _prep_weights(pre_norm_g, post_norm_g, w_in, sgu_w, sgu_b, sgu_ln_g, sgu_ln_b, pool_w, pool_scale,
                      q_norm_g, w_uq, kv_norm_g, w_ukv, w_out)
    for layer in range(pre_norm_g.shape[0]):
        yab, gate, q_t, k, v_t = _mixer_in(xs, w, layer, tables)
        o = _attention(q_t, k, v_t)
        xs = _mixer_out(xs, yab, o, gate, w, layer)
    return xs.reshape(bsz, seq, d_model)
```

```python
import math

import jax
import jax.numpy as jnp
from jax import lax
from jax.experimental import pallas as pl
from jax.experimental.pallas import tpu as pltpu

D_MODEL = 1024
SEQ = 16384
CHUNK = 64
EPS = 1e-6
NEG_INF = -1e30

SGU_WIDTH = 256
SGU_HEADS = 4
SGU_HEAD_DIM = SGU_WIDTH // SGU_HEADS
SGU_BLOCK = 128

POOL_WIDTH = 256
POOL_WINDOWS = (2, 4, 8, 16)
POOL_GROUP_DIM = POOL_WIDTH // len(POOL_WINDOWS)

MLA_WIDTH = 512
MLA_HEADS = 4
V_HEAD_DIM = MLA_WIDTH // MLA_HEADS
QK_NOPE_DIM = 128
QK_ROPE_DIM = 64
QK_HEAD_DIM = QK_NOPE_DIM + QK_ROPE_DIM
Q_LORA_RANK = 384
KV_LORA_RANK = 256
ROPE_BASE = 10000.0
ROPE_HALF = QK_ROPE_DIM // 2

LANES = 128
SUBLANES = 8
BF16_SUBLANES = 16
MXU_DIM = 256
VMEM_LIMIT_BYTES = 48 * 1024 * 1024
ATT_VMEM_LIMIT_BYTES = 54 * 1024 * 1024

ROW_TILE = 512
OUT_ROW_TILE = 1024
ATT_TILE = 512
ATT_Q_TILE = 2 * ATT_TILE
ATT_HEADS_PER_STEP = 2
SOFTMAX_ROWS = 64
TABLE_TILE = 2048
QK_PAD_DIM = MXU_DIM
VT_ROWS = V_HEAD_DIM + BF16_SUBLANES
POOL_HALO = 32

OFF_U = 0
OFF_V = OFF_U + SGU_WIDTH
OFF_GA = OFF_V + SGU_WIDTH
OFF_PIN = OFF_GA + SGU_WIDTH
OFF_PG = OFF_PIN + POOL_WIDTH
OFF_CQ = OFF_PG + POOL_WIDTH
OFF_CKV = OFF_CQ + Q_LORA_RANK
OFF_KR = OFF_CKV + KV_LORA_RANK
OFF_MG = OFF_KR + QK_ROPE_DIM
D_IN = OFF_MG + MLA_WIDTH
D_IN_PAD = -(-D_IN // LANES) * LANES

Q_PRESCALE = (QK_HEAD_DIM ** -0.5) * math.log2(math.e)


def _silu(x):
    return x * (1.0 / (1.0 + jnp.exp(-x)))


def _rms(x, g):
    return x * lax.rsqrt(jnp.mean(x * x, axis=-1, keepdims=True) + EPS) * g


def _dot(a, b):
    return jnp.dot(a, b, preferred_element_type=jnp.float32)


def _dot_nt(a, b):
    return lax.dot_general(a, b, (((1,), (1,)), ((), ())), preferred_element_type=jnp.float32)


def _rope_tables_kernel(pos_row_ref, invf_col_ref, cos_t_ref, sin_t_ref, cos_k_ref, sin_k_ref):
    ang_t = invf_col_ref[...] * pos_row_ref[...].astype(jnp.float32)
    cos_t = jnp.cos(ang_t)
    sin_t = jnp.sin(ang_t)
    cos_t_ref[...] = cos_t
    sin_t_ref[...] = sin_t
    pad = jnp.zeros((cos_t.shape[1], LANES - QK_ROPE_DIM), jnp.float32)
    cos_k_ref[...] = jnp.concatenate([cos_t.T, cos_t.T, pad], axis=1)
    sin_k_ref[...] = jnp.concatenate([-sin_t.T, sin_t.T, pad], axis=1)


def _rope_tables(positions):
    seq = positions.shape[-1]
    inv_freq = ROPE_BASE ** (-jnp.arange(0, QK_ROPE_DIM, 2, dtype=jnp.float32) / QK_ROPE_DIM)
    invf_col = inv_freq.reshape(ROPE_HALF, 1)
    pos_row = positions.reshape(1, seq)
    n = seq // TABLE_TILE
    return pl.pallas_call(
        _rope_tables_kernel,
        grid=(n,),
        in_specs=[
            pl.BlockSpec((1, TABLE_TILE), lambda i: (0, i)),
            pl.BlockSpec((ROPE_HALF, 1), lambda i: (0, 0)),
        ],
        out_specs=[
            pl.BlockSpec((ROPE_HALF, TABLE_TILE), lambda i: (0, i)),
            pl.BlockSpec((ROPE_HALF, TABLE_TILE), lambda i: (0, i)),
            pl.BlockSpec((TABLE_TILE, LANES), lambda i: (i, 0)),
            pl.BlockSpec((TABLE_TILE, LANES), lambda i: (i, 0)),
        ],
        out_shape=[
            jax.ShapeDtypeStruct((ROPE_HALF, seq), jnp.float32),
            jax.ShapeDtypeStruct((ROPE_HALF, seq), jnp.float32),
            jax.ShapeDtypeStruct((seq, LANES), jnp.float32),
            jax.ShapeDtypeStruct((seq, LANES), jnp.float32),
        ],
        compiler_params=pltpu.CompilerParams(dimension_semantics=("arbitrary",)),
        name="rope_tables",
    )(pos_row, invf_col)


def _mixer_in_kernel(x_ref, pre_g_ref, w_in_ref, sgu_w_ref, sgu_bias_ref, ln_g_ref, ln_b_ref,
                     pool_w_ref, pool_scale_ref, qn_g_ref, w_uq_t_ref, kvn_g_ref, w_k_ref, w_v_t_ref,
                     cos_t_ref, sin_t_ref, cos_k_ref, sin_k_ref,
                     yab_ref, gate_ref, qt_ref, k_ref, vt_ref,
                     w_bf_ref, z_ref, ext_ref, a2_ref, a4_ref, a8_ref):
    i = pl.program_id(0)
    tm = x_ref.shape[0]

    def mix():
        def proj(off, width):
            return z_ref[:, off:off + width]

        v = proj(OFF_V, SGU_WIDTH)
        mu = jnp.mean(v, axis=-1, keepdims=True)
        vc = v - mu
        var = jnp.mean(vc * vc, axis=-1, keepdims=True)
        vn = vc * lax.rsqrt(var + EPS) * ln_g_ref[...] + ln_b_ref[...]
        w_rows = lax.broadcasted_iota(jnp.int32, (SGU_BLOCK, SGU_HEADS * SGU_BLOCK), 0)
        w_cols = lax.broadcasted_iota(jnp.int32, (SGU_BLOCK, SGU_HEADS * SGU_BLOCK), 1)
        w_keep = ((w_cols % SGU_BLOCK) // CHUNK) <= (w_rows // CHUNK)
        w_cat = jnp.where(w_keep, sgu_w_ref[...], 0.0).astype(jnp.bfloat16)
        head_of_col = lax.broadcasted_iota(jnp.int32, (SGU_BLOCK, SGU_WIDTH), 1) // SGU_HEAD_DIM
        n_blk = tm // SGU_BLOCK
        v_stacks = []
        for r in range(n_blk):
            vb = vn[r * SGU_BLOCK:(r + 1) * SGU_BLOCK, :]
            v_stacks.append(jnp.concatenate(
                [jnp.where(head_of_col == h, vb, 0.0) for h in range(SGU_HEADS)], axis=0).astype(jnp.bfloat16))
        mixed_wide = _dot(w_cat, jnp.concatenate(v_stacks, axis=1))
        mixed = jnp.concatenate(
            [mixed_wide[:, r * SGU_WIDTH:(r + 1) * SGU_WIDTH] + sgu_bias_ref[...] for r in range(n_blk)], axis=0)
        ya = proj(OFF_U, SGU_WIDTH) * mixed * _silu(proj(OFF_GA, SGU_WIDTH))
        yab_ref[:, 0:SGU_WIDTH] = ya.astype(yab_ref.dtype)

        p = proj(OFF_PIN, POOL_WIDTH)

        @pl.when(i == 0)
        def _():
            ext_ref[0:POOL_HALO, :] = jnp.zeros((POOL_HALO, POOL_WIDTH), jnp.float32)

        ext_ref[POOL_HALO:POOL_HALO + tm, :] = p
        end = POOL_HALO + tm
        a2_ref[8:end, :] = ext_ref[8:end, :] + ext_ref[7:end - 1, :]
        a4_ref[16:end, :] = a2_ref[16:end, :] + a2_ref[14:end - 2, :]
        a8_ref[24:end, :] = a4_ref[24:end, :] + a4_ref[20:end - 4, :]
        a16 = a8_ref[POOL_HALO:end, :] + a8_ref[POOL_HALO - 8:end - 8, :]
        group = lax.broadcasted_iota(jnp.int32, (tm, POOL_WIDTH), 1) // POOL_GROUP_DIM
        sums = jnp.where(group == 0, a2_ref[POOL_HALO:end, :],
                         jnp.where(group == 1, a4_ref[POOL_HALO:end, :],
                                   jnp.where(group == 2, a8_ref[POOL_HALO:end, :], a16)))
        window = jnp.where(group == 0, POOL_WINDOWS[0],
                           jnp.where(group == 1, POOL_WINDOWS[1],
                                     jnp.where(group == 2, POOL_WINDOWS[2], POOL_WINDOWS[3])))
        t_glob = i * tm + lax.broadcasted_iota(jnp.int32, (tm, POOL_WIDTH), 0)
        count = jnp.minimum(t_glob + 1, window).astype(jnp.float32)
        pooled = sums / count - p
        ext_ref[0:POOL_HALO, :] = ext_ref[tm:tm + POOL_HALO, :]
        pool_mixed = _dot(pooled.astype(jnp.bfloat16), pool_w_ref[...])
        yb = pool_mixed * pool_scale_ref[...] * _silu(proj(OFF_PG, POOL_WIDTH))
        yab_ref[:, SGU_WIDTH:SGU_WIDTH + POOL_WIDTH] = yb.astype(yab_ref.dtype)

        gate_ref[...] = _silu(proj(OFF_MG, MLA_WIDTH)).astype(gate_ref.dtype)

        cqn = _rms(proj(OFF_CQ, Q_LORA_RANK), qn_g_ref[...]).astype(jnp.bfloat16)
        q_t = _dot_nt(w_uq_t_ref[...], cqn) * Q_PRESCALE
        cos_t = cos_t_ref[...]
        sin_t = sin_t_ref[...]
        for h in range(MLA_HEADS):
            base = h * QK_HEAD_DIM
            x1 = q_t[base + QK_NOPE_DIM:base + QK_NOPE_DIM + ROPE_HALF, :]
            x2 = q_t[base + QK_NOPE_DIM + ROPE_HALF:base + QK_HEAD_DIM, :]
            qt_ref[h, 0:QK_NOPE_DIM, :] = q_t[base:base + QK_NOPE_DIM, :].astype(qt_ref.dtype)
            qt_ref[h, QK_NOPE_DIM:QK_NOPE_DIM + ROPE_HALF, :] = (x1 * cos_t - x2 * sin_t).astype(qt_ref.dtype)
            qt_ref[h, QK_NOPE_DIM + ROPE_HALF:QK_HEAD_DIM, :] = (x2 * cos_t + x1 * sin_t).astype(qt_ref.dtype)
            qt_ref[h, QK_HEAD_DIM:QK_PAD_DIM, :] = jnp.zeros((QK_PAD_DIM - QK_HEAD_DIM, tm), qt_ref.dtype)

        ckvn = _rms(proj(OFF_CKV, KV_LORA_RANK), kvn_g_ref[...]).astype(jnp.bfloat16)
        k_nope = _dot(ckvn, w_k_ref[...])
        v_t = _dot_nt(w_v_t_ref[...], ckvn)
        grp = proj(OFF_KR, LANES)
        lane = lax.broadcasted_iota(jnp.int32, grp.shape, 1)
        partner = jnp.where(lane < ROPE_HALF,
                            pltpu.roll(grp, LANES - ROPE_HALF, axis=1),
                            pltpu.roll(grp, ROPE_HALF, axis=1))
        roped = grp * cos_k_ref[...] + partner * sin_k_ref[...]
        k_pe = jnp.where(lane < QK_ROPE_DIM, roped, 0.0).astype(k_ref.dtype)
        n_sub = tm // ATT_TILE
        for h in range(MLA_HEADS):
            for c in range(n_sub):
                rows = slice(c * ATT_TILE, (c + 1) * ATT_TILE)
                k_ref[h, c, :, 0:QK_NOPE_DIM] = k_nope[rows, h * QK_NOPE_DIM:(h + 1) * QK_NOPE_DIM].astype(k_ref.dtype)
                k_ref[h, c, :, QK_NOPE_DIM:QK_PAD_DIM] = k_pe[rows, :]
                vt_ref[h, c, 0:V_HEAD_DIM, :] = v_t[h * V_HEAD_DIM:(h + 1) * V_HEAD_DIM, rows].astype(vt_ref.dtype)
                ones_row = lax.broadcasted_iota(jnp.int32, (VT_ROWS - V_HEAD_DIM, ATT_TILE), 0) == 0
                vt_ref[h, c, V_HEAD_DIM:VT_ROWS, :] = jnp.where(ones_row, 1.0, 0.0).astype(vt_ref.dtype)

    @pl.when(i == 0)
    def _():
        aligned = (D_IN // LANES) * LANES
        w_bf_ref[:, 0:aligned] = w_in_ref[:, 0:aligned]
        w_bf_ref[:, aligned:D_IN_PAD] = jnp.concatenate(
            [w_in_ref[:, aligned:D_IN], jnp.zeros((D_MODEL, D_IN_PAD - D_IN), jnp.bfloat16)], axis=1)

    z_ref[...] = _dot(_rms(x_ref[...], pre_g_ref[...]).astype(jnp.bfloat16), w_bf_ref[...])
    mix()


def _mixer_in(x, w, layer, tables):
    seq = x.shape[0]
    tm = ROW_TILE
    n = seq // tm
    n_sub = tm // ATT_TILE
    cos_t, sin_t, cos_k, sin_k = tables

    def const(shape):
        return pl.BlockSpec((None,) + shape, lambda i: (layer,) + (0,) * len(shape))

    return pl.pallas_call(
        _mixer_in_kernel,
        grid=(n,),
        in_specs=[
            pl.BlockSpec((tm, D_MODEL), lambda i: (i, 0)),
            const((1, D_MODEL)),
            pl.BlockSpec((None, D_MODEL, D_IN), lambda i: (layer, 0, 0), pipeline_mode=pl.Buffered(1)),
            const((SGU_BLOCK, SGU_HEADS * SGU_BLOCK)),
            const((SGU_BLOCK, SGU_WIDTH)),
            const((1, SGU_WIDTH)),
            const((1, SGU_WIDTH)),
            const((POOL_WIDTH, POOL_WIDTH)),
            const((1, POOL_WIDTH)),
            const((1, Q_LORA_RANK)),
            const((MLA_HEADS * QK_HEAD_DIM, Q_LORA_RANK)),
            const((1, KV_LORA_RANK)),
            const((KV_LORA_RANK, MLA_HEADS * QK_NOPE_DIM)),
            const((MLA_HEADS * V_HEAD_DIM, KV_LORA_RANK)),
            pl.BlockSpec((ROPE_HALF, tm), lambda i: (0, i)),
            pl.BlockSpec((ROPE_HALF, tm), lambda i: (0, i)),
            pl.BlockSpec((tm, LANES), lambda i: (i, 0)),
            pl.BlockSpec((tm, LANES), lambda i: (i, 0)),
        ],
        out_specs=[
            pl.BlockSpec((tm, SGU_WIDTH + POOL_WIDTH), lambda i: (i, 0)),
            pl.BlockSpec((tm, MLA_WIDTH), lambda i: (i, 0)),
            pl.BlockSpec((MLA_HEADS, QK_PAD_DIM, tm), lambda i: (0, 0, i)),
            pl.BlockSpec((MLA_HEADS, n_sub, ATT_TILE, QK_PAD_DIM), lambda i: (0, i, 0, 0)),
            pl.BlockSpec((MLA_HEADS, n_sub, VT_ROWS, ATT_TILE), lambda i: (0, i, 0, 0)),
        ],
        out_shape=[
            jax.ShapeDtypeStruct((seq, SGU_WIDTH + POOL_WIDTH), jnp.bfloat16),
            jax.ShapeDtypeStruct((seq, MLA_WIDTH), jnp.bfloat16),
            jax.ShapeDtypeStruct((MLA_HEADS, QK_PAD_DIM, seq), jnp.bfloat16),
            jax.ShapeDtypeStruct((MLA_HEADS, seq // ATT_TILE, ATT_TILE, QK_PAD_DIM), jnp.bfloat16),
            jax.ShapeDtypeStruct((MLA_HEADS, seq // ATT_TILE, VT_ROWS, ATT_TILE), jnp.bfloat16),
        ],
        scratch_shapes=[
            pltpu.VMEM((D_MODEL, D_IN_PAD), jnp.bfloat16),
            pltpu.VMEM((tm, D_IN_PAD), jnp.float32),
            pltpu.VMEM((POOL_HALO + tm, POOL_WIDTH), jnp.float32),
            pltpu.VMEM((POOL_HALO + tm, POOL_WIDTH), jnp.float32),
            pltpu.VMEM((POOL_HALO + tm, POOL_WIDTH), jnp.float32),
            pltpu.VMEM((POOL_HALO + tm, POOL_WIDTH), jnp.float32),
        ],
        compiler_params=pltpu.CompilerParams(
            dimension_semantics=("arbitrary",), vmem_limit_bytes=VMEM_LIMIT_BYTES),
        name="mixer_in",
    )(x, w["pre_g"], w["w_in"], w["sgu_w"], w["sgu_bias"], w["ln_g"], w["ln_b"],
      w["pool_w"], w["pool_scale"], w["qn_g"], w["w_uq_t"], w["kvn_g"], w["w_k"], w["w_v_t"],
      cos_t, sin_t, cos_k, sin_k)


def _attention_kernel(qt_ref, qt_next_ref, k_hbm_ref, vt_hbm_ref, o_ref, k_ref, vt_ref, kv_sem, m_ref, acc_ref,
                      s0_ref, s1_ref, p0_ref, p1_ref, a0_ref, a1_ref, x0_ref, x1_ref):
    i = pl.program_id(1)
    last = pl.num_programs(1) - 1
    tk = ATT_TILE
    tq = ATT_Q_TILE
    n_heads = qt_ref.shape[0]
    n_tiles = (tq // tk) * (i + 1)
    s_refs = (s0_ref, s1_ref)
    p_refs = (p0_ref, p1_ref)
    a_refs = (a0_ref, a1_ref)
    x_refs = (x0_ref, x1_ref)
    acc_ref[...] = jnp.zeros(acc_ref.shape, jnp.float32)

    grp = pl.program_id(0)

    def kv_copies(first_tile):
        heads = pl.ds(grp * n_heads, n_heads)
        tiles = pl.ds(first_tile, tq // tk)
        return (pltpu.make_async_copy(k_hbm_ref.at[heads, tiles], k_ref.at[:, tiles], kv_sem.at[0]),
                pltpu.make_async_copy(vt_hbm_ref.at[heads, tiles], vt_ref.at[:, tiles], kv_sem.at[1]))

    @pl.when(i == 0)
    def _():
        for c in kv_copies(0):
            c.start()

    for c in kv_copies(n_tiles - tq // tk):
        c.wait()

    @pl.when(i < last)
    def _():
        for c in kv_copies(n_tiles):
            c.start()

    LATE = slice(tq // 2, tq)
    ALL = slice(0, tq)

    def score(slot, j, diag=None, q_ref=qt_ref):
        qs = LATE if diag == 1 else ALL
        for h in range(n_heads):
            s = jnp.dot(k_ref[h, j], q_ref[h, :, qs], preferred_element_type=jnp.float32)
            if diag is not None:
                key_chunk = (diag * tk + lax.broadcasted_iota(jnp.int32, s.shape, 0)) // CHUNK
                qry_chunk = (qs.start + lax.broadcasted_iota(jnp.int32, s.shape, 1)) // CHUNK
                s = jnp.where(key_chunk <= qry_chunk, s, NEG_INF)
            s_refs[slot][h, :, qs] = s
            x_refs[slot][h, :, qs] = jnp.max(s, axis=0, keepdims=True)

    def softmax(slot, first=False, qs=ALL):
        rc = SOFTMAX_ROWS
        for h in range(n_heads):
            m_old = jnp.full((1, tq), NEG_INF, jnp.float32) if first else m_ref[h, :, qs]
            m_new = jnp.maximum(m_old, x_refs[slot][h, :, qs])
            m_ref[h, :, qs] = m_new
            a_refs[slot][h, :, qs] = jnp.exp2(m_old - m_new)
            for r in range(0, tk, rc):
                d = s_refs[slot][h, r:r + rc, qs] - m_new
                p_refs[slot][h, r:r + rc, qs] = jnp.exp2(d.astype(jnp.bfloat16))

    def value(slot, j, qs=ALL):
        for h in range(n_heads):
            pv = jnp.dot(vt_ref[h, j], p_refs[slot][h, :, qs], preferred_element_type=jnp.float32)
            acc_ref[h, :, qs] = a_refs[slot][h, :, qs] * acc_ref[h, :, qs] + pv

    def pair(t0, masked):
        for u in range(2):
            score(u, t0 + u, diag=u if masked else None)
            softmax(1 - u)
            value(u, t0 + u - 2)

    def finish():
        for h in range(n_heads):
            o_t = acc_ref[h, 0:V_HEAD_DIM, :] * (1.0 / acc_ref[h, V_HEAD_DIM:V_HEAD_DIM + 1, :])
            o_ref[:, h * V_HEAD_DIM:(h + 1) * V_HEAD_DIM] = o_t.T.astype(o_ref.dtype)

    @pl.when(i == 0)
    def _():
        score(0, 0, diag=0)
        score(1, 1, diag=1)
        softmax(0, first=True)

    @pl.when(i > 0)
    def _():
        def pair_block(jj, carry):
            pair(2 + 2 * jj, masked=False)
            return carry

        lax.fori_loop(0, i - 1, pair_block, 0)
        pair(n_tiles - 2, masked=True)

    @pl.when(i < last)
    def _():
        score(0, 0, q_ref=qt_next_ref)
        softmax(1, qs=LATE)
        value(0, n_tiles - 2)
        score(1, 1, q_ref=qt_next_ref)
        value(1, n_tiles - 1, qs=LATE)
        finish()
        softmax(0, first=True)

    @pl.when(i == last)
    def _():
        softmax(1, qs=LATE)
        value(0, n_tiles - 2)
        value(1, n_tiles - 1, qs=LATE)
        finish()


def _attention(q_t, k, v_t):
    heads, n_kv, tk, _ = k.shape
    seq = n_kv * tk
    tq = ATT_Q_TILE
    hp = ATT_HEADS_PER_STEP
    n_q = seq // tq
    return pl.pallas_call(
        _attention_kernel,
        grid=(heads // hp, n_q),
        in_specs=[
            pl.BlockSpec((hp, QK_PAD_DIM, tq), lambda g, i: (g, 0, i)),
            pl.BlockSpec((hp, QK_PAD_DIM, tq), lambda g, i: (g, 0, jnp.minimum(i + 1, n_q - 1))),
            pl.BlockSpec(memory_space=pl.ANY),
            pl.BlockSpec(memory_space=pl.ANY),
        ],
        out_specs=pl.BlockSpec((tq, hp * V_HEAD_DIM), lambda g, i: (i, g)),
        out_shape=jax.ShapeDtypeStruct((seq, heads * V_HEAD_DIM), jnp.bfloat16),
        scratch_shapes=[
            pltpu.VMEM((hp, n_kv, tk, QK_PAD_DIM), jnp.bfloat16),
            pltpu.VMEM((hp, n_kv, VT_ROWS, tk), jnp.bfloat16),
            pltpu.SemaphoreType.DMA((2,)),
            pltpu.VMEM((hp, 1, tq), jnp.float32),
            pltpu.VMEM((hp, VT_ROWS, tq), jnp.float32),
            pltpu.VMEM((hp, tk, tq), jnp.float32),
            pltpu.VMEM((hp, tk, tq), jnp.float32),
            pltpu.VMEM((hp, tk, tq), jnp.bfloat16),
            pltpu.VMEM((hp, tk, tq), jnp.bfloat16),
            pltpu.VMEM((hp, 1, tq), jnp.float32),
            pltpu.VMEM((hp, 1, tq), jnp.float32),
            pltpu.VMEM((hp, 1, tq), jnp.float32),
            pltpu.VMEM((hp, 1, tq), jnp.float32),
        ],
        compiler_params=pltpu.CompilerParams(
            dimension_semantics=("arbitrary", "arbitrary"), vmem_limit_bytes=ATT_VMEM_LIMIT_BYTES),
        name="attention",
    )(q_t, q_t, k, v_t)


def _mixer_out_kernel(x_ref, yab_ref, o_ref, gate_ref, w_out_ref, post_g_ref, out_ref, w_bf_ref):
    @pl.when(pl.program_id(0) == 0)
    def _():
        w_bf_ref[...] = w_out_ref[...].astype(jnp.bfloat16)

    yc = (o_ref[...].astype(jnp.float32) * gate_ref[...].astype(jnp.float32)).astype(jnp.bfloat16)
    y = _dot(jnp.concatenate([yab_ref[...], yc], axis=1), w_bf_ref[...])
    out_ref[...] = x_ref[...] + _rms(y, post_g_ref[...])


def _mixer_out(x, yab, o, gate, w, layer):
    seq = x.shape[0]
    tm = OUT_ROW_TILE
    row = lambda width: pl.BlockSpec((tm, width), lambda i: (i, 0))
    return pl.pallas_call(
        _mixer_out_kernel,
        grid=(seq // tm,),
        in_specs=[
            row(D_MODEL), row(SGU_WIDTH + POOL_WIDTH), row(MLA_WIDTH), row(MLA_WIDTH),
            pl.BlockSpec((None, D_MODEL, D_MODEL), lambda i: (layer, 0, 0)),
            pl.BlockSpec((None, 1, D_MODEL), lambda i: (layer, 0, 0)),
        ],
        out_specs=row(D_MODEL),
        out_shape=jax.ShapeDtypeStruct((seq, D_MODEL), jnp.float32),
        scratch_shapes=[pltpu.VMEM((D_MODEL, D_MODEL), jnp.bfloat16)],
        compiler_params=pltpu.CompilerParams(
            dimension_semantics=("arbitrary",), vmem_limit_bytes=VMEM_LIMIT_BYTES),
        name="mixer_out",
    )(x, yab, o, gate, w["w_out"], w["post_g"])


def _prep_weights(pre_norm_g, post_norm_g, w_in, sgu_w, sgu_b, sgu_ln_g, sgu_ln_b, pool_w, pool_scale,
                  q_norm_g, w_uq, kv_norm_g, w_ukv, w_out):
    bf = jnp.bfloat16
    depth = w_in.shape[0]
    w_ukv_r = w_ukv.reshape(depth, KV_LORA_RANK, MLA_HEADS, QK_NOPE_DIM + V_HEAD_DIM)
    w_k = w_ukv_r[..., :QK_NOPE_DIM].reshape(depth, KV_LORA_RANK, MLA_HEADS * QK_NOPE_DIM)
    w_v = w_ukv_r[..., QK_NOPE_DIM:].reshape(depth, KV_LORA_RANK, MLA_HEADS * V_HEAD_DIM)
    groups = len(POOL_WINDOWS)
    same_group = jnp.eye(groups, dtype=bool)[None, :, None, :, None]
    pool_bd = jnp.where(same_group, pool_w[:, :, :, None, :], 0.0).reshape(depth, POOL_WIDTH, POOL_WIDTH)
    return {
        "pre_g": pre_norm_g.reshape(depth, 1, D_MODEL),
        "post_g": post_norm_g.reshape(depth, 1, D_MODEL),
        "w_in": w_in.astype(bf),
        "sgu_w": sgu_w.transpose(0, 2, 1, 3).reshape(depth, SGU_BLOCK, SGU_HEADS * SGU_BLOCK),
        "sgu_bias": jnp.repeat(sgu_b.transpose(0, 2, 1), SGU_HEAD_DIM, axis=2),
        "ln_g": sgu_ln_g.reshape(depth, 1, SGU_WIDTH),
        "ln_b": sgu_ln_b.reshape(depth, 1, SGU_WIDTH),
        "pool_w": pool_bd.astype(bf),
        "pool_scale": pool_scale.reshape(depth, 1, POOL_WIDTH),
        "qn_g": q_norm_g.reshape(depth, 1, Q_LORA_RANK),
        "w_uq_t": w_uq.transpose(0, 2, 1).astype(bf),
        "kvn_g": kv_norm_g.reshape(depth, 1, KV_LORA_RANK),
        "w_k": w_k.astype(bf),
        "w_v_t": w_v.transpose(0, 2, 1).astype(bf),
        "w_out": w_out,
    }


def kernel(x, positions, pre_norm_g, post_norm_g, w_in, sgu_w, sgu_b, sgu_ln_g, sgu_ln_b, pool_w, pool_scale,
           q_norm_g, w_uq, kv_norm_g, w_ukv, w_out):
    bsz, seq, d_model = x.shape
    assert bsz == 1 and seq == SEQ and d_model == D_MODEL
    assert seq % ROW_TILE == 0 and ROW_TILE % ATT_TILE == 0 and seq % TABLE_TILE == 0
    assert seq % OUT_ROW_TILE == 0 and seq % ATT_Q_TILE == 0
    tables = _rope_tables(positions)
    xs = x.reshape(seq, d_model)
    w = _prep_weights(pre_norm_g, post_norm_g, w_in, sgu_w, sgu_b, sgu_ln_g, sgu_ln_b, pool_w, pool_scale,
                      q_norm_g, w_uq, kv_norm_g, w_ukv, w_out)
    for layer in range(pre_norm_g.shape[0]):
        yab, gate, q_t, k, v_t = _mixer_in(xs, w, layer, tables)
        o = _attention(q_t, k, v_t)
        xs = _mixer_out(xs, yab, o, gate, w, layer)
    return xs.reshape(bsz, seq, d_model)
```

```python
import math

import jax
import jax.numpy as jnp
from jax import lax
from jax.experimental import pallas as pl
from jax.experimental.pallas import tpu as pltpu

D_MODEL = 1024
SEQ = 16384
CHUNK = 64
EPS = 1e-6
NEG_INF = -1e30

SGU_WIDTH = 256
SGU_HEADS = 4
SGU_HEAD_DIM = SGU_WIDTH // SGU_HEADS
SGU_BLOCK = 128

POOL_WIDTH = 256
POOL_WINDOWS = (2, 4, 8, 16)
POOL_GROUP_DIM = POOL_WIDTH // len(POOL_WINDOWS)

MLA_WIDTH = 512
MLA_HEADS = 4
V_HEAD_DIM = MLA_WIDTH // MLA_HEADS
QK_NOPE_DIM = 128
QK_ROPE_DIM = 64
QK_HEAD_DIM = QK_NOPE_DIM + QK_ROPE_DIM
Q_LORA_RANK = 384
KV_LORA_RANK = 256
ROPE_BASE = 10000.0
ROPE_HALF = QK_ROPE_DIM // 2

LANES = 128
SUBLANES = 8
BF16_SUBLANES = 16
MXU_DIM = 256
VMEM_LIMIT_BYTES = 48 * 1024 * 1024
ATT_VMEM_LIMIT_BYTES = 54 * 1024 * 1024

ROW_TILE = 512
OUT_ROW_TILE = 1024
ATT_TILE = 512
ATT_Q_TILE = 2 * ATT_TILE
ATT_HEADS_PER_STEP = 2
SOFTMAX_ROWS = 64
TABLE_TILE = 2048
QK_PAD_DIM = MXU_DIM
VT_ROWS = V_HEAD_DIM + BF16_SUBLANES
POOL_HALO = 32

OFF_U = 0
OFF_V = OFF_U + SGU_WIDTH
OFF_GA = OFF_V + SGU_WIDTH
OFF_PIN = OFF_GA + SGU_WIDTH
OFF_PG = OFF_PIN + POOL_WIDTH
OFF_CQ = OFF_PG + POOL_WIDTH
OFF_CKV = OFF_CQ + Q_LORA_RANK
OFF_KR = OFF_CKV + KV_LORA_RANK
OFF_MG = OFF_KR + QK_ROPE_DIM
D_IN = OFF_MG + MLA_WIDTH
D_IN_PAD = -(-D_IN // LANES) * LANES

Q_PRESCALE = (QK_HEAD_DIM ** -0.5) * math.log2(math.e)


def _silu(x):
    return x * (1.0 / (1.0 + jnp.exp(-x)))


def _rms(x, g):
    return x * lax.rsqrt(jnp.mean(x * x, axis=-1, keepdims=True) + EPS) * g


def _dot(a, b):
    return jnp.dot(a, b, preferred_element_type=jnp.float32)


def _dot_nt(a, b):
    return lax.dot_general(a, b, (((1,), (1,)), ((), ())), preferred_element_type=jnp.float32)


def _rope_tables_kernel(pos_row_ref, invf_col_ref, cos_t_ref, sin_t_ref, cos_k_ref, sin_k_ref):
    ang_t = invf_col_ref[...] * pos_row_ref[...].astype(jnp.float32)
    cos_t = jnp.cos(ang_t)
    sin_t = jnp.sin(ang_t)
    cos_t_ref[...] = cos_t
    sin_t_ref[...] = sin_t
    pad = jnp.zeros((cos_t.shape[1], LANES - QK_ROPE_DIM), jnp.float32)
    cos_k_ref[...] = jnp.concatenate([cos_t.T, cos_t.T, pad], axis=1)
    sin_k_ref[...] = jnp.concatenate([-sin_t.T, sin_t.T, pad], axis=1)


def _rope_tables(positions):
    seq = positions.shape[-1]
    inv_freq = ROPE_BASE ** (-jnp.arange(0, QK_ROPE_DIM, 2, dtype=jnp.float32) / QK_ROPE_DIM)
    invf_col = inv_freq.reshape(ROPE_HALF, 1)
    pos_row = positions.reshape(1, seq)
    n = seq // TABLE_TILE
    return pl.pallas_call(
        _rope_tables_kernel,
        grid=(n,),
        in_specs=[
            pl.BlockSpec((1, TABLE_TILE), lambda i: (0, i)),
            pl.BlockSpec((ROPE_HALF, 1), lambda i: (0, 0)),
        ],
        out_specs=[
            pl.BlockSpec((ROPE_HALF, TABLE_TILE), lambda i: (0, i)),
            pl.BlockSpec((ROPE_HALF, TABLE_TILE), lambda i: (0, i)),
            pl.BlockSpec((TABLE_TILE, LANES), lambda i: (i, 0)),
            pl.BlockSpec((TABLE_TILE, LANES), lambda i: (i, 0)),
        ],
        out_shape=[
            jax.ShapeDtypeStruct((ROPE_HALF, seq), jnp.float32),
            jax.ShapeDtypeStruct((ROPE_HALF, seq), jnp.float32),
            jax.ShapeDtypeStruct((seq, LANES), jnp.float32),
            jax.ShapeDtypeStruct((seq, LANES), jnp.float32),
        ],
        compiler_params=pltpu.CompilerParams(dimension_semantics=("arbitrary",)),
        name="rope_tables",
    )(pos_row, invf_col)


def _mixer_in_kernel(x_ref, pre_g_ref, w_in_ref, sgu_w_ref, sgu_bias_ref, ln_g_ref, ln_b_ref,
                     pool_w_ref, pool_scale_ref, qn_g_ref, w_uq_t_ref, kvn_g_ref, w_k_ref, w_v_t_ref,
                     cos_t_ref, sin_t_ref, cos_k_ref, sin_k_ref,
                     yab_ref, gate_ref, qt_ref, k_ref, vt_ref,
                     w_bf_ref, z_ref, ext_ref, a2_ref, a4_ref, a8_ref):
    i = pl.program_id(0)
    tm = x_ref.shape[0]

    def mix():
        def proj(off, width):
            return z_ref[:, off:off + width]

        v = proj(OFF_V, SGU_WIDTH)
        mu = jnp.mean(v, axis=-1, keepdims=True)
        vc = v - mu
        var = jnp.mean(vc * vc, axis=-1, keepdims=True)
        vn = vc * lax.rsqrt(var + EPS) * ln_g_ref[...] + ln_b_ref[...]
        w_rows = lax.broadcasted_iota(jnp.int32, (SGU_BLOCK, SGU_HEADS * SGU_BLOCK), 0)
        w_cols = lax.broadcasted_iota(jnp.int32, (SGU_BLOCK, SGU_HEADS * SGU_BLOCK), 1)
        w_keep = ((w_cols % SGU_BLOCK) // CHUNK) <= (w_rows // CHUNK)
        w_cat = jnp.where(w_keep, sgu_w_ref[...], 0.0).astype(jnp.bfloat16)
        head_of_col = lax.broadcasted_iota(jnp.int32, (SGU_BLOCK, SGU_WIDTH), 1) // SGU_HEAD_DIM
        n_blk = tm // SGU_BLOCK
        v_stacks = []
        for r in range(n_blk):
            vb = vn[r * SGU_BLOCK:(r + 1) * SGU_BLOCK, :]
            v_stacks.append(jnp.concatenate(
                [jnp.where(head_of_col == h, vb, 0.0) for h in range(SGU_HEADS)], axis=0).astype(jnp.bfloat16))
        mixed_wide = _dot(w_cat, jnp.concatenate(v_stacks, axis=1))
        mixed = jnp.concatenate(
            [mixed_wide[:, r * SGU_WIDTH:(r + 1) * SGU_WIDTH] + sgu_bias_ref[...] for r in range(n_blk)], axis=0)
        ya = proj(OFF_U, SGU_WIDTH) * mixed * _silu(proj(OFF_GA, SGU_WIDTH))
        yab_ref[:, 0:SGU_WIDTH] = ya.astype(yab_ref.dtype)

        p = proj(OFF_PIN, POOL_WIDTH)

        @pl.when(i == 0)
        def _():
            ext_ref[0:POOL_HALO, :] = jnp.zeros((POOL_HALO, POOL_WIDTH), jnp.float32)

        ext_ref[POOL_HALO:POOL_HALO + tm, :] = p
        end = POOL_HALO + tm
        a2_ref[8:end, :] = ext_ref[8:end, :] + ext_ref[7:end - 1, :]
        a4_ref[16:end, :] = a2_ref[16:end, :] + a2_ref[14:end - 2, :]
        a8_ref[24:end, :] = a4_ref[24:end, :] + a4_ref[20:end - 4, :]
        a16 = a8_ref[POOL_HALO:end, :] + a8_ref[POOL_HALO - 8:end - 8, :]
        group = lax.broadcasted_iota(jnp.int32, (tm, POOL_WIDTH), 1) // POOL_GROUP_DIM
        sums = jnp.where(group == 0, a2_ref[POOL_HALO:end, :],
                         jnp.where(group == 1, a4_ref[POOL_HALO:end, :],
                                   jnp.where(group == 2, a8_ref[POOL_HALO:end, :], a16)))
        window = jnp.where(group == 0, POOL_WINDOWS[0],
                           jnp.where(group == 1, POOL_WINDOWS[1],
                                     jnp.where(group == 2, POOL_WINDOWS[2], POOL_WINDOWS[3])))
        t_glob = i * tm + lax.broadcasted_iota(jnp.int32, (tm, POOL_WIDTH), 0)
        count = jnp.minimum(t_glob + 1, window).astype(jnp.float32)
        pooled = sums / count - p
        ext_ref[0:POOL_HALO, :] = ext_ref[tm:tm + POOL_HALO, :]
        pool_mixed = _dot(pooled.astype(jnp.bfloat16), pool_w_ref[...])
        yb = pool_mixed * pool_scale_ref[...] * _silu(proj(OFF_PG, POOL_WIDTH))
        yab_ref[:, SGU_WIDTH:SGU_WIDTH + POOL_WIDTH] = yb.astype(yab_ref.dtype)

        gate_ref[...] = _silu(proj(OFF_MG, MLA_WIDTH)).astype(gate_ref.dtype)

        cqn = _rms(proj(OFF_CQ, Q_LORA_RANK), qn_g_ref[...]).astype(jnp.bfloat16)
        q_t = _dot_nt(w_uq_t_ref[...], cqn) * Q_PRESCALE
        cos_t = cos_t_ref[...]
        sin_t = sin_t_ref[...]
        for h in range(MLA_HEADS):
            base = h * QK_HEAD_DIM
            x1 = q_t[base + QK_NOPE_DIM:base + QK_NOPE_DIM + ROPE_HALF, :]
            x2 = q_t[base + QK_NOPE_DIM + ROPE_HALF:base + QK_HEAD_DIM, :]
            qt_ref[h, 0:QK_NOPE_DIM, :] = q_t[base:base + QK_NOPE_DIM, :].astype(qt_ref.dtype)
            qt_ref[h, QK_NOPE_DIM:QK_NOPE_DIM + ROPE_HALF, :] = (x1 * cos_t - x2 * sin_t).astype(qt_ref.dtype)
            qt_ref[h, QK_NOPE_DIM + ROPE_HALF:QK_HEAD_DIM, :] = (x2 * cos_t + x1 * sin_t).astype(qt_ref.dtype)
            qt_ref[h, QK_HEAD_DIM:QK_PAD_DIM, :] = jnp.zeros((QK_PAD_DIM - QK_HEAD_DIM, tm), qt_ref.dtype)

        ckvn = _rms(proj(OFF_CKV, KV_LORA_RANK), kvn_g_ref[...]).astype(jnp.bfloat16)
        k_nope = _dot(ckvn, w_k_ref[...])
        v_t = _dot_nt(w_v_t_ref[...], ckvn)
        grp = proj(OFF_KR, LANES)
        lane = lax.broadcasted_iota(jnp.int32, grp.shape, 1)
        partner = jnp.where(lane < ROPE_HALF,
                            pltpu.roll(grp, LANES - ROPE_HALF, axis=1),
                            pltpu.roll(grp, ROPE_HALF, axis=1))
        roped = grp * cos_k_ref[...] + partner * sin_k_ref[...]
        k_pe = jnp.where(lane < QK_ROPE_DIM, roped, 0.0).astype(k_ref.dtype)
        n_sub = tm // ATT_TILE
        for h in range(MLA_HEADS):
            for c in range(n_sub):
                rows = slice(c * ATT_TILE, (c + 1) * ATT_TILE)
                k_ref[h, c, :, 0:QK_NOPE_DIM] = k_nope[rows, h * QK_NOPE_DIM:(h + 1) * QK_NOPE_DIM].astype(k_ref.dtype)
                k_ref[h, c, :, QK_NOPE_DIM:QK_PAD_DIM] = k_pe[rows, :]
                vt_ref[h, c, 0:V_HEAD_DIM, :] = v_t[h * V_HEAD_DIM:(h + 1) * V_HEAD_DIM, rows].astype(vt_ref.dtype)
                ones_row = lax.broadcasted_iota(jnp.int32, (VT_ROWS - V_HEAD_DIM, ATT_TILE), 0) == 0
                vt_ref[h, c, V_HEAD_DIM:VT_ROWS, :] = jnp.where(ones_row, 1.0, 0.0).astype(vt_ref.dtype)

    @pl.when(i == 0)
    def _():
        aligned = (D_IN // LANES) * LANES
        w_bf_ref[:, 0:aligned] = w_in_ref[:, 0:aligned].astype(jnp.bfloat16)
        tail = jnp.concatenate(
            [w_in_ref[:, aligned:D_IN], jnp.zeros((D_MODEL, D_IN_PAD - D_IN), jnp.float32)], axis=1)
        w_bf_ref[:, aligned:D_IN_PAD] = tail.astype(jnp.bfloat16)

    z_ref[...] = _dot(_rms(x_ref[...], pre_g_ref[...]).astype(jnp.bfloat16), w_bf_ref[...])
    mix()


def _mixer_in(x, w, layer, tables):
    seq = x.shape[0]
    tm = ROW_TILE
    n = seq // tm
    n_sub = tm // ATT_TILE
    cos_t, sin_t, cos_k, sin_k = tables

    def const(shape):
        return pl.BlockSpec((None,) + shape, lambda i: (layer,) + (0,) * len(shape))

    return pl.pallas_call(
        _mixer_in_kernel,
        grid=(n,),
        in_specs=[
            pl.BlockSpec((tm, D_MODEL), lambda i: (i, 0)),
            const((1, D_MODEL)),
            pl.BlockSpec((None, D_MODEL, D_IN), lambda i: (layer, 0, 0), pipeline_mode=pl.Buffered(1)),
            const((SGU_BLOCK, SGU_HEADS * SGU_BLOCK)),
            const((SGU_BLOCK, SGU_WIDTH)),
            const((1, SGU_WIDTH)),
            const((1, SGU_WIDTH)),
            const((POOL_WIDTH, POOL_WIDTH)),
            const((1, POOL_WIDTH)),
            const((1, Q_LORA_RANK)),
            const((MLA_HEADS * QK_HEAD_DIM, Q_LORA_RANK)),
            const((1, KV_LORA_RANK)),
            const((KV_LORA_RANK, MLA_HEADS * QK_NOPE_DIM)),
            const((MLA_HEADS * V_HEAD_DIM, KV_LORA_RANK)),
            pl.BlockSpec((ROPE_HALF, tm), lambda i: (0, i)),
            pl.BlockSpec((ROPE_HALF, tm), lambda i: (0, i)),
            pl.BlockSpec((tm, LANES), lambda i: (i, 0)),
            pl.BlockSpec((tm, LANES), lambda i: (i, 0)),
        ],
        out_specs=[
            pl.BlockSpec((tm, SGU_WIDTH + POOL_WIDTH), lambda i: (i, 0)),
            pl.BlockSpec((tm, MLA_WIDTH), lambda i: (i, 0)),
            pl.BlockSpec((MLA_HEADS, None, QK_PAD_DIM, tm), lambda i: (0, i, 0, 0)),
            pl.BlockSpec((MLA_HEADS, n_sub, ATT_TILE, QK_PAD_DIM), lambda i: (0, i, 0, 0)),
            pl.BlockSpec((MLA_HEADS, n_sub, VT_ROWS, ATT_TILE), lambda i: (0, i, 0, 0)),
        ],
        out_shape=[
            jax.ShapeDtypeStruct((seq, SGU_WIDTH + POOL_WIDTH), jnp.bfloat16),
            jax.ShapeDtypeStruct((seq, MLA_WIDTH), jnp.bfloat16),
            jax.ShapeDtypeStruct((MLA_HEADS, seq // tm, QK_PAD_DIM, tm), jnp.bfloat16),
            jax.ShapeDtypeStruct((MLA_HEADS, seq // ATT_TILE, ATT_TILE, QK_PAD_DIM), jnp.bfloat16),
            jax.ShapeDtypeStruct((MLA_HEADS, seq // ATT_TILE, VT_ROWS, ATT_TILE), jnp.bfloat16),
        ],
        scratch_shapes=[
            pltpu.VMEM((D_MODEL, D_IN_PAD), jnp.bfloat16),
            pltpu.VMEM((tm, D_IN_PAD), jnp.float32),
            pltpu.VMEM((POOL_HALO + tm, POOL_WIDTH), jnp.float32),
            pltpu.VMEM((POOL_HALO + tm, POOL_WIDTH), jnp.float32),
            pltpu.VMEM((POOL_HALO + tm, POOL_WIDTH), jnp.float32),
            pltpu.VMEM((POOL_HALO + tm, POOL_WIDTH), jnp.float32),
        ],
        compiler_params=pltpu.CompilerParams(
            dimension_semantics=("arbitrary",), vmem_limit_bytes=VMEM_LIMIT_BYTES),
        name="mixer_in",
    )(x, w["pre_g"], w["w_in"], w["sgu_w"], w["sgu_bias"], w["ln_g"], w["ln_b"],
      w["pool_w"], w["pool_scale"], w["qn_g"], w["w_uq_t"], w["kvn_g"], w["w_k"], w["w_v_t"],
      cos_t, sin_t, cos_k, sin_k)


def _attention_kernel(qt_ref, qt_next_ref, k_hbm_ref, vt_hbm_ref, o_ref, k_ref, vt_ref, kv_sem, m_ref, acc_ref,
                      s0_ref, s1_ref, p0_ref, p1_ref, a0_ref, a1_ref, x0_ref, x1_ref):
    i = pl.program_id(1)
    last = pl.num_programs(1) - 1
    tk = ATT_TILE
    tq = ATT_Q_TILE
    n_heads = qt_ref.shape[0]
    n_tiles = (tq // tk) * (i + 1)
    s_refs = (s0_ref, s1_ref)
    p_refs = (p0_ref, p1_ref)
    a_refs = (a0_ref, a1_ref)
    x_refs = (x0_ref, x1_ref)
    acc_ref[...] = jnp.zeros(acc_ref.shape, jnp.float32)

    grp = pl.program_id(0)

    def kv_copies(first_tile):
        heads = pl.ds(grp * n_heads, n_heads)
        tiles = pl.ds(first_tile, tq // tk)
        return (pltpu.make_async_copy(k_hbm_ref.at[heads, tiles], k_ref.at[:, tiles], kv_sem.at[0]),
                pltpu.make_async_copy(vt_hbm_ref.at[heads, tiles], vt_ref.at[:, tiles], kv_sem.at[1]))

    @pl.when(i == 0)
    def _():
        for c in kv_copies(0):
            c.start()

    for c in kv_copies(n_tiles - tq // tk):
        c.wait()

    @pl.when(i < last)
    def _():
        for c in kv_copies(n_tiles):
            c.start()

    LATE = slice(tq // 2, tq)
    ALL = slice(0, tq)

    def score(slot, j, diag=None, q_ref=qt_ref):
        qs = LATE if diag == 1 else ALL
        for h in range(n_heads):
            n_chunks = tq // tk
            q = q_ref[h, n_chunks - 1] if diag == 1 else jnp.concatenate(
                [q_ref[h, c] for c in range(n_chunks)], axis=1)
            s = jnp.dot(k_ref[h, j], q, preferred_element_type=jnp.float32)
            if diag is not None:
                key_chunk = (diag * tk + lax.broadcasted_iota(jnp.int32, s.shape, 0)) // CHUNK
                qry_chunk = (qs.start + lax.broadcasted_iota(jnp.int32, s.shape, 1)) // CHUNK
                s = jnp.where(key_chunk <= qry_chunk, s, NEG_INF)
            s_refs[slot][h, :, qs] = s
            x_refs[slot][h, :, qs] = jnp.max(s, axis=0, keepdims=True)

    def softmax(slot, first=False, qs=ALL):
        rc = SOFTMAX_ROWS
        for h in range(n_heads):
            m_old = jnp.full((1, tq), NEG_INF, jnp.float32) if first else m_ref[h, :, qs]
            m_new = jnp.maximum(m_old, x_refs[slot][h, :, qs])
            m_ref[h, :, qs] = m_new
            a_refs[slot][h, :, qs] = jnp.exp2(m_old - m_new)
            for r in range(0, tk, rc):
                d = s_refs[slot][h, r:r + rc, qs] - m_new
                p_refs[slot][h, r:r + rc, qs] = jnp.exp2(d.astype(jnp.bfloat16))

    def value(slot, j, qs=ALL):
        for h in range(n_heads):
            pv = jnp.dot(vt_ref[h, j], p_refs[slot][h, :, qs], preferred_element_type=jnp.float32)
            acc_ref[h, :, qs] = a_refs[slot][h, :, qs] * acc_ref[h, :, qs] + pv

    def pair(t0, masked):
        for u in range(2):
            score(u, t0 + u, diag=u if masked else None)
            softmax(1 - u)
            value(u, t0 + u - 2)

    def finish():
        for h in range(n_heads):
            o_t = acc_ref[h, 0:V_HEAD_DIM, :] * (1.0 / acc_ref[h, V_HEAD_DIM:V_HEAD_DIM + 1, :])
            o_ref[:, h * V_HEAD_DIM:(h + 1) * V_HEAD_DIM] = o_t.T.astype(o_ref.dtype)

    @pl.when(i == 0)
    def _():
        score(0, 0, diag=0)
        score(1, 1, diag=1)
        softmax(0, first=True)

    @pl.when(i > 0)
    def _():
        def pair_block(jj, carry):
            pair(2 + 2 * jj, masked=False)
            return carry

        lax.fori_loop(0, i - 1, pair_block, 0)
        pair(n_tiles - 2, masked=True)

    @pl.when(i < last)
    def _():
        score(0, 0, q_ref=qt_next_ref)
        softmax(1, qs=LATE)
        value(0, n_tiles - 2)
        score(1, 1, q_ref=qt_next_ref)
        value(1, n_tiles - 1, qs=LATE)
        finish()
        softmax(0, first=True)

    @pl.when(i == last)
    def _():
        softmax(1, qs=LATE)
        value(0, n_tiles - 2)
        value(1, n_tiles - 1, qs=LATE)
        finish()


def _attention(q_t, k, v_t):
    heads, n_kv, tk, _ = k.shape
    seq = n_kv * tk
    tq = ATT_Q_TILE
    hp = ATT_HEADS_PER_STEP
    n_q = seq // tq
    return pl.pallas_call(
        _attention_kernel,
        grid=(heads // hp, n_q),
        in_specs=[
            pl.BlockSpec((hp, tq // tk, QK_PAD_DIM, tk), lambda g, i: (g, i, 0, 0)),
            pl.BlockSpec((hp, tq // tk, QK_PAD_DIM, tk), lambda g, i: (g, jnp.minimum(i + 1, n_q - 1), 0, 0)),
            pl.BlockSpec(memory_space=pl.ANY),
            pl.BlockSpec(memory_space=pl.ANY),
        ],
        out_specs=pl.BlockSpec((tq, hp * V_HEAD_DIM), lambda g, i: (i, g)),
        out_shape=jax.ShapeDtypeStruct((seq, heads * V_HEAD_DIM), jnp.bfloat16),
        scratch_shapes=[
            pltpu.VMEM((hp, n_kv, tk, QK_PAD_DIM), jnp.bfloat16),
            pltpu.VMEM((hp, n_kv, VT_ROWS, tk), jnp.bfloat16),
            pltpu.SemaphoreType.DMA((2,)),
            pltpu.VMEM((hp, 1, tq), jnp.float32),
            pltpu.VMEM((hp, VT_ROWS, tq), jnp.float32),
            pltpu.VMEM((hp, tk, tq), jnp.float32),
            pltpu.VMEM((hp, tk, tq), jnp.float32),
            pltpu.VMEM((hp, tk, tq), jnp.bfloat16),
            pltpu.VMEM((hp, tk, tq), jnp.bfloat16),
            pltpu.VMEM((hp, 1, tq), jnp.float32),
            pltpu.VMEM((hp, 1, tq), jnp.float32),
            pltpu.VMEM((hp, 1, tq), jnp.float32),
            pltpu.VMEM((hp, 1, tq), jnp.float32),
        ],
        compiler_params=pltpu.CompilerParams(
            dimension_semantics=("arbitrary", "arbitrary"), vmem_limit_bytes=ATT_VMEM_LIMIT_BYTES),
        name="attention",
    )(q_t, q_t, k, v_t)


def _mixer_out_kernel(x_ref, yab_ref, o_ref, gate_ref, w_out_ref, post_g_ref, out_ref, w_bf_ref):
    @pl.when(pl.program_id(0) == 0)
    def _():
        w_bf_ref[...] = w_out_ref[...].astype(jnp.bfloat16)

    yc = (o_ref[...].astype(jnp.float32) * gate_ref[...].astype(jnp.float32)).astype(jnp.bfloat16)
    y = _dot(jnp.concatenate([yab_ref[...], yc], axis=1), w_bf_ref[...])
    out_ref[...] = x_ref[...] + _rms(y, post_g_ref[...])


def _mixer_out(x, yab, o, gate, w, layer):
    seq = x.shape[0]
    tm = OUT_ROW_TILE
    row = lambda width: pl.BlockSpec((tm, width), lambda i: (i, 0))
    return pl.pallas_call(
        _mixer_out_kernel,
        grid=(seq // tm,),
        in_specs=[
            row(D_MODEL), row(SGU_WIDTH + POOL_WIDTH), row(MLA_WIDTH), row(MLA_WIDTH),
            pl.BlockSpec((None, D_MODEL, D_MODEL), lambda i: (layer, 0, 0)),
            pl.BlockSpec((None, 1, D_MODEL), lambda i: (layer, 0, 0)),
        ],
        out_specs=row(D_MODEL),
        out_shape=jax.ShapeDtypeStruct((seq, D_MODEL), jnp.float32),
        scratch_shapes=[pltpu.VMEM((D_MODEL, D_MODEL), jnp.bfloat16)],
        compiler_params=pltpu.CompilerParams(
            dimension_semantics=("arbitrary",), vmem_limit_bytes=VMEM_LIMIT_BYTES),
        name="mixer_out",
    )(x, yab, o, gate, w["w_out"], w["post_g"])


def _prep_weights(pre_norm_g, post_norm_g, w_in, sgu_w, sgu_b, sgu_ln_g, sgu_ln_b, pool_w, pool_scale,
                  q_norm_g, w_uq, kv_norm_g, w_ukv, w_out):
    bf = jnp.bfloat16
    depth = w_in.shape[0]
    w_ukv_r = w_ukv.reshape(depth, KV_LORA_RANK, MLA_HEADS, QK_NOPE_DIM + V_HEAD_DIM)
    w_k = w_ukv_r[..., :QK_NOPE_DIM].reshape(depth, KV_LORA_RANK, MLA_HEADS * QK_NOPE_DIM)
    w_v = w_ukv_r[..., QK_NOPE_DIM:].reshape(depth, KV_LORA_RANK, MLA_HEADS * V_HEAD_DIM)
    groups = len(POOL_WINDOWS)
    same_group = jnp.eye(groups, dtype=bool)[None, :, None, :, None]
    pool_bd = jnp.where(same_group, pool_w[:, :, :, None, :], 0.0).reshape(depth, POOL_WIDTH, POOL_WIDTH)
    return {
        "pre_g": pre_norm_g.reshape(depth, 1, D_MODEL),
        "post_g": post_norm_g.reshape(depth, 1, D_MODEL),
        "w_in": w_in,
        "sgu_w": sgu_w.transpose(0, 2, 1, 3).reshape(depth, SGU_BLOCK, SGU_HEADS * SGU_BLOCK),
        "sgu_bias": jnp.repeat(sgu_b.transpose(0, 2, 1), SGU_HEAD_DIM, axis=2),
        "ln_g": sgu_ln_g.reshape(depth, 1, SGU_WIDTH),
        "ln_b": sgu_ln_b.reshape(depth, 1, SGU_WIDTH),
        "pool_w": pool_bd.astype(bf),
        "pool_scale": pool_scale.reshape(depth, 1, POOL_WIDTH),
        "qn_g": q_norm_g.reshape(depth, 1, Q_LORA_RANK),
        "w_uq_t": w_uq.transpose(0, 2, 1).astype(bf),
        "kvn_g": kv_norm_g.reshape(depth, 1, KV_LORA_RANK),
        "w_k": w_k.astype(bf),
        "w_v_t": w_v.transpose(0, 2, 1).astype(bf),
        "w_out": w_out,
    }


def kernel(x, positions, pre_norm_g, post_norm_g, w_in, sgu_w, sgu_b, sgu_ln_g, sgu_ln_b, pool_w, pool_scale,
           q_norm_g, w_uq, kv_norm_g, w_ukv, w_out):
    bsz, seq, d_model = x.shape
    assert bsz == 1 and seq == SEQ and d_model == D_MODEL
    assert seq % ROW_TILE == 0 and ROW_TILE % ATT_TILE == 0 and seq % TABLE_TILE == 0
    assert seq % OUT_ROW_TILE == 0 and seq % ATT_Q_TILE == 0
    assert ROW_TILE == ATT_TILE
    tables = _rope_tables(positions)
    xs = x.reshape(seq, d_model)
    w = _prep_weights(pre_norm_g, post_norm_g, w_in, sgu_w, sgu_b, sgu_ln_g, sgu_ln_b, pool_w, pool_scale,
                      q_norm_g, w_uq, kv_norm_g, w_ukv, w_out)
    for layer in range(pre_norm_g.shape[0]):
        yab, gate, q_t, k, v_t = _mixer_in(xs, w, layer, tables)
        o = _attention(q_t, k, v_t)
        xs = _mixer_out(xs, yab, o, gate, w, layer)
    return xs.reshape(bsz, seq, d_model)
```
